```python
import math
import jax
import jax.numpy as jnp
from jax import lax
import numpy as np

D_MODEL = 1024
BATCH = 8
SEQ = 8192
DEPTH = 4
DEC_BATCH = 8
DEC_SEQ = 32
PAST_LEN = 2048

CHUNK = 64
EPS = 1e-6
D_FF = 2816
RET_HEADS = 4
RET_DK = 64
RET_DV = 128
ROPE_BASE = 10000.0
GDN_HEADS = 8
GDN_DK = 64
GDN_DV = 64
CONV_W = 4
RWKV_HEADS = 8
RWKV_N = 64
W_LORA = 64
A_LORA = 64
G_LORA = 128
RWKV_GN_EPS = 64e-5
N_BRANCH = 3
BRANCH_W = 512
RET_QK = RET_HEADS * RET_DK
RET_W = RET_HEADS * RET_DV
GDN_QK = GDN_HEADS * GDN_DK
GDN_V = GDN_HEADS * GDN_DV
GDN_CONV_CH = 2 * GDN_QK + GDN_V
RWKV_W = RWKV_HEADS * RWKV_N
RWKV_SHIFT_CH = 3 * RWKV_W + W_LORA + A_LORA + G_LORA
IN_SIZES = (RET_QK, RET_QK, RET_W, RET_W, GDN_CONV_CH, GDN_V, GDN_HEADS, GDN_HEADS, RWKV_SHIFT_CH, N_BRANCH * D_MODEL)
D_IN = sum(IN_SIZES)

kernel_name = 'hybrid_retention_gdn_rwkv7_stream_step'


def _split_points(sizes):
    pts, acc = [], 0
    for s in sizes[:-1]:
        acc += s
        pts.append(acc)
    return pts


def rms_norm(x, g):
    xf = x.astype(jnp.float32)
    y = xf * lax.rsqrt(jnp.mean(xf * xf, axis=-1, keepdims=True) + EPS)
    return (y * g.astype(jnp.float32)).astype(x.dtype)


def l2norm(x):
    return x * lax.rsqrt(jnp.sum(x * x, axis=-1, keepdims=True) + 1e-6)


def group_norm(o, g, eps):
    mu = jnp.mean(o, axis=-1, keepdims=True)
    var = jnp.mean(jnp.square(o - mu), axis=-1, keepdims=True)
    y = (o - mu) * lax.rsqrt(var + eps)
    return y.reshape(o.shape[0], o.shape[1], -1) * g.astype(jnp.float32)


def swiglu(x, w13, w2):
    a, b = jnp.split(x @ w13, 2, axis=-1)
    return (jax.nn.silu(a) * b) @ w2


def rope(x, pos):
    half = x.shape[-1] // 2
    inv = ROPE_BASE ** (-jnp.arange(half, dtype=jnp.float32) / half)
    ang = pos.astype(jnp.float32)[:, None] * inv[None, :]
    cos = jnp.cos(ang)[None, :, None, :]
    sin = jnp.sin(ang)[None, :, None, :]
    x1, x2 = x[..., :half], x[..., half:]
    return jnp.concatenate([x1 * cos - x2 * sin, x1 * sin + x2 * cos], axis=-1)


def causal_conv(x, buf, w):
    L = x.shape[1]
    xp = jnp.concatenate([buf.astype(x.dtype), x], axis=1)
    y = xp[:, 0:L] * w[0]
    for i in range(1, CONV_W):
        y = y + xp[:, i:i + L] * w[i]
    return y, xp[:, L:]


def token_shift(z, prev, mu):
    zp = jnp.concatenate([prev.astype(z.dtype), z[:, :-1]], axis=1)
    return z + (zp - z) * mu, z[:, -1:]


def retention(q, k, v, s0, chunk):
    B, L, H, dk = q.shape
    dv = v.shape[-1]
    n = L // chunk
    log_g = jnp.log1p(-jnp.exp2(-5.0 - jnp.arange(H, dtype=jnp.float32)))
    idx = jnp.arange(chunk, dtype=jnp.float32)
    diff = idx[:, None] - idx[None, :]
    causal = diff >= 0
    dmask = jnp.where(causal, jnp.exp(jnp.where(causal, diff, 0.0)[None] * log_g[:, None, None]), 0.0)
    qc = q.reshape(B, n, chunk, H, dk)
    kc = k.reshape(B, n, chunk, H, dk)
    vc = v.reshape(B, n, chunk, H, dv)
    scores = jnp.einsum('bnihd,bnjhd->bnhij', qc, kc) * dmask
    intra = jnp.einsum('bnhij,bnjhe->bnihe', scores, vc)
    k_dec = jnp.exp((chunk - 1.0 - idx)[:, None] * log_g[None, :])
    kv = jnp.einsum('bnjhd,jh,bnjhe->nbhde', kc, k_dec, vc)
    c_dec = jnp.exp(chunk * log_g)[None, :, None, None]

    def step(S, kv_n):
        return c_dec * S + kv_n, S

    s_last, s_prev = lax.scan(step, s0, kv)
    q_dec = jnp.exp((idx + 1.0)[:, None] * log_g[None, :])
    inter = jnp.einsum('bnihd,ih,nbhde->bnihe', qc, q_dec, s_prev)
    return (intra + inter).reshape(B, L, H, dv), s_last


def gated_delta(q, k, v, log_a, beta, s0, chunk):
    B, L, H, dk = q.shape
    dv = v.shape[-1]
    n = L // chunk

    def blk(t):
        return jnp.moveaxis(t.reshape(B, n, chunk, *t.shape[2:]), 3, 2)

    qc, kc, vc, bc = blk(q), blk(k), blk(v), blk(beta)
    ga = jnp.cumsum(blk(log_a), axis=-1)
    incl = jnp.tril(jnp.ones((chunk, chunk), dtype=bool))
    strict = jnp.tril(jnp.ones((chunk, chunk), dtype=bool), -1)
    gdiff = ga[..., :, None] - ga[..., None, :]
    gam = jnp.where(incl, jnp.exp(jnp.where(incl, gdiff, 0.0)), 0.0)
    kk = jnp.einsum('bnhid,bnhjd->bnhij', kc, kc)
    a_mat = jnp.where(strict, bc[..., :, None] * kk * gam, 0.0)
    lmat = a_mat + jnp.eye(chunk, dtype=a_mat.dtype)
    rhs = jnp.concatenate([bc[..., None] * vc, (bc * jnp.exp(ga))[..., None] * kc], axis=-1)
    sol = lax.linalg.triangular_solve(lmat, rhs, left_side=True, lower=True, unit_diagonal=True)
    u0, wmat = sol[..., :dv], sol[..., dv:]
    pmat = jnp.einsum('bnhid,bnhjd->bnhij', qc, kc) * gam
    qg = qc * jnp.exp(ga)[..., None]
    kd = kc * jnp.exp(ga[..., -1:] - ga)[..., None]
    gc = jnp.exp(ga[..., -1])[..., None, None]

    def step(S, xs):
        u0n, wn, pn, qgn, kdn, gcn = xs
        u = u0n - jnp.einsum('bhcd,bhde->bhce', wn, S)
        o = jnp.einsum('bhcd,bhde->bhce', qgn, S) + jnp.einsum('bhij,bhje->bhie', pn, u)
        S = gcn * S + jnp.einsum('bhcd,bhce->bhde', kdn, u)
        return S, o

    xs = (jnp.moveaxis(u0, 1, 0), jnp.moveaxis(wmat, 1, 0), jnp.moveaxis(pmat, 1, 0),
          jnp.moveaxis(qg, 1, 0), jnp.moveaxis(kd, 1, 0), jnp.moveaxis(gc, 1, 0))
    s_last, o = lax.scan(step, s0, xs)
    o = jnp.transpose(o, (1, 0, 3, 2, 4)).reshape(B, L, H, dv)
    return o, s_last


def rwkv7_recur(r, wd, k, v, kk, a, s0):
    def step(S, xs):
        r_t, w_t, k_t, v_t, kk_t, a_t = xs
        sa = jnp.einsum('bhvk,bhk->bhv', S, kk_t)
        S = S * w_t[:, :, None, :] - sa[..., None] * (kk_t * a_t)[:, :, None, :] + v_t[..., None] * k_t[:, :, None, :]
        return S, jnp.einsum('bhvk,bhk->bhv', S, r_t)

    xs = (jnp.moveaxis(r, 1, 0), jnp.moveaxis(wd, 1, 0), jnp.moveaxis(k, 1, 0),
          jnp.moveaxis(v, 1, 0), jnp.moveaxis(kk, 1, 0), jnp.moveaxis(a, 1, 0))
    s_last, o = lax.scan(step, s0, xs)
    return jnp.moveaxis(o, 0, 1), s_last


def token_mixer(h, pos0, st, lp):
    B, L, _ = h.shape
    chunk = min(CHUNK, L)
    f32 = jnp.float32
    dt = h.dtype
    s_ret, s_delta, s_conv, s_wkv, s_shift = st
    z = h @ lp['w_in']
    a_q, a_k, a_v, a_g, b_qkv, b_g, b_beta, b_alpha, c_z, gate = jnp.split(z, _split_points(IN_SIZES), axis=-1)

    pos = pos0 + jnp.arange(L)
    q = rope(a_q.astype(f32).reshape(B, L, RET_HEADS, RET_DK), pos)
    k = rope(a_k.astype(f32).reshape(B, L, RET_HEADS, RET_DK), pos) * (RET_DK ** -0.5)
    v = a_v.astype(f32).reshape(B, L, RET_HEADS, RET_DV)
    o, s_ret_new = retention(q, k, v, s_ret.astype(f32), chunk)
    o_a = group_norm(o, lp['ret_gn'], 1e-5) * jax.nn.silu(a_g.astype(f32))

    qkv, s_conv_new = causal_conv(b_qkv, s_conv, lp['gdn_conv'])
    qkv = jax.nn.silu(qkv.astype(f32))
    q, k, v = jnp.split(qkv, [GDN_QK, 2 * GDN_QK], axis=-1)
    q = l2norm(q.reshape(B, L, GDN_HEADS, GDN_DK)) * (GDN_DK ** -0.5)
    k = l2norm(k.reshape(B, L, GDN_HEADS, GDN_DK))
    v = v.reshape(B, L, GDN_HEADS, GDN_DV)
    beta = jax.nn.sigmoid(b_beta.astype(f32))
    log_a = -jnp.exp(lp['gdn_a_log'].astype(f32)) * jax.nn.softplus(b_alpha.astype(f32) + lp['gdn_dt_bias'].astype(f32))
    o, s_delta_new = gated_delta(q, k, v, log_a, beta, s_delta.astype(f32), chunk)
    o = o * lax.rsqrt(jnp.mean(o * o, axis=-1, keepdims=True) + EPS) * lp['gdn_norm'].astype(f32)
    o_b = o.reshape(B, L, GDN_V) * jax.nn.silu(b_g.astype(f32))

    cs, s_shift_new = token_shift(c_z, s_shift, lp['rwkv_mu'])
    cs = cs.astype(f32)
    r, k, v, zw, za, zg = jnp.split(cs, [RWKV_W, 2 * RWKV_W, 3 * RWKV_W, 3 * RWKV_W + W_LORA, 3 * RWKV_W + W_LORA + A_LORA], axis=-1)
    w = -jax.nn.softplus(-(lp['rwkv_w0'] + jnp.tanh(zw) @ lp['rwkv_w2'])) - 0.5
    wd = jnp.exp(-jnp.exp(w))
    a = jax.nn.sigmoid(lp['rwkv_a0'] + za @ lp['rwkv_a2'])
    g = jax.nn.sigmoid(zg) @ lp['rwkv_g2']
    kk = l2norm((k * lp['rwkv_kk']).reshape(B, L, RWKV_HEADS, RWKV_N))
    k = k * (1.0 + (a - 1.0) * lp['rwkv_ka'])
    hs = (B, L, RWKV_HEADS, RWKV_N)
    rh, kh, vh = r.reshape(hs), k.reshape(hs), v.reshape(hs)
    o, s_wkv_new = rwkv7_recur(rh, wd.reshape(hs), kh, vh, kk, a.reshape(hs), s_wkv.astype(f32))
    bonus = jnp.sum(rh * kh * lp['rwkv_rk'].astype(f32).reshape(RWKV_HEADS, RWKV_N), axis=-1, keepdims=True) * vh
    o_c = (group_norm(o, lp['rwkv_gn'], RWKV_GN_EPS) + bonus.reshape(B, L, RWKV_W)) * g

    gates = jax.nn.sigmoid(gate.astype(f32)).reshape(B, L, N_BRANCH, D_MODEL)
    m = 0.0
    for i, o_i in enumerate((o_a, o_b, o_c)):
        m = m + gates[:, :, i] * (o_i.astype(dt) @ lp['w_branch'][i])
    y = m.astype(dt) @ lp['w_out']
    return y, (s_ret_new, s_delta_new, s_conv_new, s_wkv_new, s_shift_new)


def layer_forward(x, pos0, st, lp):
    x = x + 0.5 * swiglu(rms_norm(x, lp['norm_ffn1']), lp['ffn1_w13'], lp['ffn1_w2'])
    y, st = token_mixer(rms_norm(x, lp['norm_mix']), pos0, st, lp)
    x = x + y
    x = x + 0.5 * swiglu(rms_norm(x, lp['norm_ffn2']), lp['ffn2_w13'], lp['ffn2_w2'])
    return x, st


def setup_inputs(seed: int = 0) -> dict:
    key = jax.random.key(seed)
    keys = jax.random.split(key, 40)
    f32 = jnp.float32

    def nrm(i, shape, scale):
        return jax.random.normal(keys[i], shape, f32) * scale

    def gain(i, shape):
        return 1.0 + 0.05 * jax.random.normal(keys[i], shape, f32)

    dt_init = jnp.exp(jax.random.uniform(keys[14], (DEPTH, GDN_HEADS), f32, math.log(1e-3), math.log(1e-1)))
    w0_base = -7.0 + 5.0 * (jnp.arange(RWKV_W, dtype=f32) / (RWKV_W - 1)) ** 0.85
    return {
        'x_prompt': nrm(0, (BATCH, SEQ, D_MODEL), 1.0),
        'x_sample': nrm(1, (DEC_BATCH, DEC_SEQ, D_MODEL), 1.0),
        'state_ret': nrm(2, (DEPTH, DEC_BATCH, RET_HEADS, RET_DK, RET_DV), 0.5),
        'state_delta': nrm(3, (DEPTH, DEC_BATCH, GDN_HEADS, GDN_DK, GDN_DV), 0.3),
        'state_conv': nrm(4, (DEPTH, DEC_BATCH, CONV_W - 1, GDN_CONV_CH), 1.0),
        'state_wkv': nrm(5, (DEPTH, DEC_BATCH, RWKV_HEADS, RWKV_N, RWKV_N), 0.5),
        'state_shift': nrm(6, (DEPTH, DEC_BATCH, 1, RWKV_SHIFT_CH), 1.0),
        'norm_ffn1': gain(7, (DEPTH, D_MODEL)),
        'ffn1_w13': nrm(8, (DEPTH, D_MODEL, 2 * D_FF), D_MODEL ** -0.5),
        'ffn1_w2': nrm(9, (DEPTH, D_FF, D_MODEL), D_FF ** -0.5),
        'norm_mix': gain(10, (DEPTH, D_MODEL)),
        'w_in': nrm(11, (DEPTH, D_MODEL, D_IN), D_MODEL ** -0.5),
        'ret_gn': gain(12, (DEPTH, RET_W)),
        'gdn_conv': nrm(13, (DEPTH, CONV_W, GDN_CONV_CH), CONV_W ** -0.5),
        'gdn_a_log': jnp.log(jax.random.uniform(keys[15], (DEPTH, GDN_HEADS), f32, 1.0, 16.0)),
        'gdn_dt_bias': dt_init + jnp.log(-jnp.expm1(-dt_init)),
        'gdn_norm': gain(16, (DEPTH, GDN_DV)),
        'rwkv_mu': jax.random.uniform(keys[17], (DEPTH, RWKV_SHIFT_CH), f32),
        'rwkv_w0': w0_base[None, :] + nrm(18, (DEPTH, RWKV_W), 0.1),
        'rwkv_w2': nrm(19, (DEPTH, W_LORA, RWKV_W), 0.5 * W_LORA ** -0.5),
        'rwkv_a0': nrm(20, (DEPTH, RWKV_W), 0.1),
        'rwkv_a2': nrm(21, (DEPTH, A_LORA, RWKV_W), A_LORA ** -0.5),
        'rwkv_g2': nrm(22, (DEPTH, G_LORA, RWKV_W), G_LORA ** -0.5),
        'rwkv_kk': 0.85 + nrm(23, (DEPTH, RWKV_W), 0.05),
        'rwkv_ka': 1.0 + nrm(24, (DEPTH, RWKV_W), 0.05),
        'rwkv_rk': nrm(25, (DEPTH, RWKV_W), 0.1),
        'rwkv_gn': gain(26, (DEPTH, RWKV_W)),
        'w_branch': nrm(27, (DEPTH, N_BRANCH, BRANCH_W, D_MODEL), BRANCH_W ** -0.5),
        'w_out': nrm(28, (DEPTH, D_MODEL, D_MODEL), D_MODEL ** -0.5),
        'norm_ffn2': gain(29, (DEPTH, D_MODEL)),
        'ffn2_w13': nrm(30, (DEPTH, D_MODEL, 2 * D_FF), D_MODEL ** -0.5),
        'ffn2_w2': nrm(31, (DEPTH, D_FF, D_MODEL), D_FF ** -0.5),
        'norm_final': gain(32, (D_MODEL,)),
    }


def reference(x_prompt, x_sample, state_ret, state_delta, state_conv, state_wkv, state_shift,
              norm_ffn1, ffn1_w13, ffn1_w2, norm_mix, w_in, ret_gn, gdn_conv, gdn_a_log, gdn_dt_bias,
              gdn_norm, rwkv_mu, rwkv_w0, rwkv_w2, rwkv_a0, rwkv_a2, rwkv_g2, rwkv_kk, rwkv_ka, rwkv_rk,
              rwkv_gn, w_branch, w_out, norm_ffn2, ffn2_w13, ffn2_w2, norm_final):
    f32 = jnp.float32
    bp = x_prompt.shape[0]
    zero_st = (jnp.zeros((bp, RET_HEADS, RET_DK, RET_DV), f32),
               jnp.zeros((bp, GDN_HEADS, GDN_DK, GDN_DV), f32),
               jnp.zeros((bp, CONV_W - 1, GDN_CONV_CH), x_prompt.dtype),
               jnp.zeros((bp, RWKV_HEADS, RWKV_N, RWKV_N), f32),
               jnp.zeros((bp, 1, RWKV_SHIFT_CH), x_prompt.dtype))
    xp, xs = x_prompt, x_sample
    p_new = ([], [], [], [], [])
    s_new = ([], [], [], [], [])
    for l in range(DEPTH):
        lp = {
            'norm_ffn1': norm_ffn1[l], 'ffn1_w13': ffn1_w13[l], 'ffn1_w2': ffn1_w2[l],
            'norm_mix': norm_mix[l], 'w_in': w_in[l], 'ret_gn': ret_gn[l],
            'gdn_conv': gdn_conv[l], 'gdn_a_log': gdn_a_log[l], 'gdn_dt_bias': gdn_dt_bias[l],
            'gdn_norm': gdn_norm[l], 'rwkv_mu': rwkv_mu[l], 'rwkv_w0': rwkv_w0[l],
            'rwkv_w2': rwkv_w2[l], 'rwkv_a0': rwkv_a0[l], 'rwkv_a2': rwkv_a2[l],
            'rwkv_g2': rwkv_g2[l], 'rwkv_kk': rwkv_kk[l], 'rwkv_ka': rwkv_ka[l],
            'rwkv_rk': rwkv_rk[l], 'rwkv_gn': rwkv_gn[l], 'w_branch': w_branch[l],
            'w_out': w_out[l], 'norm_ffn2': norm_ffn2[l], 'ffn2_w13': ffn2_w13[l],
            'ffn2_w2': ffn2_w2[l],
        }
        xp, stp = layer_forward(xp, 0, zero_st, lp)
        xs, sts = layer_forward(xs, PAST_LEN, (state_ret[l], state_delta[l], state_conv[l], state_wkv[l], state_shift[l]), lp)
        for j in range(5):
            p_new[j].append(stp[j])
            s_new[j].append(sts[j])
    y_prompt = rms_norm(xp, norm_final)
    y_sample = rms_norm(xs, norm_final)
    p_ret, p_delta, p_conv, p_wkv, p_shift = (jnp.stack(t, axis=0) for t in p_new)
    s_ret, s_delta, s_conv, s_wkv, s_shift = (jnp.stack(t, axis=0) for t in s_new)
    return (y_prompt, y_sample, p_ret, p_delta, p_conv, p_wkv, p_shift, s_ret, s_delta, s_conv, s_wkv, s_shift)
```

```python
import functools
import math

import jax
import jax.numpy as jnp
from jax import lax
from jax.experimental import pallas as pl
from jax.experimental.pallas import tpu as pltpu

F32 = jnp.float32
BF16 = jnp.bfloat16

D_MODEL = 1024
D_FF = 2816
EPS = 1e-6
CHUNK = 64
PAST_LEN = 2048
RET_HEADS, RET_DK, RET_DV = 4, 64, 128
ROPE_BASE = 10000.0
GDN_HEADS, GDN_DK, GDN_DV, CONV_W = 8, 64, 64, 4
RWKV_HEADS, RWKV_N = 8, 64
RWKV_GN_EPS = 64e-5
RET_QK = RET_HEADS * RET_DK
RET_W = RET_HEADS * RET_DV
GDN_QK = GDN_HEADS * GDN_DK
GDN_V = GDN_HEADS * GDN_DV
GDN_CONV_CH = 2 * GDN_QK + GDN_V
RWKV_W = RWKV_HEADS * RWKV_N
RWKV_SHIFT_CH = 3 * RWKV_W + 256
LANES = 128
SUBLANES = 8
VMEM_LIMIT = 56 * 1024 * 1024


def _dot1(a, b, dims=(((1,), (0,)), ((), ()))):
    return lax.dot_general(a.astype(BF16), b.astype(BF16), dims, preferred_element_type=F32)


def _split(a):
    hi = a.astype(BF16)
    lo = (a - hi.astype(F32)).astype(BF16)
    return hi, lo


def _dot3(a, b, dims=(((1,), (0,)), ((), ()))):
    ah, al = _split(a)
    bh, bl = _split(b)
    d = functools.partial(lax.dot_general, dimension_numbers=dims, preferred_element_type=F32)
    return d(ah, bh) + (d(ah, bl) + d(al, bh))


def _dot_exact_rhs(a, b_bf16):
    a0 = a.astype(BF16)
    r1 = a - a0.astype(F32)
    a1 = r1.astype(BF16)
    a2 = (r1 - a1.astype(F32)).astype(BF16)
    d = functools.partial(jnp.dot, preferred_element_type=F32)
    return d(a0, b_bf16) + (d(a1, b_bf16) + d(a2, b_bf16))


def _dot_exact_lhs(a_bf16, b, dims=(((1,), (0,)), ((), ()))):
    b0 = b.astype(BF16)
    r1 = b - b0.astype(F32)
    b1 = r1.astype(BF16)
    b2 = (r1 - b1.astype(F32)).astype(BF16)
    d = functools.partial(lax.dot_general, dimension_numbers=dims, preferred_element_type=F32)
    return d(a_bf16, b0) + (d(a_bf16, b1) + d(a_bf16, b2))


NT = (((1,), (1,)), ((), ()))
TN = (((0,), (0,)), ((), ()))

_core_mm = _dot3


def _rms(x, g):
    return x * lax.rsqrt(jnp.mean(x * x, axis=-1, keepdims=True) + EPS) * g


def _sigmoid(x):
    return 1.0 / (1.0 + jnp.exp(-x))


def _silu(x):
    return x * _sigmoid(x)


def _softplus(x):
    return jnp.maximum(x, 0.0) + jnp.log(1.0 + jnp.exp(-jnp.abs(x)))


def _iota2(shape, dim):
    return lax.broadcasted_iota(jnp.int32, shape, dim)


def _group_ones(width, group):
    r = _iota2((width, width), 0) // group
    c = _iota2((width, width), 1) // group
    return jnp.where(r == c, 1.0, 0.0).astype(BF16)


def _chunk_tril(tb, chunk, inclusive_lower=True, full=False):
    r = _iota2((tb, tb), 0)
    c = _iota2((tb, tb), 1)
    same = (r // chunk) == (c // chunk)
    if not full:
        same = same & (c <= r)
    return jnp.where(same, 1.0, 0.0).astype(BF16)


def _tri_inv(a, n):
    r = _iota2((n, n), 0)
    c = _iota2((n, n), 1)
    eye = jnp.where(r == c, 1.0, 0.0).astype(F32)
    base = 16
    a0 = jnp.where((r // base) == (c // base), a, 0.0)
    a2 = _core_mm(a0, a0)
    a4 = _core_mm(a2, a2)
    a8 = _core_mm(a4, a4)
    m = eye - a0
    m = m + _core_mm(m, a2)
    m = m + _core_mm(m, a4)
    m = m + _core_mm(m, a8)
    size = 2 * base
    while size <= n:
        half = size // 2
        b = jnp.where(((r // size) == (c // size)) & ((r // half) != (c // half)), a, 0.0)
        m = m - _core_mm(_core_mm(m, b), m)
        size *= 2
    return m


def _const_spec(shape):
    nd = len(shape)
    return pl.BlockSpec(shape, lambda *_: (0,) * nd, pipeline_mode=pl.Buffered(1))


def _ffn_body(x_ref, g_ref, w13_ref, w2_ref, gf_ref, o_ref, *, final_norm):
    x = x_ref[...]
    h = _rms(x, g_ref[...]).astype(BF16)
    ab = jnp.dot(h, w13_ref[...], preferred_element_type=F32)
    act = (_silu(ab[:, :D_FF]) * ab[:, D_FF:]).astype(BF16)
    y = x + 0.5 * jnp.dot(act, w2_ref[...], preferred_element_type=F32)
    if final_norm:
        y = _rms(y, gf_ref[...])
    o_ref[...] = y


def _ffn(x2d, g, w13, w2, gf, final_norm):
    m = x2d.shape[0]
    tm = min(256, m)
    return pl.pallas_call(
        functools.partial(_ffn_body, final_norm=final_norm),
        grid=(m // tm,),
        in_specs=[
            pl.BlockSpec((tm, D_MODEL), lambda i: (i, 0)),
            _const_spec((1, D_MODEL)),
            _const_spec((D_MODEL, 2 * D_FF)),
            _const_spec((D_FF, D_MODEL)),
            _const_spec((1, D_MODEL)),
        ],
        out_specs=pl.BlockSpec((tm, D_MODEL), lambda i: (i, 0)),
        out_shape=jax.ShapeDtypeStruct((m, D_MODEL), F32),
        compiler_params=pltpu.CompilerParams(
            dimension_semantics=("arbitrary",), vmem_limit_bytes=VMEM_LIMIT),
        name="ffn",
    )(x2d, g, w13, w2, gf)


def _ret_body(x_ref, g_ref, w_ref, cos_ref, sin_ref, gn_ref, s0_ref,
              o_ref, sn_ref, s_scr, q_scr, k_scr, v_scr, o_scr, *, tb, chunk):
    j = pl.program_id(1)

    @pl.when(j == 0)
    def _():
        s_scr[...] = s0_ref[...]

    h = _rms(x_ref[...], g_ref[...]).astype(BF16)
    z = jnp.dot(h, w_ref[...], preferred_element_type=F32)
    q = z[:, :RET_QK]
    k = z[:, RET_QK:2 * RET_QK]
    cos = cos_ref[...]
    sin = sin_ref[...]
    first = (_iota2((tb, RET_QK), 1) % RET_DK) < (RET_DK // 2)

    def rot(t):
        swapped = jnp.where(first, pltpu.roll(t, RET_QK - RET_DK // 2, 1),
                            pltpu.roll(t, RET_DK // 2, 1))
        return t * cos + swapped * sin

    q_scr[...] = rot(q)
    k_scr[...] = rot(k) * (RET_DK ** -0.5)
    v_scr[...] = z[:, 2 * RET_QK:2 * RET_QK + RET_W]

    ri = _iota2((chunk, chunk), 0)
    ci = _iota2((chunk, chunk), 1)
    causal = ri >= ci
    dfi = jnp.where(causal, (ri - ci).astype(F32), 0.0)
    col = _iota2((chunk, 1), 0).astype(F32)
    log_g = [math.log1p(-(2.0 ** (-5.0 - hh))) for hh in range(RET_HEADS)]
    dmask = [jnp.where(causal, jnp.exp(dfi * lg), 0.0) for lg in log_g]
    qdec = [jnp.exp((col + 1.0) * lg) for lg in log_g]
    kdec = [jnp.exp((chunk - 1.0 - col) * lg) for lg in log_g]
    cdec = [math.exp(chunk * lg) for lg in log_g]

    def body(c, carry):
        r0 = pl.multiple_of(c * chunk, chunk)
        qc = q_scr[pl.ds(r0, chunk), :]
        kc = k_scr[pl.ds(r0, chunk), :]
        vc = v_scr[pl.ds(r0, chunk), :]
        for hh in range(RET_HEADS):
            qh = qc[:, hh * RET_DK:(hh + 1) * RET_DK]
            kh = kc[:, hh * RET_DK:(hh + 1) * RET_DK]
            vh = vc[:, hh * RET_DV:(hh + 1) * RET_DV]
            sh = s_scr[hh]
            scores = _dot1(qh, kh, NT) * dmask[hh]
            o = _dot1(scores, vh) + _dot1(qh * qdec[hh], sh)
            s_scr[hh] = cdec[hh] * sh + _dot1(kh * kdec[hh], vh, TN)
            o_scr[pl.ds(r0, chunk), hh * RET_DV:(hh + 1) * RET_DV] = o
        return carry

    lax.fori_loop(0, tb // chunk, body, 0)

    gate = z[:, 2 * RET_QK + RET_W:]
    gn = gn_ref[...]
    for hh in range(RET_HEADS):
        sl = slice(hh * RET_DV, (hh + 1) * RET_DV)
        oh = o_scr[:, sl]
        mu = jnp.mean(oh, axis=-1, keepdims=True)
        dlt = oh - mu
        var = jnp.mean(dlt * dlt, axis=-1, keepdims=True)
        y = dlt * lax.rsqrt(var + 1e-5) * gn[:, sl]
        o_ref[:, sl] = (y * _silu(gate[:, sl])).astype(BF16)

    @pl.when(j == pl.num_programs(1) - 1)
    def _():
        sn_ref[...] = s_scr[...]


def _retention(x, g, w, cos, sin, gn, s0, tb, chunk):
    b, l, _ = x.shape
    st = (RET_HEADS, RET_DK, RET_DV)
    return pl.pallas_call(
        functools.partial(_ret_body, tb=tb, chunk=chunk),
        grid=(b, l // tb),
        in_specs=[
            pl.BlockSpec((None, tb, D_MODEL), lambda i, j: (i, j, 0)),
            _const_spec((1, D_MODEL)),
            _const_spec(w.shape),
            pl.BlockSpec((tb, RET_QK), lambda i, j: (j, 0)),
            pl.BlockSpec((tb, RET_QK), lambda i, j: (j, 0)),
            _const_spec((1, RET_W)),
            pl.BlockSpec((None,) + st, lambda i, j: (i, 0, 0, 0)),
        ],
        out_specs=[
            pl.BlockSpec((None, tb, RET_W), lambda i, j: (i, j, 0)),
            pl.BlockSpec((None,) + st, lambda i, j: (i, 0, 0, 0)),
        ],
        out_shape=[jax.ShapeDtypeStruct((b, l, RET_W), BF16),
                   jax.ShapeDtypeStruct((b,) + st, F32)],
        scratch_shapes=[
            pltpu.VMEM(st, F32),
            pltpu.VMEM((tb, RET_QK), F32),
            pltpu.VMEM((tb, RET_QK), F32),
            pltpu.VMEM((tb, RET_W), F32),
            pltpu.VMEM((tb, RET_W), F32),
        ],
        compiler_params=pltpu.CompilerParams(
            dimension_semantics=("arbitrary", "arbitrary"), vmem_limit_bytes=VMEM_LIMIT),
        name="retention",
    )(x, g, w, cos, sin, gn, s0)


GDN_COLS = GDN_CONV_CH + GDN_V + 2 * LANES
HALO = SUBLANES


def _gdn_body(x_ref, g_ref, w_ref, cw_ref, alog_ref, dtb_ref, nrm_ref, s0_ref, c0_ref,
              o_ref, sn_ref, cn_ref,
              s_scr, xp_scr, q_scr, k_scr, v_scr, be_scr, ga_scr, gat_scr, o_scr,
              *, tb, chunk):
    j = pl.program_id(1)
    nc = tb // chunk

    @pl.when(j == 0)
    def _():
        s_scr[...] = s0_ref[...]
        xp_scr[pl.ds(HALO - (CONV_W - 1), CONV_W - 1), :] = c0_ref[...]

    h = _rms(x_ref[...], g_ref[...]).astype(BF16)
    z = jnp.dot(h, w_ref[...], preferred_element_type=F32)
    xp_scr[pl.ds(HALO, tb), :] = z[:, :GDN_CONV_CH]
    cw = cw_ref[...]
    y = xp_scr[pl.ds(HALO - 3, tb), :] * cw[0:1, :]
    for i in range(1, CONV_W):
        y = y + xp_scr[pl.ds(HALO - 3 + i, tb), :] * cw[i:i + 1, :]
    tail = xp_scr[pl.ds(HALO + tb - (CONV_W - 1), CONV_W - 1), :]
    xp_scr[pl.ds(HALO - (CONV_W - 1), CONV_W - 1), :] = tail

    @pl.when(j == pl.num_programs(1) - 1)
    def _():
        cn_ref[...] = tail

    qkv = _silu(y)
    gones = _group_ones(GDN_QK, GDN_DK)
    q = qkv[:, :GDN_QK]
    k = qkv[:, GDN_QK:2 * GDN_QK]
    q_scr[...] = q * lax.rsqrt(_dot_exact_rhs(q * q, gones) + 1e-6) * (GDN_DK ** -0.5)
    k_scr[...] = k * lax.rsqrt(_dot_exact_rhs(k * k, gones) + 1e-6)
    v_scr[...] = qkv[:, 2 * GDN_QK:]

    zb = z[:, GDN_CONV_CH + GDN_V:GDN_CONV_CH + GDN_V + LANES]
    za = z[:, GDN_CONV_CH + GDN_V + LANES:]
    be_scr[...] = _sigmoid(zb)
    log_a = -jnp.exp(alog_ref[...]) * _softplus(za + dtb_ref[...])
    ga = _dot_exact_lhs(_chunk_tril(tb, chunk), log_a)
    ga_scr[...] = ga
    eye16 = jnp.where(_iota2((2 * SUBLANES, LANES), 0) == _iota2((2 * SUBLANES, LANES), 1),
                      1.0, 0.0).astype(BF16)
    gat = _dot_exact_lhs(eye16, ga, NT)
    for c in range(nc):
        gat_scr[c] = gat[:, c * chunk:(c + 1) * chunk]

    ri = _iota2((chunk, chunk), 0)
    ci = _iota2((chunk, chunk), 1)
    incl = ri >= ci
    strict = ri > ci

    def body(c, carry):
        r0 = pl.multiple_of(c * chunk, chunk)
        qc = q_scr[pl.ds(r0, chunk), :]
        kc = k_scr[pl.ds(r0, chunk), :]
        vc = v_scr[pl.ds(r0, chunk), :]
        bec = be_scr[pl.ds(r0, chunk), :]
        gac = ga_scr[pl.ds(r0, chunk), :]
        gatc = gat_scr[c]
        for hh in range(GDN_HEADS):
            sl = slice(hh * GDN_DK, (hh + 1) * GDN_DK)
            qh, kh, vh = qc[:, sl], kc[:, sl], vc[:, sl]
            bcol = bec[:, hh:hh + 1]
            gcol = gac[:, hh:hh + 1]
            grow = gatc[hh:hh + 1, :]
            gam = jnp.where(incl, jnp.exp(jnp.where(incl, gcol - grow, 0.0)), 0.0)
            kk = _core_mm(kh, kh, NT)
            a_mat = jnp.where(strict, bcol * kk * gam, 0.0)
            tinv = _tri_inv(a_mat, chunk)
            eg = jnp.exp(gcol)
            rhs = jnp.concatenate([bcol * vh, (bcol * eg) * kh], axis=1)
            sol = _core_mm(tinv, rhs)
            u0 = sol[:, :GDN_DV]
            wm = sol[:, GDN_DV:]
            pm = _core_mm(qh, kh, NT) * gam
            sh = s_scr[hh]
            u = u0 - _core_mm(wm, sh)
            o = _core_mm(qh * eg, sh) + _core_mm(pm, u)
            glast = gcol[chunk - 1:chunk, :]
            s_scr[hh] = jnp.exp(glast) * sh + _core_mm(kh * jnp.exp(glast - gcol), u, TN)
            o_scr[pl.ds(r0, chunk), sl] = o
        return carry

    lax.fori_loop(0, nc, body, 0)

    o = o_scr[...]
    ms = _dot_exact_rhs(o * o, _group_ones(GDN_V, GDN_DV)) * (1.0 / GDN_DV)
    gate = z[:, GDN_CONV_CH:GDN_CONV_CH + GDN_V]
    o_ref[...] = (o * lax.rsqrt(ms + EPS) * nrm_ref[...] * _silu(gate)).astype(BF16)

    @pl.when(j == pl.num_programs(1) - 1)
    def _():
        sn_ref[...] = s_scr[...]


def _gdn(x, g, w, cw, alog, dtb, nrm, s0, c0, tb, chunk):
    b, l, _ = x.shape
    st = (GDN_HEADS, GDN_DK, GDN_DV)
    cst = (CONV_W - 1, GDN_CONV_CH)
    return pl.pallas_call(
        functools.partial(_gdn_body, tb=tb, chunk=chunk),
        grid=(b, l // tb),
        in_specs=[
            pl.BlockSpec((None, tb, D_MODEL), lambda i, j: (i, j, 0)),
            _const_spec((1, D_MODEL)),
            _const_spec(w.shape),
            _const_spec(cw.shape),
            _const_spec((1, LANES)),
            _const_spec((1, LANES)),
            _const_spec((1, GDN_V)),
            pl.BlockSpec((None,) + st, lambda i, j: (i, 0, 0, 0)),
            pl.BlockSpec((None,) + cst, lambda i, j: (i, 0, 0)),
        ],
        out_specs=[
            pl.BlockSpec((None, tb, GDN_V), lambda i, j: (i, j, 0)),
            pl.BlockSpec((None,) + st, lambda i, j: (i, 0, 0, 0)),
            pl.BlockSpec((None,) + cst, lambda i, j: (i, 0, 0)),
        ],
        out_shape=[jax.ShapeDtypeStruct((b, l, GDN_V), BF16),
                   jax.ShapeDtypeStruct((b,) + st, F32),
                   jax.ShapeDtypeStruct((b,) + cst, F32)],
        scratch_shapes=[
            pltpu.VMEM(st, F32),
            pltpu.VMEM((tb + HALO, GDN_CONV_CH), F32),
            pltpu.VMEM((tb, GDN_QK), F32),
            pltpu.VMEM((tb, GDN_QK), F32),
            pltpu.VMEM((tb, GDN_V), F32),
            pltpu.VMEM((tb, LANES), F32),
            pltpu.VMEM((tb, LANES), F32),
            pltpu.VMEM((tb // chunk, 2 * SUBLANES, chunk), F32),
            pltpu.VMEM((tb, GDN_V), F32),
        ],
        compiler_params=pltpu.CompilerParams(
            dimension_semantics=("arbitrary", "arbitrary"), vmem_limit_bytes=VMEM_LIMIT),
        name="gdn",
    )(x, g, w, cw, alog, dtb, nrm, s0, c0)


def _rwkv_body(x_ref, g_ref, w_ref, mu_ref, w0_ref, w2_ref, a0_ref, a2_ref, g2_ref,
               kkw_ref, ka_ref, rk_ref, gn_ref, s0_ref, sh0_ref,
               o_ref, sn_ref, shn_ref,
               s_scr, zc_scr, at_scr, bt_scr, kt_scr, rt_scr, bh_scr, kh_scr, v_scr, wt_scr,
               o_scr, *, tb, chunk):
    j = pl.program_id(1)
    nc = tb // chunk

    @pl.when(j == 0)
    def _():
        s_scr[...] = s0_ref[...]
        zc_scr[pl.ds(HALO - 1, 1), :] = sh0_ref[...]

    h = _rms(x_ref[...], g_ref[...]).astype(BF16)
    z = jnp.dot(h, w_ref[...], preferred_element_type=F32)
    zc_scr[pl.ds(HALO, tb), :] = z
    zp = zc_scr[pl.ds(HALO - 1, tb), :]
    last = zc_scr[pl.ds(HALO + tb - 1, 1), :]
    zc_scr[pl.ds(HALO - 1, 1), :] = last

    @pl.when(j == pl.num_programs(1) - 1)
    def _():
        shn_ref[...] = last

    cs = z + (zp - z) * mu_ref[...]
    r = cs[:, :RWKV_W]
    k = cs[:, RWKV_W:2 * RWKV_W]
    v = cs[:, 2 * RWKV_W:3 * RWKV_W]
    zwa = cs[:, 3 * RWKV_W:3 * RWKV_W + LANES]
    zg = cs[:, 3 * RWKV_W + LANES:]
    w_raw = -_softplus(-(w0_ref[...] + _dot3(jnp.tanh(zwa), w2_ref[...]))) - 0.5
    lw = -jnp.exp(w_raw)
    a = _sigmoid(a0_ref[...] + _dot3(zwa, a2_ref[...]))
    g = _dot3(_sigmoid(zg), g2_ref[...])
    gones = _group_ones(RWKV_W, RWKV_N)
    kkn = k * kkw_ref[...]
    kkn = kkn * lax.rsqrt(_dot_exact_rhs(kkn * kkn, gones) + 1e-6)
    k2 = k * (1.0 + (a - 1.0) * ka_ref[...])
    bonus = _dot_exact_rhs(r * k2 * rk_ref[...], gones) * v

    cum = _dot_exact_lhs(_chunk_tril(tb, chunk), lw)
    tot = _dot_exact_lhs(_chunk_tril(tb, chunk, full=True), lw)
    w_in = jnp.exp(cum)
    w_inv = jnp.exp(-cum)
    w_rem = jnp.exp(tot - cum)
    kka = kkn * a
    at_scr[...] = -kkn * jnp.exp(cum - lw)
    bt_scr[...] = kka * w_inv
    kt_scr[...] = k2 * w_inv
    rt_scr[...] = r * w_in
    bh_scr[...] = kka * w_rem
    kh_scr[...] = k2 * w_rem
    v_scr[...] = v
    wt_scr[...] = jnp.exp(tot)

    ri = _iota2((chunk, chunk), 0)
    ci = _iota2((chunk, chunk), 1)
    incl = ri >= ci
    strict = ri > ci

    def body(c, carry):
        r0 = pl.multiple_of(c * chunk, chunk)
        atc = at_scr[pl.ds(r0, chunk), :]
        btc = bt_scr[pl.ds(r0, chunk), :]
        ktc = kt_scr[pl.ds(r0, chunk), :]
        rtc = rt_scr[pl.ds(r0, chunk), :]
        bhc = bh_scr[pl.ds(r0, chunk), :]
        khc = kh_scr[pl.ds(r0, chunk), :]
        vc = v_scr[pl.ds(r0, chunk), :]
        wtc = wt_scr[pl.ds(r0, 1), :]
        for hh in range(RWKV_HEADS):
            sl = slice(hh * RWKV_N, (hh + 1) * RWKV_N)
            at, bt, kt, rt, vh = atc[:, sl], btc[:, sl], ktc[:, sl], rtc[:, sl], vc[:, sl]
            sh = s_scr[hh]
            a_ab = jnp.where(strict, _core_mm(at, bt, NT), 0.0)
            a_ak = jnp.where(strict, _core_mm(at, kt, NT), 0.0)
            a_rb = jnp.where(incl, _core_mm(rt, bt, NT), 0.0)
            a_rk = jnp.where(incl, _core_mm(rt, kt, NT), 0.0)
            tinv = _tri_inv(-a_ab, chunk)
            rhs = _core_mm(at, sh, NT) + _core_mm(a_ak, vh)
            u = _core_mm(tinv, rhs)
            o = _core_mm(rt, sh, NT) + _core_mm(a_rb, u) + _core_mm(a_rk, vh)
            s_scr[hh] = (sh * wtc[:, sl] + _core_mm(u, bhc[:, sl], TN)
                         + _core_mm(vh, khc[:, sl], TN))
            o_scr[pl.ds(r0, chunk), sl] = o
        return carry

    lax.fori_loop(0, nc, body, 0)

    o = o_scr[...]
    mean = _dot_exact_rhs(o, gones) * (1.0 / RWKV_N)
    dlt = o - mean
    var = _dot_exact_rhs(dlt * dlt, gones) * (1.0 / RWKV_N)
    y = dlt * lax.rsqrt(var + RWKV_GN_EPS) * gn_ref[...]
    o_ref[...] = ((y + bonus) * g).astype(BF16)

    @pl.when(j == pl.num_programs(1) - 1)
    def _():
        sn_ref[...] = s_scr[...]


def _rwkv(x, g, w, mu, w0, w2, a0, a2, g2, kkw, ka, rk, gn, s0, sh0, tb, chunk):
    b, l, _ = x.shape
    st = (RWKV_HEADS, RWKV_N, RWKV_N)
    sst = (1, RWKV_SHIFT_CH)
    row = _const_spec((1, RWKV_W))
    big = pltpu.VMEM((tb, RWKV_W), F32)
    return pl.pallas_call(
        functools.partial(_rwkv_body, tb=tb, chunk=chunk),
        grid=(b, l // tb),
        in_specs=[
            pl.BlockSpec((None, tb, D_MODEL), lambda i, j: (i, j, 0)),
            _const_spec((1, D_MODEL)),
            _const_spec(w.shape),
            _const_spec((1, RWKV_SHIFT_CH)),
            row, _const_spec((LANES, RWKV_W)), row, _const_spec((LANES, RWKV_W)),
            _const_spec((LANES, RWKV_W)),
            row, row, row, row,
            pl.BlockSpec((None,) + st, lambda i, j: (i, 0, 0, 0)),
            pl.BlockSpec((None,) + sst, lambda i, j: (i, 0, 0)),
        ],
        out_specs=[
            pl.BlockSpec((None, tb, RWKV_W), lambda i, j: (i, j, 0)),
            pl.BlockSpec((None,) + st, lambda i, j: (i, 0, 0, 0)),
            pl.BlockSpec((None,) + sst, lambda i, j: (i, 0, 0)),
        ],
        out_shape=[jax.ShapeDtypeStruct((b, l, RWKV_W), BF16),
                   jax.ShapeDtypeStruct((b,) + st, F32),
                   jax.ShapeDtypeStruct((b,) + sst, F32)],
        scratch_shapes=[
            pltpu.VMEM(st, F32),
            pltpu.VMEM((tb + HALO, RWKV_SHIFT_CH), F32),
            big, big, big, big, big, big, big, big, big,
        ],
        compiler_params=pltpu.CompilerParams(
            dimension_semantics=("arbitrary", "arbitrary"), vmem_limit_bytes=VMEM_LIMIT),
        name="rwkv7",
    )(x, g, w, mu, w0, w2, a0, a2, g2, kkw, ka, rk, gn, s0, sh0)


def _merge_body(x_ref, g_ref, wg_ref, oa_ref, ob_ref, oc_ref, wb_ref, wo_ref, o_ref):
    x = x_ref[...]
    h = _rms(x, g_ref[...]).astype(BF16)
    gate = jnp.dot(h, wg_ref[...], preferred_element_type=F32)
    m = None
    for i, br in enumerate((oa_ref, ob_ref, oc_ref)):
        t = _sigmoid(gate[:, i * D_MODEL:(i + 1) * D_MODEL]) * jnp.dot(
            br[...], wb_ref[i], preferred_element_type=F32)
        m = t if m is None else m + t
    o_ref[...] = x + jnp.dot(m.astype(BF16), wo_ref[...], preferred_element_type=F32)


def _merge(x2d, g, wg, oa, ob, oc, wb, wo):
    m = x2d.shape[0]
    tm = min(256, m)
    bw = oa.shape[1]
    tok = lambda w: pl.BlockSpec((tm, w), lambda i: (i, 0))
    return pl.pallas_call(
        _merge_body,
        grid=(m // tm,),
        in_specs=[tok(D_MODEL), _const_spec((1, D_MODEL)), _const_spec(wg.shape),
                  tok(bw), tok(bw), tok(bw), _const_spec(wb.shape), _const_spec(wo.shape)],
        out_specs=tok(D_MODEL),
        out_shape=jax.ShapeDtypeStruct((m, D_MODEL), F32),
        compiler_params=pltpu.CompilerParams(
            dimension_semantics=("arbitrary",), vmem_limit_bytes=VMEM_LIMIT),
        name="merge",
    )(x2d, g, wg, oa, ob, oc, wb, wo)


def _rope_tables(pos0, l):
    half = RET_DK // 2
    inv = ROPE_BASE ** (-jnp.arange(half, dtype=F32) / half)
    ang = (pos0 + jnp.arange(l)).astype(F32)[:, None] * inv[None, :]
    cos = jnp.cos(ang)
    sin = jnp.sin(ang)
    cos = jnp.tile(jnp.concatenate([cos, cos], axis=1), (1, RET_HEADS))
    sin = jnp.tile(jnp.concatenate([-sin, sin], axis=1), (1, RET_HEADS))
    return cos, sin


def _prep_weights(p):
    w_in = p["w_in"]
    depth = w_in.shape[0]
    o = 0
    pieces = {}
    for name, size in (("a_qkvg", 2 * RET_QK + 2 * RET_W), ("b_qkv", GDN_CONV_CH), ("b_g", GDN_V),
                       ("b_beta", GDN_HEADS), ("b_alpha", GDN_HEADS), ("c_z", RWKV_SHIFT_CH),
                       ("gate", 3 * D_MODEL)):
        pieces[name] = w_in[:, :, o:o + size]
        o += size
    pad_l = lambda t, n: jnp.pad(t, ((0, 0), (0, 0), (0, n - t.shape[2])))
    row = lambda t: t[:, None, :]
    zeros64 = jnp.zeros((depth, 64, RWKV_W), F32)
    return dict(
        norm_ffn1=row(p["norm_ffn1"]), norm_mix=row(p["norm_mix"]), norm_ffn2=row(p["norm_ffn2"]),
        ffn1_w13=p["ffn1_w13"].astype(BF16), ffn1_w2=p["ffn1_w2"].astype(BF16),
        ffn2_w13=p["ffn2_w13"].astype(BF16), ffn2_w2=p["ffn2_w2"].astype(BF16),
        w_ret=pieces["a_qkvg"].astype(BF16),
        w_gdn=jnp.concatenate([pieces["b_qkv"], pieces["b_g"], pad_l(pieces["b_beta"], LANES),
                               pad_l(pieces["b_alpha"], LANES)], axis=2).astype(BF16),
        w_rwkv=pieces["c_z"].astype(BF16),
        w_gate=pieces["gate"].astype(BF16),
        ret_gn=row(p["ret_gn"]),
        gdn_conv=p["gdn_conv"],
        gdn_a_log=row(jnp.pad(p["gdn_a_log"], ((0, 0), (0, LANES - GDN_HEADS)))),
        gdn_dt_bias=row(jnp.pad(p["gdn_dt_bias"], ((0, 0), (0, LANES - GDN_HEADS)))),
        gdn_norm=row(jnp.tile(p["gdn_norm"], (1, GDN_HEADS))),
        rwkv_mu=row(p["rwkv_mu"]),
        rwkv_w0=row(p["rwkv_w0"]),
        rwkv_w2=jnp.concatenate([p["rwkv_w2"], zeros64], axis=1),
        rwkv_a0=row(p["rwkv_a0"]),
        rwkv_a2=jnp.concatenate([zeros64, p["rwkv_a2"]], axis=1),
        rwkv_g2=p["rwkv_g2"],
        rwkv_kk=row(p["rwkv_kk"]), rwkv_ka=row(p["rwkv_ka"]), rwkv_rk=row(p["rwkv_rk"]),
        rwkv_gn=row(p["rwkv_gn"]),
        w_branch=p["w_branch"].astype(BF16), w_out=p["w_out"].astype(BF16),
    )


def _layer(x, st, lp, tables, gf, final_norm, tb, chunk):
    b, l, d = x.shape
    s_ret, s_delta, s_conv, s_wkv, s_shift = st
    cos, sin = tables
    x2 = _ffn(x.reshape(b * l, d), lp["norm_ffn1"], lp["ffn1_w13"], lp["ffn1_w2"], gf, False)
    x3 = x2.reshape(b, l, d)
    oa, n_ret = _retention(x3, lp["norm_mix"], lp["w_ret"], cos, sin, lp["ret_gn"], s_ret, tb, chunk)
    ob, n_delta, n_conv = _gdn(x3, lp["norm_mix"], lp["w_gdn"], lp["gdn_conv"], lp["gdn_a_log"],
                               lp["gdn_dt_bias"], lp["gdn_norm"], s_delta, s_conv, tb, chunk)
    oc, n_wkv, n_shift = _rwkv(x3, lp["norm_mix"], lp["w_rwkv"], lp["rwkv_mu"], lp["rwkv_w0"],
                               lp["rwkv_w2"], lp["rwkv_a0"], lp["rwkv_a2"], lp["rwkv_g2"],
                               lp["rwkv_kk"], lp["rwkv_ka"], lp["rwkv_rk"], lp["rwkv_gn"],
                               s_wkv, s_shift, tb, chunk)
    bw = oa.shape[-1]
    x4 = _merge(x2, lp["norm_mix"], lp["w_gate"], oa.reshape(b * l, bw), ob.reshape(b * l, bw),
                oc.reshape(b * l, bw), lp["w_branch"], lp["w_out"])
    x5 = _ffn(x4, lp["norm_ffn2"], lp["ffn2_w13"], lp["ffn2_w2"], gf, final_norm)
    return x5.reshape(b, l, d), (n_ret, n_delta, n_conv, n_wkv, n_shift)


def _run_stream(x, pos0, states, wts, gf, tb):
    depth = wts["w_ret"].shape[0]
    l = x.shape[1]
    chunk = min(CHUNK, l)
    tb = min(tb, l)
    tables = _rope_tables(pos0, l)
    new = [[] for _ in range(5)]
    for li in range(depth):
        lp = {k: v[li] for k, v in wts.items()}
        st = tuple(s[li] for s in states)
        x, nst = _layer(x, st, lp, tables, gf, li == depth - 1, tb, chunk)
        for jdx in range(5):
            new[jdx].append(nst[jdx])
    return x, tuple(jnp.stack(t, axis=0) for t in new)


def kernel(x_prompt, x_sample, state_ret, state_delta, state_conv, state_wkv, state_shift,
           norm_ffn1, ffn1_w13, ffn1_w2, norm_mix, w_in, ret_gn, gdn_conv, gdn_a_log, gdn_dt_bias,
           gdn_norm, rwkv_mu, rwkv_w0, rwkv_w2, rwkv_a0, rwkv_a2, rwkv_g2, rwkv_kk, rwkv_ka, rwkv_rk,
           rwkv_gn, w_branch, w_out, norm_ffn2, ffn2_w13, ffn2_w2, norm_final):
    params = dict(
        norm_ffn1=norm_ffn1, ffn1_w13=ffn1_w13, ffn1_w2=ffn1_w2, norm_mix=norm_mix, w_in=w_in,
        ret_gn=ret_gn, gdn_conv=gdn_conv, gdn_a_log=gdn_a_log, gdn_dt_bias=gdn_dt_bias,
        gdn_norm=gdn_norm, rwkv_mu=rwkv_mu, rwkv_w0=rwkv_w0, rwkv_w2=rwkv_w2, rwkv_a0=rwkv_a0,
        rwkv_a2=rwkv_a2, rwkv_g2=rwkv_g2, rwkv_kk=rwkv_kk, rwkv_ka=rwkv_ka, rwkv_rk=rwkv_rk,
        rwkv_gn=rwkv_gn, w_branch=w_branch, w_out=w_out, norm_ffn2=norm_ffn2, ffn2_w13=ffn2_w13,
        ffn2_w2=ffn2_w2)
    wts = _prep_weights(params)
    gf = norm_final[None, :]
    depth = w_in.shape[0]
    bp = x_prompt.shape[0]
    zero_states = (
        jnp.zeros((depth, bp, RET_HEADS, RET_DK, RET_DV), F32),
        jnp.zeros((depth, bp, GDN_HEADS, GDN_DK, GDN_DV), F32),
        jnp.zeros((depth, bp, CONV_W - 1, GDN_CONV_CH), F32),
        jnp.zeros((depth, bp, RWKV_HEADS, RWKV_N, RWKV_N), F32),
        jnp.zeros((depth, bp, 1, RWKV_SHIFT_CH), F32),
    )
    y_p, st_p = _run_stream(x_prompt, 0, zero_states, wts, gf, 256)
    y_s, st_s = _run_stream(x_sample, PAST_LEN, (state_ret, state_delta, state_conv, state_wkv,
                                                 state_shift), wts, gf, 256)
    return (y_p, y_s) + st_p + st_s
```

```python
import functools
import math

import jax
import jax.numpy as jnp
from jax import lax
from jax.experimental import pallas as pl
from jax.experimental.pallas import tpu as pltpu

F32 = jnp.float32
BF16 = jnp.bfloat16

D_MODEL = 1024
D_FF = 2816
EPS = 1e-6
CHUNK = 64
PAST_LEN = 2048
RET_HEADS, RET_DK, RET_DV = 4, 64, 128
ROPE_BASE = 10000.0
GDN_HEADS, GDN_DK, GDN_DV, CONV_W = 8, 64, 64, 4
RWKV_HEADS, RWKV_N = 8, 64
RWKV_GN_EPS = 64e-5
RET_QK = RET_HEADS * RET_DK
RET_W = RET_HEADS * RET_DV
GDN_QK = GDN_HEADS * GDN_DK
GDN_V = GDN_HEADS * GDN_DV
GDN_CONV_CH = 2 * GDN_QK + GDN_V
RWKV_W = RWKV_HEADS * RWKV_N
RWKV_SHIFT_CH = 3 * RWKV_W + 256
LANES = 128
SUBLANES = 8
VMEM_LIMIT = 56 * 1024 * 1024


def _dot1(a, b, dims=(((1,), (0,)), ((), ()))):
    return lax.dot_general(a.astype(BF16), b.astype(BF16), dims, preferred_element_type=F32)


def _split(a):
    hi = a.astype(BF16)
    lo = (a - hi.astype(F32)).astype(BF16)
    return hi, lo


def _dot3(a, b, dims=(((1,), (0,)), ((), ()))):
    ah, al = _split(a)
    bh, bl = _split(b)
    d = functools.partial(lax.dot_general, dimension_numbers=dims, preferred_element_type=F32)
    return d(ah, bh) + (d(ah, bl) + d(al, bh))


def _dot_exact_rhs(a, b_bf16):
    a0 = a.astype(BF16)
    r1 = a - a0.astype(F32)
    a1 = r1.astype(BF16)
    a2 = (r1 - a1.astype(F32)).astype(BF16)
    d = functools.partial(jnp.dot, preferred_element_type=F32)
    return d(a0, b_bf16) + (d(a1, b_bf16) + d(a2, b_bf16))


def _dot_exact_lhs(a_bf16, b, dims=(((1,), (0,)), ((), ()))):
    b0 = b.astype(BF16)
    r1 = b - b0.astype(F32)
    b1 = r1.astype(BF16)
    b2 = (r1 - b1.astype(F32)).astype(BF16)
    d = functools.partial(lax.dot_general, dimension_numbers=dims, preferred_element_type=F32)
    return d(a_bf16, b0) + (d(a_bf16, b1) + d(a_bf16, b2))


NT = (((1,), (1,)), ((), ()))
TN = (((0,), (0,)), ((), ()))

_core_mm = _dot3


def _rms(x, g):
    return x * lax.rsqrt(jnp.mean(x * x, axis=-1, keepdims=True) + EPS) * g


def _sigmoid(x):
    return 1.0 / (1.0 + jnp.exp(-x))


def _silu(x):
    return x * _sigmoid(x)


def _softplus(x):
    return jnp.maximum(x, 0.0) + jnp.log(1.0 + jnp.exp(-jnp.abs(x)))


def _iota2(shape, dim):
    return lax.broadcasted_iota(jnp.int32, shape, dim)


def _group_ones(width, group):
    r = _iota2((width, width), 0) // group
    c = _iota2((width, width), 1) // group
    return jnp.where(r == c, 1.0, 0.0).astype(BF16)


def _chunk_tril(tb, chunk, inclusive_lower=True, full=False):
    r = _iota2((tb, tb), 0)
    c = _iota2((tb, tb), 1)
    same = (r // chunk) == (c // chunk)
    if not full:
        same = same & (c <= r)
    return jnp.where(same, 1.0, 0.0).astype(BF16)


def _bf(x):
    return x.astype(BF16)


def _mm(a, b, dims=(((1,), (0,)), ((), ()))):
    return lax.dot_general(a, b, dims, preferred_element_type=F32)


def _tri_inv_multi(a_list, n):
    r = _iota2((n, n), 0)
    c = _iota2((n, n), 1)
    eye = jnp.where(r == c, 1.0, 0.0).astype(F32)
    base = 16
    blk = (r // base) == (c // base)
    a0f = [jnp.where(blk, a, 0.0) for a in a_list]
    a0 = [_bf(x) for x in a0f]
    a2 = [_bf(_mm(x, x)) for x in a0]
    a4 = [_bf(_mm(x, x)) for x in a2]
    a8 = [_bf(_mm(x, x)) for x in a4]
    m = [eye - x for x in a0f]
    for p in (a2, a4, a8):
        m = [mi + _mm(_bf(mi), x) for mi, x in zip(m, p)]
    size = 2 * base
    while size <= n:
        half = size // 2
        sel = ((r // size) == (c // size)) & ((r // half) != (c // half))
        b = [_bf(jnp.where(sel, a, 0.0)) for a in a_list]
        mb = [_bf(mi) for mi in m]
        t = [_bf(_mm(x, y)) for x, y in zip(mb, b)]
        m = [mi - _mm(x, y) for mi, x, y in zip(m, t, mb)]
        size *= 2
    return m


def _const_spec(shape):
    nd = len(shape)
    return pl.BlockSpec(shape, lambda *_: (0,) * nd, pipeline_mode=pl.Buffered(1))


def _ffn_body(x_ref, g_ref, w13_ref, w2_ref, gf_ref, o_ref, *, final_norm):
    x = x_ref[...]
    h = _rms(x, g_ref[...]).astype(BF16)
    ab = jnp.dot(h, w13_ref[...], preferred_element_type=F32)
    act = (_silu(ab[:, :D_FF]) * ab[:, D_FF:]).astype(BF16)
    y = x + 0.5 * jnp.dot(act, w2_ref[...], preferred_element_type=F32)
    if final_norm:
        y = _rms(y, gf_ref[...])
    o_ref[...] = y


def _ffn(x2d, g, w13, w2, gf, final_norm):
    m = x2d.shape[0]
    tm = min(256, m)
    return pl.pallas_call(
        functools.partial(_ffn_body, final_norm=final_norm),
        grid=(m // tm,),
        in_specs=[
            pl.BlockSpec((tm, D_MODEL), lambda i: (i, 0)),
            _const_spec((1, D_MODEL)),
            _const_spec((D_MODEL, 2 * D_FF)),
            _const_spec((D_FF, D_MODEL)),
            _const_spec((1, D_MODEL)),
        ],
        out_specs=pl.BlockSpec((tm, D_MODEL), lambda i: (i, 0)),
        out_shape=jax.ShapeDtypeStruct((m, D_MODEL), F32),
        compiler_params=pltpu.CompilerParams(
            dimension_semantics=("arbitrary",), vmem_limit_bytes=VMEM_LIMIT),
        name="ffn",
    )(x2d, g, w13, w2, gf)


def _ret_body(x_ref, g_ref, w_ref, cos_ref, sin_ref, gn_ref, s0_ref,
              o_ref, sn_ref, s_scr, q_scr, k_scr, v_scr, o_scr, *, tb, chunk):
    j = pl.program_id(1)

    @pl.when(j == 0)
    def _():
        s_scr[...] = s0_ref[...]

    h = _rms(x_ref[...], g_ref[...]).astype(BF16)
    z = jnp.dot(h, w_ref[...], preferred_element_type=F32)
    q = z[:, :RET_QK]
    k = z[:, RET_QK:2 * RET_QK]
    cos = cos_ref[...]
    sin = sin_ref[...]
    first = (_iota2((tb, RET_QK), 1) % RET_DK) < (RET_DK // 2)

    def rot(t):
        swapped = jnp.where(first, pltpu.roll(t, RET_QK - RET_DK // 2, 1),
                            pltpu.roll(t, RET_DK // 2, 1))
        return t * cos + swapped * sin

    q_scr[...] = rot(q)
    k_scr[...] = rot(k) * (RET_DK ** -0.5)
    v_scr[...] = z[:, 2 * RET_QK:2 * RET_QK + RET_W]

    ri = _iota2((chunk, chunk), 0)
    ci = _iota2((chunk, chunk), 1)
    causal = ri >= ci
    dfi = jnp.where(causal, (ri - ci).astype(F32), 0.0)
    col = _iota2((chunk, 1), 0).astype(F32)
    log_g = [math.log1p(-(2.0 ** (-5.0 - hh))) for hh in range(RET_HEADS)]
    dmask = [jnp.where(causal, jnp.exp(dfi * lg), 0.0) for lg in log_g]
    qdec = [jnp.exp((col + 1.0) * lg) for lg in log_g]
    kdec = [jnp.exp((chunk - 1.0 - col) * lg) for lg in log_g]
    cdec = [math.exp(chunk * lg) for lg in log_g]

    def body(c, carry):
        r0 = pl.multiple_of(c * chunk, chunk)
        qc = q_scr[pl.ds(r0, chunk), :]
        kc = k_scr[pl.ds(r0, chunk), :]
        vc = v_scr[pl.ds(r0, chunk), :]
        for hh in range(RET_HEADS):
            qh = qc[:, hh * RET_DK:(hh + 1) * RET_DK]
            kh = kc[:, hh * RET_DK:(hh + 1) * RET_DK]
            vh = vc[:, hh * RET_DV:(hh + 1) * RET_DV]
            sh = s_scr[hh]
            scores = _dot1(qh, kh, NT) * dmask[hh]
            o = _dot1(scores, vh) + _dot1(qh * qdec[hh], sh)
            s_scr[hh] = cdec[hh] * sh + _dot1(kh * kdec[hh], vh, TN)
            o_scr[pl.ds(r0, chunk), hh * RET_DV:(hh + 1) * RET_DV] = o
        return carry

    lax.fori_loop(0, tb // chunk, body, 0)

    gate = z[:, 2 * RET_QK + RET_W:]
    gn = gn_ref[...]
    for hh in range(RET_HEADS):
        sl = slice(hh * RET_DV, (hh + 1) * RET_DV)
        oh = o_scr[:, sl]
        mu = jnp.mean(oh, axis=-1, keepdims=True)
        dlt = oh - mu
        var = jnp.mean(dlt * dlt, axis=-1, keepdims=True)
        y = dlt * lax.rsqrt(var + 1e-5) * gn[:, sl]
        o_ref[:, sl] = (y * _silu(gate[:, sl])).astype(BF16)

    @pl.when(j == pl.num_programs(1) - 1)
    def _():
        sn_ref[...] = s_scr[...]


def _retention(x, g, w, cos, sin, gn, s0, tb, chunk):
    b, l, _ = x.shape
    st = (RET_HEADS, RET_DK, RET_DV)
    return pl.pallas_call(
        functools.partial(_ret_body, tb=tb, chunk=chunk),
        grid=(b, l // tb),
        in_specs=[
            pl.BlockSpec((None, tb, D_MODEL), lambda i, j: (i, j, 0)),
            _const_spec((1, D_MODEL)),
            _const_spec(w.shape),
            pl.BlockSpec((tb, RET_QK), lambda i, j: (j, 0)),
            pl.BlockSpec((tb, RET_QK), lambda i, j: (j, 0)),
            _const_spec((1, RET_W)),
            pl.BlockSpec((None,) + st, lambda i, j: (i, 0, 0, 0)),
        ],
        out_specs=[
            pl.BlockSpec((None, tb, RET_W), lambda i, j: (i, j, 0)),
            pl.BlockSpec((None,) + st, lambda i, j: (i, 0, 0, 0)),
        ],
        out_shape=[jax.ShapeDtypeStruct((b, l, RET_W), BF16),
                   jax.ShapeDtypeStruct((b,) + st, F32)],
        scratch_shapes=[
            pltpu.VMEM(st, F32),
            pltpu.VMEM((tb, RET_QK), F32),
            pltpu.VMEM((tb, RET_QK), F32),
            pltpu.VMEM((tb, RET_W), F32),
            pltpu.VMEM((tb, RET_W), F32),
        ],
        compiler_params=pltpu.CompilerParams(
            dimension_semantics=("arbitrary", "arbitrary"), vmem_limit_bytes=VMEM_LIMIT),
        name="retention",
    )(x, g, w, cos, sin, gn, s0)


GDN_COLS = GDN_CONV_CH + GDN_V + 2 * LANES
HALO = SUBLANES


def _gdn_body(x_ref, g_ref, w_ref, cw_ref, alog_ref, dtb_ref, nrm_ref, s0_ref, c0_ref,
              o_ref, sn_ref, cn_ref,
              s_scr, xp_scr, q_scr, k_scr, v_scr, be_scr, ga_scr, gat_scr, o_scr,
              m_scr, n_scr, qp_scr, egl_scr, *, tb, chunk):
    j = pl.program_id(1)
    nc = tb // chunk

    @pl.when(j == 0)
    def _():
        s_scr[...] = s0_ref[...]
        xp_scr[pl.ds(HALO - (CONV_W - 1), CONV_W - 1), :] = c0_ref[...]

    h = _rms(x_ref[...], g_ref[...]).astype(BF16)
    z = jnp.dot(h, w_ref[...], preferred_element_type=F32)
    xp_scr[pl.ds(HALO, tb), :] = z[:, :GDN_CONV_CH]
    cw = cw_ref[...]
    y = xp_scr[pl.ds(HALO - 3, tb), :] * cw[0:1, :]
    for i in range(1, CONV_W):
        y = y + xp_scr[pl.ds(HALO - 3 + i, tb), :] * cw[i:i + 1, :]
    tail = xp_scr[pl.ds(HALO + tb - (CONV_W - 1), CONV_W - 1), :]
    xp_scr[pl.ds(HALO - (CONV_W - 1), CONV_W - 1), :] = tail

    @pl.when(j == pl.num_programs(1) - 1)
    def _():
        cn_ref[...] = tail

    qkv = _silu(y)
    gones = _group_ones(GDN_QK, GDN_DK)
    q = qkv[:, :GDN_QK]
    k = qkv[:, GDN_QK:2 * GDN_QK]
    q_scr[...] = q * lax.rsqrt(_dot1(q * q, gones) + 1e-6) * (GDN_DK ** -0.5)
    k_scr[...] = k * lax.rsqrt(_dot1(k * k, gones) + 1e-6)
    v_scr[...] = qkv[:, 2 * GDN_QK:]

    zb = z[:, GDN_CONV_CH + GDN_V:GDN_CONV_CH + GDN_V + LANES]
    za = z[:, GDN_CONV_CH + GDN_V + LANES:]
    be_scr[...] = _sigmoid(zb)
    log_a = -jnp.exp(alog_ref[...]) * _softplus(za + dtb_ref[...])
    ga = _dot_exact_lhs(_chunk_tril(tb, chunk), log_a)
    ga_scr[...] = ga
    eye16 = jnp.where(_iota2((2 * SUBLANES, LANES), 0) == _iota2((2 * SUBLANES, LANES), 1),
                      1.0, 0.0).astype(BF16)
    gat = _dot_exact_lhs(eye16, ga, NT)
    for c in range(nc):
        gat_scr[c] = gat[:, c * chunk:(c + 1) * chunk]

    ri = _iota2((chunk, chunk), 0)
    ci = _iota2((chunk, chunk), 1)
    incl = ri >= ci
    strict = ri > ci

    heads = range(GDN_HEADS)
    hsl = [slice(hh * GDN_DK, (hh + 1) * GDN_DK) for hh in heads]

    def phase1(c, carry):
        r0 = pl.multiple_of(c * chunk, chunk)
        k0 = pl.multiple_of(c * GDN_DK, GDN_DK)
        qc = q_scr[pl.ds(r0, chunk), :]
        kc = k_scr[pl.ds(r0, chunk), :]
        vc = v_scr[pl.ds(r0, chunk), :]
        bec = be_scr[pl.ds(r0, chunk), :]
        gac = ga_scr[pl.ds(r0, chunk), :]
        gatc = gat_scr[c]
        glast = gac[chunk - 1:chunk, :]
        egc = jnp.exp(gac)
        ekd = jnp.exp(glast - gac)
        egl_scr[pl.ds(c, 1), :] = jnp.exp(glast)
        q = [qc[:, s] for s in hsl]
        k = [kc[:, s] for s in hsl]
        v = [vc[:, s] for s in hsl]
        bcol = [bec[:, hh:hh + 1] for hh in heads]
        gam = [jnp.where(incl, jnp.exp(jnp.where(incl, gac[:, hh:hh + 1] - gatc[hh:hh + 1, :], 0.0)),
                         0.0) for hh in heads]
        qk = [_mm(_bf(jnp.concatenate([q[hh], k[hh]], axis=0)), _bf(k[hh]), NT) for hh in heads]
        pm = [_bf(qk[hh][:chunk] * gam[hh]) for hh in heads]
        amat = [jnp.where(strict, bcol[hh] * qk[hh][chunk:] * gam[hh], 0.0) for hh in heads]
        tinv = _tri_inv_multi(amat, chunk)
        rhs = [_bf(jnp.concatenate([bcol[hh] * v[hh], (bcol[hh] * egc[:, hh:hh + 1]) * k[hh]],
                                   axis=1)) for hh in heads]
        sol = [_bf(_mm(_bf(tinv[hh]), rhs[hh])) for hh in heads]
        kd = [_bf(k[hh] * ekd[:, hh:hh + 1]) for hh in heads]
        kts = [_mm(kd[hh], sol[hh], TN) for hh in heads]
        ps = [_mm(pm[hh], sol[hh]) for hh in heads]
        for hh in heads:
            n_scr[pl.ds(k0, GDN_DK), hsl[hh]] = kts[hh][:, :GDN_DV]
            m_scr[pl.ds(k0, GDN_DK), hsl[hh]] = kts[hh][:, GDN_DV:]
            o_scr[pl.ds(r0, chunk), hsl[hh]] = ps[hh][:, :GDN_DV]
            qp_scr[pl.ds(r0, chunk), hsl[hh]] = q[hh] * egc[:, hh:hh + 1] - ps[hh][:, GDN_DV:]
        return carry

    lax.fori_loop(0, nc, phase1, 0)

    def phase2(c, carry):
        r0 = pl.multiple_of(c * chunk, chunk)
        k0 = pl.multiple_of(c * GDN_DK, GDN_DK)
        mc = m_scr[pl.ds(k0, GDN_DK), :]
        nn = n_scr[pl.ds(k0, GDN_DK), :]
        qpc = qp_scr[pl.ds(r0, chunk), :]
        eglc = egl_scr[pl.ds(c, 1), :]
        s_old = [s_scr[hh] for hh in heads]
        res = [_mm(_bf(jnp.concatenate([mc[:, hsl[hh]], qpc[:, hsl[hh]]], axis=0)), _bf(s_old[hh]))
               for hh in heads]
        for hh in heads:
            s_scr[hh] = eglc[:, hh:hh + 1] * s_old[hh] - res[hh][:GDN_DK] + nn[:, hsl[hh]]
            o_scr[pl.ds(r0, chunk), hsl[hh]] = o_scr[pl.ds(r0, chunk), hsl[hh]] + res[hh][GDN_DK:]
        return carry

    lax.fori_loop(0, nc, phase2, 0)

    o = o_scr[...]
    ms = _dot1(o * o, _group_ones(GDN_V, GDN_DV)) * (1.0 / GDN_DV)
    gate = z[:, GDN_CONV_CH:GDN_CONV_CH + GDN_V]
    o_ref[...] = (o * lax.rsqrt(ms + EPS) * nrm_ref[...] * _silu(gate)).astype(BF16)

    @pl.when(j == pl.num_programs(1) - 1)
    def _():
        sn_ref[...] = s_scr[...]


def _gdn(x, g, w, cw, alog, dtb, nrm, s0, c0, tb, chunk):
    b, l, _ = x.shape
    st = (GDN_HEADS, GDN_DK, GDN_DV)
    cst = (CONV_W - 1, GDN_CONV_CH)
    return pl.pallas_call(
        functools.partial(_gdn_body, tb=tb, chunk=chunk),
        grid=(b, l // tb),
        in_specs=[
            pl.BlockSpec((None, tb, D_MODEL), lambda i, j: (i, j, 0)),
            _const_spec((1, D_MODEL)),
            _const_spec(w.shape),
            _const_spec(cw.shape),
            _const_spec((1, LANES)),
            _const_spec((1, LANES)),
            _const_spec((1, GDN_V)),
            pl.BlockSpec((None,) + st, lambda i, j: (i, 0, 0, 0)),
            pl.BlockSpec((None,) + cst, lambda i, j: (i, 0, 0)),
        ],
        out_specs=[
            pl.BlockSpec((None, tb, GDN_V), lambda i, j: (i, j, 0)),
            pl.BlockSpec((None,) + st, lambda i, j: (i, 0, 0, 0)),
            pl.BlockSpec((None,) + cst, lambda i, j: (i, 0, 0)),
        ],
        out_shape=[jax.ShapeDtypeStruct((b, l, GDN_V), BF16),
                   jax.ShapeDtypeStruct((b,) + st, F32),
                   jax.ShapeDtypeStruct((b,) + cst, F32)],
        scratch_shapes=[
            pltpu.VMEM(st, F32),
            pltpu.VMEM((tb + HALO, GDN_CONV_CH), F32),
            pltpu.VMEM((tb, GDN_QK), F32),
            pltpu.VMEM((tb, GDN_QK), F32),
            pltpu.VMEM((tb, GDN_V), F32),
            pltpu.VMEM((tb, LANES), F32),
            pltpu.VMEM((tb, LANES), F32),
            pltpu.VMEM((tb // chunk, 2 * SUBLANES, chunk), F32),
            pltpu.VMEM((tb, GDN_V), F32),
            pltpu.VMEM((tb // chunk * GDN_DK, GDN_QK), F32),
            pltpu.VMEM((tb // chunk * GDN_DK, GDN_V), F32),
            pltpu.VMEM((tb, GDN_QK), F32),
            pltpu.VMEM((max(tb // chunk, SUBLANES), LANES), F32),
        ],
        compiler_params=pltpu.CompilerParams(
            dimension_semantics=("arbitrary", "arbitrary"), vmem_limit_bytes=VMEM_LIMIT),
        name="gdn",
    )(x, g, w, cw, alog, dtb, nrm, s0, c0)


def _rwkv_body(x_ref, g_ref, w_ref, mu_ref, w0_ref, w2_ref, a0_ref, a2_ref, g2_ref,
               kkw_ref, ka_ref, rk_ref, gn_ref, s0_ref, sh0_ref,
               o_ref, sn_ref, shn_ref,
               s_scr, zc_scr, r_scr, k2_scr, v_scr, kk_scr, kka_scr, lw_scr, cum_scr, o_scr,
               qp_scr, m_scr, n_scr, wt_scr, *, tb, chunk):
    j = pl.program_id(1)
    nc = tb // chunk

    @pl.when(j == 0)
    def _():
        s_scr[...] = s0_ref[...]
        zc_scr[pl.ds(HALO - 1, 1), :] = sh0_ref[...]

    h = _rms(x_ref[...], g_ref[...]).astype(BF16)
    z = jnp.dot(h, w_ref[...], preferred_element_type=F32)
    zc_scr[pl.ds(HALO, tb), :] = z
    zp = zc_scr[pl.ds(HALO - 1, tb), :]
    last = zc_scr[pl.ds(HALO + tb - 1, 1), :]
    zc_scr[pl.ds(HALO - 1, 1), :] = last

    @pl.when(j == pl.num_programs(1) - 1)
    def _():
        shn_ref[...] = last

    cs = z + (zp - z) * mu_ref[...]
    r = cs[:, :RWKV_W]
    k = cs[:, RWKV_W:2 * RWKV_W]
    v = cs[:, 2 * RWKV_W:3 * RWKV_W]
    zwa = cs[:, 3 * RWKV_W:3 * RWKV_W + LANES]
    zg = cs[:, 3 * RWKV_W + LANES:]
    w_raw = -_softplus(-(w0_ref[...] + _dot1(jnp.tanh(zwa), w2_ref[...]))) - 0.5
    lw = -jnp.exp(w_raw)
    a = _sigmoid(a0_ref[...] + _dot1(zwa, a2_ref[...]))
    g = _dot1(_sigmoid(zg), g2_ref[...])
    gones = _group_ones(RWKV_W, RWKV_N)
    kkn = k * kkw_ref[...]
    kkn = kkn * lax.rsqrt(_dot1(kkn * kkn, gones) + 1e-6)
    k2 = k * (1.0 + (a - 1.0) * ka_ref[...])
    bonus = _dot1(r * k2 * rk_ref[...], gones) * v

    r_scr[...] = r
    k2_scr[...] = k2
    v_scr[...] = v
    kk_scr[...] = kkn
    kka_scr[...] = kkn * a
    lw_scr[...] = lw
    cum_scr[...] = _dot_exact_lhs(_chunk_tril(tb, chunk), lw)

    ri = _iota2((chunk, chunk), 0)
    ci = _iota2((chunk, chunk), 1)
    incl = ri >= ci
    strict = ri > ci
    heads = range(RWKV_HEADS)
    hsl = [slice(hh * RWKV_N, (hh + 1) * RWKV_N) for hh in heads]

    def phase1(c, carry):
        r0 = pl.multiple_of(c * chunk, chunk)
        k0 = pl.multiple_of(c * RWKV_N, RWKV_N)
        rows = pl.ds(r0, chunk)
        cum = cum_scr[rows, :]
        tot = cum[chunk - 1:chunk, :]
        w_inv = jnp.exp(-cum)
        w_rem = jnp.exp(tot - cum)
        kkc = kk_scr[rows, :]
        kkac = kka_scr[rows, :]
        k2c = k2_scr[rows, :]
        vc = v_scr[rows, :]
        atc = -kkc * jnp.exp(cum - lw_scr[rows, :])
        rtc = r_scr[rows, :] * jnp.exp(cum)
        btc = _bf(kkac * w_inv)
        ktc = _bf(k2c * w_inv)
        bhc = _bf(kkac * w_rem)
        khc = _bf(k2c * w_rem)
        vcb = _bf(vc)
        at = [atc[:, s] for s in hsl]
        rt = [rtc[:, s] for s in hsl]
        vb = [vcb[:, s] for s in hsl]
        atrt = [_bf(jnp.concatenate([at[hh], rt[hh]], axis=0)) for hh in heads]
        xb = [_mm(atrt[hh], btc[:, hsl[hh]], NT) for hh in heads]
        xk = [_mm(atrt[hh], ktc[:, hsl[hh]], NT) for hh in heads]
        neg_ab = [jnp.where(strict, -xb[hh][:chunk], 0.0) for hh in heads]
        a_rb = [_bf(jnp.where(incl, xb[hh][chunk:], 0.0)) for hh in heads]
        a_ak = [_bf(jnp.where(strict, xk[hh][:chunk], 0.0)) for hh in heads]
        a_rk = [_bf(jnp.where(incl, xk[hh][chunk:], 0.0)) for hh in heads]
        tinv = _tri_inv_multi(neg_ab, chunk)
        akv = [_mm(a_ak[hh], vb[hh]) for hh in heads]
        pu = [_bf(_mm(_bf(tinv[hh]), _bf(jnp.concatenate([at[hh], akv[hh]], axis=1))))
              for hh in heads]
        rbp = [_mm(a_rb[hh], pu[hh]) for hh in heads]
        rkv = [_mm(a_rk[hh], vb[hh]) for hh in heads]
        bp = [_mm(pu[hh], bhc[:, hsl[hh]], TN) for hh in heads]
        vk = [_mm(vb[hh], khc[:, hsl[hh]], TN) for hh in heads]
        for hh in heads:
            qp_scr[rows, hsl[hh]] = rt[hh] + rbp[hh][:, :RWKV_N]
            o_scr[rows, hsl[hh]] = rbp[hh][:, RWKV_N:] + rkv[hh]
            m_scr[pl.ds(k0, RWKV_N), hsl[hh]] = bp[hh][:RWKV_N]
            n_scr[pl.ds(k0, RWKV_N), hsl[hh]] = bp[hh][RWKV_N:] + vk[hh]
        wt_scr[pl.ds(c, 1), :] = jnp.exp(tot)
        return carry

    lax.fori_loop(0, nc, phase1, 0)

    def phase2(c, carry):
        r0 = pl.multiple_of(c * chunk, chunk)
        k0 = pl.multiple_of(c * RWKV_N, RWKV_N)
        rows = pl.ds(r0, chunk)
        mc = _bf(m_scr[pl.ds(k0, RWKV_N), :])
        nn = n_scr[pl.ds(k0, RWKV_N), :]
        qpc = _bf(qp_scr[rows, :])
        wtc = wt_scr[pl.ds(c, 1), :]
        s_old = [s_scr[hh] for hh in heads]
        sb = [_bf(s) for s in s_old]
        od = [_mm(qpc[:, hsl[hh]], sb[hh], NT) for hh in heads]
        sm = [_mm(sb[hh], mc[:, hsl[hh]]) for hh in heads]
        for hh in heads:
            s_scr[hh] = s_old[hh] * wtc[:, hsl[hh]] + sm[hh] + nn[:, hsl[hh]]
            o_scr[rows, hsl[hh]] = o_scr[rows, hsl[hh]] + od[hh]
        return carry

    lax.fori_loop(0, nc, phase2, 0)

    o = o_scr[...]
    mean = _dot1(o, gones) * (1.0 / RWKV_N)
    dlt = o - mean
    var = _dot1(dlt * dlt, gones) * (1.0 / RWKV_N)
    y = dlt * lax.rsqrt(var + RWKV_GN_EPS) * gn_ref[...]
    o_ref[...] = ((y + bonus) * g).astype(BF16)

    @pl.when(j == pl.num_programs(1) - 1)
    def _():
        sn_ref[...] = s_scr[...]


def _rwkv(x, g, w, mu, w0, w2, a0, a2, g2, kkw, ka, rk, gn, s0, sh0, tb, chunk):
    b, l, _ = x.shape
    st = (RWKV_HEADS, RWKV_N, RWKV_N)
    sst = (1, RWKV_SHIFT_CH)
    row = _const_spec((1, RWKV_W))
    big = pltpu.VMEM((tb, RWKV_W), F32)
    return pl.pallas_call(
        functools.partial(_rwkv_body, tb=tb, chunk=chunk),
        grid=(b, l // tb),
        in_specs=[
            pl.BlockSpec((None, tb, D_MODEL), lambda i, j: (i, j, 0)),
            _const_spec((1, D_MODEL)),
            _const_spec(w.shape),
            _const_spec((1, RWKV_SHIFT_CH)),
            row, _const_spec((LANES, RWKV_W)), row, _const_spec((LANES, RWKV_W)),
            _const_spec((LANES, RWKV_W)),
            row, row, row, row,
            pl.BlockSpec((None,) + st, lambda i, j: (i, 0, 0, 0)),
            pl.BlockSpec((None,) + sst, lambda i, j: (i, 0, 0)),
        ],
        out_specs=[
            pl.BlockSpec((None, tb, RWKV_W), lambda i, j: (i, j, 0)),
            pl.BlockSpec((None,) + st, lambda i, j: (i, 0, 0, 0)),
            pl.BlockSpec((None,) + sst, lambda i, j: (i, 0, 0)),
        ],
        out_shape=[jax.ShapeDtypeStruct((b, l, RWKV_W), BF16),
                   jax.ShapeDtypeStruct((b,) + st, F32),
                   jax.ShapeDtypeStruct((b,) + sst, F32)],
        scratch_shapes=[
            pltpu.VMEM(st, F32),
            pltpu.VMEM((tb + HALO, RWKV_SHIFT_CH), F32),
            big, big, big, big, big, big, big, big, big,
            pltpu.VMEM((tb // chunk * RWKV_N, RWKV_W), F32),
            pltpu.VMEM((tb // chunk * RWKV_N, RWKV_W), F32),
            pltpu.VMEM((max(tb // chunk, SUBLANES), RWKV_W), F32),
        ],
        compiler_params=pltpu.CompilerParams(
            dimension_semantics=("arbitrary", "arbitrary"), vmem_limit_bytes=VMEM_LIMIT),
        name="rwkv7",
    )(x, g, w, mu, w0, w2, a0, a2, g2, kkw, ka, rk, gn, s0, sh0)


def _merge_body(x_ref, g_ref, wg_ref, oa_ref, ob_ref, oc_ref, wb_ref, wo_ref, o_ref):
    x = x_ref[...]
    h = _rms(x, g_ref[...]).astype(BF16)
    gate = jnp.dot(h, wg_ref[...], preferred_element_type=F32)
    m = None
    for i, br in enumerate((oa_ref, ob_ref, oc_ref)):
        t = _sigmoid(gate[:, i * D_MODEL:(i + 1) * D_MODEL]) * jnp.dot(
            br[...], wb_ref[i], preferred_element_type=F32)
        m = t if m is None else m + t
    o_ref[...] = x + jnp.dot(m.astype(BF16), wo_ref[...], preferred_element_type=F32)


def _merge(x2d, g, wg, oa, ob, oc, wb, wo):
    m = x2d.shape[0]
    tm = min(256, m)
    bw = oa.shape[1]
    tok = lambda w: pl.BlockSpec((tm, w), lambda i: (i, 0))
    return pl.pallas_call(
        _merge_body,
        grid=(m // tm,),
        in_specs=[tok(D_MODEL), _const_spec((1, D_MODEL)), _const_spec(wg.shape),
                  tok(bw), tok(bw), tok(bw), _const_spec(wb.shape), _const_spec(wo.shape)],
        out_specs=tok(D_MODEL),
        out_shape=jax.ShapeDtypeStruct((m, D_MODEL), F32),
        compiler_params=pltpu.CompilerParams(
            dimension_semantics=("arbitrary",), vmem_limit_bytes=VMEM_LIMIT),
        name="merge",
    )(x2d, g, wg, oa, ob, oc, wb, wo)


def _rope_tables(pos0, l):
    half = RET_DK // 2
    inv = ROPE_BASE ** (-jnp.arange(half, dtype=F32) / half)
    ang = (pos0 + jnp.arange(l)).astype(F32)[:, None] * inv[None, :]
    cos = jnp.cos(ang)
    sin = jnp.sin(ang)
    cos = jnp.tile(jnp.concatenate([cos, cos], axis=1), (1, RET_HEADS))
    sin = jnp.tile(jnp.concatenate([-sin, sin], axis=1), (1, RET_HEADS))
    return cos, sin


def _prep_weights(p):
    w_in = p["w_in"]
    depth = w_in.shape[0]
    o = 0
    pieces = {}
    for name, size in (("a_qkvg", 2 * RET_QK + 2 * RET_W), ("b_qkv", GDN_CONV_CH), ("b_g", GDN_V),
                       ("b_beta", GDN_HEADS), ("b_alpha", GDN_HEADS), ("c_z", RWKV_SHIFT_CH),
                       ("gate", 3 * D_MODEL)):
        pieces[name] = w_in[:, :, o:o + size]
        o += size
    pad_l = lambda t, n: jnp.pad(t, ((0, 0), (0, 0), (0, n - t.shape[2])))
    row = lambda t: t[:, None, :]
    zeros64 = jnp.zeros((depth, 64, RWKV_W), F32)
    return dict(
        norm_ffn1=row(p["norm_ffn1"]), norm_mix=row(p["norm_mix"]), norm_ffn2=row(p["norm_ffn2"]),
        ffn1_w13=p["ffn1_w13"].astype(BF16), ffn1_w2=p["ffn1_w2"].astype(BF16),
        ffn2_w13=p["ffn2_w13"].astype(BF16), ffn2_w2=p["ffn2_w2"].astype(BF16),
        w_ret=pieces["a_qkvg"].astype(BF16),
        w_gdn=jnp.concatenate([pieces["b_qkv"], pieces["b_g"], pad_l(pieces["b_beta"], LANES),
                               pad_l(pieces["b_alpha"], LANES)], axis=2).astype(BF16),
        w_rwkv=pieces["c_z"].astype(BF16),
        w_gate=pieces["gate"].astype(BF16),
        ret_gn=row(p["ret_gn"]),
        gdn_conv=p["gdn_conv"],
        gdn_a_log=row(jnp.pad(p["gdn_a_log"], ((0, 0), (0, LANES - GDN_HEADS)))),
        gdn_dt_bias=row(jnp.pad(p["gdn_dt_bias"], ((0, 0), (0, LANES - GDN_HEADS)))),
        gdn_norm=row(jnp.tile(p["gdn_norm"], (1, GDN_HEADS))),
        rwkv_mu=row(p["rwkv_mu"]),
        rwkv_w0=row(p["rwkv_w0"]),
        rwkv_w2=jnp.concatenate([p["rwkv_w2"], zeros64], axis=1),
        rwkv_a0=row(p["rwkv_a0"]),
        rwkv_a2=jnp.concatenate([zeros64, p["rwkv_a2"]], axis=1),
        rwkv_g2=p["rwkv_g2"],
        rwkv_kk=row(p["rwkv_kk"]), rwkv_ka=row(p["rwkv_ka"]), rwkv_rk=row(p["rwkv_rk"]),
        rwkv_gn=row(p["rwkv_gn"]),
        w_branch=p["w_branch"].astype(BF16), w_out=p["w_out"].astype(BF16),
    )


def _layer(x, st, lp, tables, gf, final_norm, tb, chunk):
    b, l, d = x.shape
    s_ret, s_delta, s_conv, s_wkv, s_shift = st
    cos, sin = tables
    x2 = _ffn(x.reshape(b * l, d), lp["norm_ffn1"], lp["ffn1_w13"], lp["ffn1_w2"], gf, False)
    x3 = x2.reshape(b, l, d)
    oa, n_ret = _retention(x3, lp["norm_mix"], lp["w_ret"], cos, sin, lp["ret_gn"], s_ret, tb, chunk)
    ob, n_delta, n_conv = _gdn(x3, lp["norm_mix"], lp["w_gdn"], lp["gdn_conv"], lp["gdn_a_log"],
                               lp["gdn_dt_bias"], lp["gdn_norm"], s_delta, s_conv, tb, chunk)
    oc, n_wkv, n_shift = _rwkv(x3, lp["norm_mix"], lp["w_rwkv"], lp["rwkv_mu"], lp["rwkv_w0"],
                               lp["rwkv_w2"], lp["rwkv_a0"], lp["rwkv_a2"], lp["rwkv_g2"],
                               lp["rwkv_kk"], lp["rwkv_ka"], lp["rwkv_rk"], lp["rwkv_gn"],
                               s_wkv, s_shift, tb, chunk)
    bw = oa.shape[-1]
    x4 = _merge(x2, lp["norm_mix"], lp["w_gate"], oa.reshape(b * l, bw), ob.reshape(b * l, bw),
                oc.reshape(b * l, bw), lp["w_branch"], lp["w_out"])
    x5 = _ffn(x4, lp["norm_ffn2"], lp["ffn2_w13"], lp["ffn2_w2"], gf, final_norm)
    return x5.reshape(b, l, d), (n_ret, n_delta, n_conv, n_wkv, n_shift)


def _run_stream(x, pos0, states, wts, gf, tb):
    depth = wts["w_ret"].shape[0]
    l = x.shape[1]
    chunk = min(CHUNK, l)
    tb = min(tb, l)
    tables = _rope_tables(pos0, l)
    new = [[] for _ in range(5)]
    for li in range(depth):
        lp = {k: v[li] for k, v in wts.items()}
        st = tuple(s[li] for s in states)
        x, nst = _layer(x, st, lp, tables, gf, li == depth - 1, tb, chunk)
        for jdx in range(5):
            new[jdx].append(nst[jdx])
    return x, tuple(jnp.stack(t, axis=0) for t in new)


def kernel(x_prompt, x_sample, state_ret, state_delta, state_conv, state_wkv, state_shift,
           norm_ffn1, ffn1_w13, ffn1_w2, norm_mix, w_in, ret_gn, gdn_conv, gdn_a_log, gdn_dt_bias,
           gdn_norm, rwkv_mu, rwkv_w0, rwkv_w2, rwkv_a0, rwkv_a2, rwkv_g2, rwkv_kk, rwkv_ka, rwkv_rk,
           rwkv_gn, w_branch, w_out, norm_ffn2, ffn2_w13, ffn2_w2, norm_final):
    params = dict(
        norm_ffn1=norm_ffn1, ffn1_w13=ffn1_w13, ffn1_w2=ffn1_w2, norm_mix=norm_mix, w_in=w_in,
        ret_gn=ret_gn, gdn_conv=gdn_conv, gdn_a_log=gdn_a_log, gdn_dt_bias=gdn_dt_bias,
        gdn_norm=gdn_norm, rwkv_mu=rwkv_mu, rwkv_w0=rwkv_w0, rwkv_w2=rwkv_w2, rwkv_a0=rwkv_a0,
        rwkv_a2=rwkv_a2, rwkv_g2=rwkv_g2, rwkv_kk=rwkv_kk, rwkv_ka=rwkv_ka, rwkv_rk=rwkv_rk,
        rwkv_gn=rwkv_gn, w_branch=w_branch, w_out=w_out, norm_ffn2=norm_ffn2, ffn2_w13=ffn2_w13,
        ffn2_w2=ffn2_w2)
    wts = _prep_weights(params)
    gf = norm_final[None, :]
    depth = w_in.shape[0]
    bp = x_prompt.shape[0]
    zero_states = (
        jnp.zeros((depth, bp, RET_HEADS, RET_DK, RET_DV), F32),
        jnp.zeros((depth, bp, GDN_HEADS, GDN_DK, GDN_DV), F32),
        jnp.zeros((depth, bp, CONV_W - 1, GDN_CONV_CH), F32),
        jnp.zeros((depth, bp, RWKV_HEADS, RWKV_N, RWKV_N), F32),
        jnp.zeros((depth, bp, 1, RWKV_SHIFT_CH), F32),
    )
    y_p, st_p = _run_stream(x_prompt, 0, zero_states, wts, gf, 256)
    y_s, st_s = _run_stream(x_sample, PAST_LEN, (state_ret, state_delta, state_conv, state_wkv,
                                                 state_shift), wts, gf, 256)
    return (y_p, y_s) + st_p + st_s
```

```python
import functools
import math

import jax
import jax.numpy as jnp
from jax import lax
from jax.experimental import pallas as pl
from jax.experimental.pallas import tpu as pltpu

F32 = jnp.float32
BF16 = jnp.bfloat16

D_MODEL = 1024
D_FF = 2816
EPS = 1e-6
CHUNK = 64
PAST_LEN = 2048
RET_HEADS, RET_DK, RET_DV = 4, 64, 128
ROPE_BASE = 10000.0
GDN_HEADS, GDN_DK, GDN_DV, CONV_W = 8, 64, 64, 4
RWKV_HEADS, RWKV_N = 8, 64
RWKV_GN_EPS = 64e-5
RET_QK = RET_HEADS * RET_DK
RET_W = RET_HEADS * RET_DV
GDN_QK = GDN_HEADS * GDN_DK
GDN_V = GDN_HEADS * GDN_DV
GDN_CONV_CH = 2 * GDN_QK + GDN_V
RWKV_W = RWKV_HEADS * RWKV_N
RWKV_SHIFT_CH = 3 * RWKV_W + 256
LANES = 128
SUBLANES = 8
VMEM_LIMIT = 56 * 1024 * 1024
HEAD = 64
PAIRS = 4
CORE = 64
SEQ_BLOCK = 256
TOKEN_TILE = 256
assert GDN_DK == GDN_DV == RWKV_N == HEAD and GDN_HEADS == RWKV_HEADS == 2 * PAIRS


NT = (((1,), (1,)), ((), ()))
TN = (((0,), (0,)), ((), ()))
NN = (((1,), (0,)), ((), ()))


def _bf(x):
    return x.astype(BF16)


def _mm(a, b, dims=NN):
    return lax.dot_general(a, b, dims, preferred_element_type=F32)


def _dot1(a, b, dims=NN):
    return _mm(_bf(a), _bf(b), dims)


def _split3(x):
    x0 = _bf(x)
    r1 = x - x0.astype(F32)
    x1 = _bf(r1)
    x2 = _bf(r1 - x1.astype(F32))
    return x0, x1, x2


def _dot_exact_rhs(a, b_bf16):
    a0, a1, a2 = _split3(a)
    return _mm(a0, b_bf16) + (_mm(a1, b_bf16) + _mm(a2, b_bf16))


def _dot_exact_lhs(a_bf16, b):
    b0, b1, b2 = _split3(b)
    return _mm(a_bf16, b0) + (_mm(a_bf16, b1) + _mm(a_bf16, b2))


def _rms(x, g):
    return x * lax.rsqrt(jnp.mean(x * x, axis=-1, keepdims=True) + EPS) * g


def _sigmoid(x):
    return 1.0 / (1.0 + jnp.exp(-x))


def _silu(x):
    return x * _sigmoid(x)


def _softplus(x):
    return jnp.maximum(x, 0.0) + jnp.log(1.0 + jnp.exp(-jnp.abs(x)))


def _iota2(shape, dim):
    return lax.broadcasted_iota(jnp.int32, shape, dim)


def _group_ones(width, group):
    r = _iota2((width, width), 0) // group
    c = _iota2((width, width), 1) // group
    return jnp.where(r == c, 1.0, 0.0).astype(BF16)


def _head_expand(width):
    r = _iota2((LANES, width), 0)
    c = _iota2((LANES, width), 1) // HEAD
    return jnp.where(r == c, 1.0, 0.0).astype(BF16)


def _chunk_tril(tb, chunk):
    r = _iota2((tb, tb), 0)
    c = _iota2((tb, tb), 1)
    return jnp.where(((r // chunk) == (c // chunk)) & (c <= r), 1.0, 0.0).astype(BF16)


def _pad_rows(x, rows):
    if x.shape[0] == rows:
        return x
    return jnp.concatenate([x, jnp.zeros((rows - x.shape[0], x.shape[1]), x.dtype)], axis=0)


def _shift_rows(x, prev, i):
    rolled = pltpu.roll(x, i, 0)
    first = jnp.where(_iota2((SUBLANES, x.shape[1]), 0) < i, pltpu.roll(prev, i, 0),
                      rolled[:SUBLANES])
    return jnp.concatenate([first, rolled[SUBLANES:]], axis=0)


def _bd(y, left):
    z = jnp.zeros_like(y)
    return jnp.concatenate([jnp.where(left, y, z), jnp.where(left, z, y)], axis=0)


def _diag(r, left):
    return jnp.where(left, r[:HEAD], r[HEAD:])


def _tri_inv_sbs(a_list, left):
    r = _iota2((CORE, LANES), 0)
    c = _iota2((CORE, LANES), 1) % HEAD
    eye = jnp.where(r == c, 1.0, 0.0).astype(F32)
    base = 16
    blk = (r // base) == (c // base)
    a0f = [jnp.where(blk, a, 0.0) for a in a_list]
    a0 = [_bf(x) for x in a0f]
    bd0 = [_bd(x, left) for x in a0]
    a2 = [_bf(_mm(x, y)) for x, y in zip(a0, bd0)]
    bd2 = [_bd(x, left) for x in a2]
    a4 = [_bf(_mm(x, y)) for x, y in zip(a2, bd2)]
    bd4 = [_bd(x, left) for x in a4]
    a8 = [_bf(_mm(x, y)) for x, y in zip(a4, bd4)]
    bd8 = [_bd(x, left) for x in a8]
    m = [eye - x for x in a0f]
    for bdp in (bd2, bd4, bd8):
        m = [mi + _mm(_bf(mi), y) for mi, y in zip(m, bdp)]
    size = 2 * base
    while size <= CORE:
        half = size // 2
        sel = ((r // size) == (c // size)) & ((r // half) != (c // half))
        bdb = [_bd(_bf(jnp.where(sel, a, 0.0)), left) for a in a_list]
        mb = [_bf(mi) for mi in m]
        t = [_bf(_mm(x, y)) for x, y in zip(mb, bdb)]
        m = [mi - _mm(x, _bd(y, left)) for mi, x, y in zip(m, t, mb)]
        size *= 2
    return m


def _const_spec(shape):
    nd = len(shape)
    return pl.BlockSpec(shape, lambda *_: (0,) * nd, pipeline_mode=pl.Buffered(1))


def _ffn_body(x_ref, g_ref, w13_ref, w2_ref, gf_ref, o_ref, *, final_norm):
    x = x_ref[...]
    h = _rms(x, g_ref[...]).astype(BF16)
    ab = jnp.dot(h, w13_ref[...], preferred_element_type=F32)
    act = (_silu(ab[:, :D_FF]) * ab[:, D_FF:]).astype(BF16)
    y = x + 0.5 * jnp.dot(act, w2_ref[...], preferred_element_type=F32)
    if final_norm:
        y = _rms(y, gf_ref[...])
    o_ref[...] = y


def _ffn(x2d, g, w13, w2, gf, final_norm):
    m = x2d.shape[0]
    tm = min(TOKEN_TILE, m)
    return pl.pallas_call(
        functools.partial(_ffn_body, final_norm=final_norm),
        grid=(m // tm,),
        in_specs=[
            pl.BlockSpec((tm, D_MODEL), lambda i: (i, 0)),
            _const_spec((1, D_MODEL)),
            _const_spec((D_MODEL, 2 * D_FF)),
            _const_spec((D_FF, D_MODEL)),
            _const_spec((1, D_MODEL)),
        ],
        out_specs=pl.BlockSpec((tm, D_MODEL), lambda i: (i, 0)),
        out_shape=jax.ShapeDtypeStruct((m, D_MODEL), F32),
        compiler_params=pltpu.CompilerParams(
            dimension_semantics=("arbitrary",), vmem_limit_bytes=VMEM_LIMIT),
        name="ffn",
    )(x2d, g, w13, w2, gf)


def _ret_body(x_ref, g_ref, w_ref, cos_ref, sin_ref, gn_ref, s0_ref,
              o_ref, sn_ref, s_scr, q_scr, k_scr, v_scr, o_scr, *, tb, chunk):
    j = pl.program_id(1)

    @pl.when(j == 0)
    def _():
        s_scr[...] = s0_ref[...]

    h = _rms(x_ref[...], g_ref[...]).astype(BF16)
    z = jnp.dot(h, w_ref[...], preferred_element_type=F32)
    q = z[:, :RET_QK]
    k = z[:, RET_QK:2 * RET_QK]
    cos = cos_ref[...]
    sin = sin_ref[...]
    first = (_iota2((tb, RET_QK), 1) % RET_DK) < (RET_DK // 2)

    def rot(t):
        swapped = jnp.where(first, pltpu.roll(t, RET_QK - RET_DK // 2, 1),
                            pltpu.roll(t, RET_DK // 2, 1))
        return t * cos + swapped * sin

    q_scr[...] = rot(q)
    k_scr[...] = rot(k) * (RET_DK ** -0.5)
    v_scr[...] = z[:, 2 * RET_QK:2 * RET_QK + RET_W]

    ri = _iota2((chunk, chunk), 0)
    ci = _iota2((chunk, chunk), 1)
    causal = ri >= ci
    dfi = jnp.where(causal, (ri - ci).astype(F32), 0.0)
    col = _iota2((chunk, 1), 0).astype(F32)
    log_g = [math.log1p(-(2.0 ** (-5.0 - hh))) for hh in range(RET_HEADS)]
    dmask = [jnp.where(causal, jnp.exp(dfi * lg), 0.0) for lg in log_g]
    qdec = [jnp.exp((col + 1.0) * lg) for lg in log_g]
    kdec = [jnp.exp((chunk - 1.0 - col) * lg) for lg in log_g]
    cdec = [math.exp(chunk * lg) for lg in log_g]

    def body(c, carry):
        r0 = pl.multiple_of(c * chunk, chunk)
        qc = q_scr[pl.ds(r0, chunk), :]
        kc = k_scr[pl.ds(r0, chunk), :]
        vc = v_scr[pl.ds(r0, chunk), :]
        for hh in range(RET_HEADS):
            qh = qc[:, hh * RET_DK:(hh + 1) * RET_DK]
            kh = kc[:, hh * RET_DK:(hh + 1) * RET_DK]
            vh = vc[:, hh * RET_DV:(hh + 1) * RET_DV]
            sh = s_scr[hh]
            scores = _dot1(qh, kh, NT) * dmask[hh]
            o = _dot1(scores, vh) + _dot1(qh * qdec[hh], sh)
            s_scr[hh] = cdec[hh] * sh + _dot1(kh * kdec[hh], vh, TN)
            o_scr[pl.ds(r0, chunk), hh * RET_DV:(hh + 1) * RET_DV] = o
        return carry

    lax.fori_loop(0, tb // chunk, body, 0)

    gate = z[:, 2 * RET_QK + RET_W:]
    gn = gn_ref[...]
    for hh in range(RET_HEADS):
        sl = slice(hh * RET_DV, (hh + 1) * RET_DV)
        oh = o_scr[:, sl]
        mu = jnp.mean(oh, axis=-1, keepdims=True)
        dlt = oh - mu
        var = jnp.mean(dlt * dlt, axis=-1, keepdims=True)
        y = dlt * lax.rsqrt(var + 1e-5) * gn[:, sl]
        o_ref[:, sl] = (y * _silu(gate[:, sl])).astype(BF16)

    @pl.when(j == pl.num_programs(1) - 1)
    def _():
        sn_ref[...] = s_scr[...]


def _retention(x, g, w, cos, sin, gn, s0, tb, chunk):
    b, l, _ = x.shape
    st = (RET_HEADS, RET_DK, RET_DV)
    return pl.pallas_call(
        functools.partial(_ret_body, tb=tb, chunk=chunk),
        grid=(b, l // tb),
        in_specs=[
            pl.BlockSpec((None, tb, D_MODEL), lambda i, j: (i, j, 0)),
            _const_spec((1, D_MODEL)),
            _const_spec(w.shape),
            pl.BlockSpec((tb, RET_QK), lambda i, j: (j, 0)),
            pl.BlockSpec((tb, RET_QK), lambda i, j: (j, 0)),
            _const_spec((1, RET_W)),
            pl.BlockSpec((None,) + st, lambda i, j: (i, 0, 0, 0)),
        ],
        out_specs=[
            pl.BlockSpec((None, tb, RET_W), lambda i, j: (i, j, 0)),
            pl.BlockSpec((None,) + st, lambda i, j: (i, 0, 0, 0)),
        ],
        out_shape=[jax.ShapeDtypeStruct((b, l, RET_W), BF16),
                   jax.ShapeDtypeStruct((b,) + st, F32)],
        scratch_shapes=[
            pltpu.VMEM(st, F32),
            pltpu.VMEM((tb, RET_QK), F32),
            pltpu.VMEM((tb, RET_QK), F32),
            pltpu.VMEM((tb, RET_W), F32),
            pltpu.VMEM((tb, RET_W), F32),
        ],
        compiler_params=pltpu.CompilerParams(
            dimension_semantics=("arbitrary", "arbitrary"), vmem_limit_bytes=VMEM_LIMIT),
        name="retention",
    )(x, g, w, cos, sin, gn, s0)


HALO = SUBLANES


def _core_masks():
    r = _iota2((CORE, LANES), 0)
    c = _iota2((CORE, LANES), 1) % HEAD
    left = _iota2((CORE, LANES), 1) < HEAD
    return left, r >= c, r > c


def _load_state_sbs(s0_ref, s_scr):
    for p in range(PAIRS):
        s_scr[p] = jnp.concatenate([s0_ref[2 * p], s0_ref[2 * p + 1]], axis=1)


def _store_state_sbs(s_scr, sn_ref):
    for p in range(PAIRS):
        s = s_scr[p]
        sn_ref[2 * p] = s[:, :HEAD]
        sn_ref[2 * p + 1] = s[:, HEAD:]


def _chunk_groups(nc):
    return 2 if nc % 2 == 0 else 1


GDN_COLS = GDN_CONV_CH + GDN_V + 2 * LANES


def _gdn_body(x_ref, g_ref, w_ref, cw_ref, alog_ref, dtb_ref, nrm_ref, s0_ref, c0_ref,
              o_ref, sn_ref, cn_ref,
              s_scr, xp_scr, q_scr, k_scr, v_scr, be_scr, ga_scr, o_scr, m_scr, n_scr, qp_scr,
              gar_scr, egl_scr, *, tb):
    j = pl.program_id(1)
    tbp = max(tb, CORE)
    nc = tbp // CORE
    grp = _chunk_groups(nc)

    @pl.when(j == 0)
    def _():
        _load_state_sbs(s0_ref, s_scr)
        xp_scr[...] = jnp.zeros_like(xp_scr)
        xp_scr[pl.ds(HALO - (CONV_W - 1), CONV_W - 1), :] = c0_ref[...]

    h = _rms(x_ref[...], g_ref[...]).astype(BF16)
    z = jnp.dot(h, w_ref[...], preferred_element_type=F32)
    zq = z[:, :GDN_CONV_CH]
    prev = xp_scr[...]
    cw = cw_ref[...]
    y = zq * cw[CONV_W - 1:CONV_W, :]
    for i in range(1, CONV_W):
        y = y + _shift_rows(zq, prev, i) * cw[CONV_W - 1 - i:CONV_W - i, :]
    xp_scr[...] = zq[tb - HALO:, :]

    @pl.when(j == pl.num_programs(1) - 1)
    def _():
        cn_ref[...] = xp_scr[pl.ds(HALO - (CONV_W - 1), CONV_W - 1), :]

    qkv = _silu(y)
    gones = _group_ones(GDN_QK, GDN_DK)
    q = qkv[:, :GDN_QK]
    k = qkv[:, GDN_QK:2 * GDN_QK]
    q = _pad_rows(q * lax.rsqrt(_dot1(q * q, gones) + 1e-6) * (GDN_DK ** -0.5), tbp)
    k = _pad_rows(k * lax.rsqrt(_dot1(k * k, gones) + 1e-6), tbp)
    v = _pad_rows(qkv[:, 2 * GDN_QK:], tbp)

    zb = z[:, GDN_CONV_CH + GDN_V:GDN_CONV_CH + GDN_V + LANES]
    za = z[:, GDN_CONV_CH + GDN_V + LANES:]
    beta = _pad_rows(_sigmoid(zb), tbp)
    log_a = _pad_rows(-jnp.exp(alog_ref[...]) * _softplus(za + dtb_ref[...]), tbp)
    ga = _dot_exact_lhs(_chunk_tril(tbp, CORE), log_a)
    expand = _head_expand(GDN_V)
    gae = _dot_exact_rhs(ga, expand)
    q_scr[...] = q
    k_scr[...] = k
    v_scr[...] = v
    be_scr[...] = _dot_exact_rhs(beta, expand)
    ga_scr[...] = gae
    rr = _iota2((tbp, GDN_V), 0)
    cc = _iota2((tbp, GDN_V), 1)
    pick = jnp.where((rr % CORE) == (cc % HEAD), gae, 0.0)
    ncp = gar_scr.shape[0]
    chunk_of = jnp.where(_iota2((ncp, tbp), 0) == _iota2((ncp, tbp), 1) // CORE, 1.0, 0.0)
    gar_scr[...] = _dot_exact_lhs(_bf(chunk_of), pick)

    left, incl, strict = _core_masks()
    incl4 = jnp.concatenate([incl] * PAIRS, axis=1)
    psl = [slice(p * LANES, (p + 1) * LANES) for p in range(PAIRS)]

    def phase1(i, carry):
        probs = []
        for gi in range(grp):
            c = i * grp + gi
            rows = pl.ds(pl.multiple_of(c * CORE, CORE), CORE)
            gac = ga_scr[rows, :]
            glast = gac[CORE - 1:CORE, :]
            egl_scr[pl.ds(c, 1), :] = jnp.exp(glast)
            eg = jnp.exp(gac)
            gam = jnp.where(incl4, jnp.exp(jnp.where(incl4, gac - gar_scr[pl.ds(c, 1), :], 0.0)), 0.0)
            qc = q_scr[rows, :]
            kc = k_scr[rows, :]
            bec = be_scr[rows, :]
            probs.append(dict(rows=rows, gam=gam, be=bec, q=_bf(qc), k=_bf(kc), qe=qc * eg,
                              kd=_bf(kc * jnp.exp(glast - gac)), bv=_bf(bec * v_scr[rows, :]),
                              bek=_bf(bec * eg * kc)))
        items = [(pr, s) for pr in probs for s in psl]
        qk = [_mm(jnp.concatenate([pr["q"][:, s], pr["k"][:, s]], axis=0), _bd(pr["k"][:, s], left), NT)
              for pr, s in items]
        pm = [_bf(x[:CORE] * pr["gam"][:, s]) for x, (pr, s) in zip(qk, items)]
        amat = [jnp.where(strict, pr["be"][:, s] * x[CORE:] * pr["gam"][:, s], 0.0)
                for x, (pr, s) in zip(qk, items)]
        tinv = _tri_inv_sbs(amat, left)
        sol = [_bf(_mm(_bf(t), jnp.concatenate([_bd(pr["bv"][:, s], left), _bd(pr["bek"][:, s], left)],
                                               axis=1)))
               for t, (pr, s) in zip(tinv, items)]
        kts = [_mm(pr["kd"][:, s], x, TN) for x, (pr, s) in zip(sol, items)]
        ps = [_mm(p_, jnp.concatenate([_bd(x[:, :LANES], left), _bd(x[:, LANES:], left)], axis=1))
              for p_, x in zip(pm, sol)]
        for n, (pr, s) in enumerate(items):
            n_scr[pr["rows"], s] = _diag(kts[n][:, :LANES], left)
            m_scr[pr["rows"], s] = _diag(kts[n][:, LANES:], left)
            o_scr[pr["rows"], s] = ps[n][:, :LANES]
            qp_scr[pr["rows"], s] = pr["qe"][:, s] - ps[n][:, LANES:]
        return carry

    lax.fori_loop(0, nc // grp, phase1, 0)

    def phase2(c, carry):
        rows = pl.ds(pl.multiple_of(c * CORE, CORE), CORE)
        mc = _bf(m_scr[rows, :])
        nn = n_scr[rows, :]
        qpc = _bf(qp_scr[rows, :])
        eglc = egl_scr[pl.ds(c, 1), :]
        s_old = [s_scr[p] for p in range(PAIRS)]
        res = [_mm(jnp.concatenate([mc[:, psl[p]], qpc[:, psl[p]]], axis=0), _bd(_bf(s_old[p]), left))
               for p in range(PAIRS)]
        for p in range(PAIRS):
            s_scr[p] = eglc[:, psl[p]] * s_old[p] - res[p][:CORE] + nn[:, psl[p]]
        o_scr[rows, :] = o_scr[rows, :] + jnp.concatenate([x[CORE:] for x in res], axis=1)
        return carry

    lax.fori_loop(0, nc, phase2, 0)

    o = o_scr[pl.ds(0, tb), :]
    ms = _dot1(o * o, _group_ones(GDN_V, GDN_DV)) * (1.0 / GDN_DV)
    gate = z[:, GDN_CONV_CH:GDN_CONV_CH + GDN_V]
    o_ref[...] = (o * lax.rsqrt(ms + EPS) * nrm_ref[...] * _silu(gate)).astype(BF16)

    @pl.when(j == pl.num_programs(1) - 1)
    def _():
        _store_state_sbs(s_scr, sn_ref)


def _gdn(x, g, w, cw, alog, dtb, nrm, s0, c0, tb):
    b, l, _ = x.shape
    st = (GDN_HEADS, GDN_DK, GDN_DV)
    cst = (CONV_W - 1, GDN_CONV_CH)
    tbp = max(tb, CORE)
    big = pltpu.VMEM((tbp, GDN_V), F32)
    small = pltpu.VMEM((max(tbp // CORE, SUBLANES), GDN_V), F32)
    return pl.pallas_call(
        functools.partial(_gdn_body, tb=tb),
        grid=(b, l // tb),
        in_specs=[
            pl.BlockSpec((None, tb, D_MODEL), lambda i, j: (i, j, 0)),
            _const_spec((1, D_MODEL)),
            _const_spec(w.shape),
            _const_spec(cw.shape),
            _const_spec((1, LANES)),
            _const_spec((1, LANES)),
            _const_spec((1, GDN_V)),
            pl.BlockSpec((None,) + st, lambda i, j: (i, 0, 0, 0)),
            pl.BlockSpec((None,) + cst, lambda i, j: (i, 0, 0)),
        ],
        out_specs=[
            pl.BlockSpec((None, tb, GDN_V), lambda i, j: (i, j, 0)),
            pl.BlockSpec((None,) + st, lambda i, j: (i, 0, 0, 0)),
            pl.BlockSpec((None,) + cst, lambda i, j: (i, 0, 0)),
        ],
        out_shape=[jax.ShapeDtypeStruct((b, l, GDN_V), BF16),
                   jax.ShapeDtypeStruct((b,) + st, F32),
                   jax.ShapeDtypeStruct((b,) + cst, F32)],
        scratch_shapes=[
            pltpu.VMEM((PAIRS, HEAD, LANES), F32),
            pltpu.VMEM((HALO, GDN_CONV_CH), F32),
            big, big, big, big, big, big, big, big, big,
            small, small,
        ],
        compiler_params=pltpu.CompilerParams(
            dimension_semantics=("arbitrary", "arbitrary"), vmem_limit_bytes=VMEM_LIMIT),
        name="gdn",
    )(x, g, w, cw, alog, dtb, nrm, s0, c0)


def _rwkv_body(x_ref, g_ref, w_ref, mu_ref, w0_ref, w2_ref, a0_ref, a2_ref, g2_ref,
               kkw_ref, ka_ref, rk_ref, gn_ref, s0_ref, sh0_ref,
               o_ref, sn_ref, shn_ref,
               s_scr, zc_scr, r_scr, k2_scr, v_scr, kk_scr, kka_scr, lw_scr, cum_scr, o_scr,
               qp_scr, m_scr, n_scr, wt_scr, *, tb):
    j = pl.program_id(1)
    tbp = max(tb, CORE)
    nc = tbp // CORE
    grp = _chunk_groups(nc)

    @pl.when(j == 0)
    def _():
        _load_state_sbs(s0_ref, s_scr)
        zc_scr[...] = jnp.zeros_like(zc_scr)
        zc_scr[pl.ds(HALO - 1, 1), :] = sh0_ref[...]

    h = _rms(x_ref[...], g_ref[...]).astype(BF16)
    z = jnp.dot(h, w_ref[...], preferred_element_type=F32)
    zp = _shift_rows(z, zc_scr[...], 1)
    zc_scr[...] = z[tb - HALO:, :]

    @pl.when(j == pl.num_programs(1) - 1)
    def _():
        shn_ref[...] = zc_scr[pl.ds(HALO - 1, 1), :]

    cs = z + (zp - z) * mu_ref[...]
    r = cs[:, :RWKV_W]
    k = cs[:, RWKV_W:2 * RWKV_W]
    v = cs[:, 2 * RWKV_W:3 * RWKV_W]
    zwa = cs[:, 3 * RWKV_W:3 * RWKV_W + LANES]
    zg = cs[:, 3 * RWKV_W + LANES:]
    w_raw = -_softplus(-(w0_ref[...] + _dot1(jnp.tanh(zwa), w2_ref[...]))) - 0.5
    lw = _pad_rows(-jnp.exp(w_raw), tbp)
    a = _sigmoid(a0_ref[...] + _dot1(zwa, a2_ref[...]))
    g = _dot1(_sigmoid(zg), g2_ref[...])
    gones = _group_ones(RWKV_W, RWKV_N)
    kkn = k * kkw_ref[...]
    kkn = kkn * lax.rsqrt(_dot1(kkn * kkn, gones) + 1e-6)
    k2 = k * (1.0 + (a - 1.0) * ka_ref[...])
    bonus = _dot1(r * k2 * rk_ref[...], gones) * v

    r_scr[...] = _pad_rows(r, tbp)
    k2_scr[...] = _pad_rows(k2, tbp)
    v_scr[...] = _pad_rows(v, tbp)
    kk_scr[...] = _pad_rows(kkn, tbp)
    kka_scr[...] = _pad_rows(kkn * a, tbp)
    lw_scr[...] = lw
    cum_scr[...] = _dot_exact_lhs(_chunk_tril(tbp, CORE), lw)

    left, incl, strict = _core_masks()
    psl = [slice(p * LANES, (p + 1) * LANES) for p in range(PAIRS)]

    def phase1(i, carry):
        probs = []
        for gi in range(grp):
            c = i * grp + gi
            rows = pl.ds(pl.multiple_of(c * CORE, CORE), CORE)
            cum = cum_scr[rows, :]
            tot = cum[CORE - 1:CORE, :]
            wt_scr[pl.ds(c, 1), :] = jnp.exp(tot)
            w_inv = jnp.exp(-cum)
            w_rem = jnp.exp(tot - cum)
            kkac = kka_scr[rows, :]
            k2c = k2_scr[rows, :]
            rt = r_scr[rows, :] * jnp.exp(cum)
            probs.append(dict(rows=rows, rt=rt, rtb=_bf(rt),
                              at=_bf(-kk_scr[rows, :] * jnp.exp(cum - lw_scr[rows, :])),
                              bt=_bf(kkac * w_inv), kt=_bf(k2c * w_inv), bh=_bf(kkac * w_rem),
                              kh=_bf(k2c * w_rem), v=_bf(v_scr[rows, :])))
        items = [(pr, s) for pr in probs for s in psl]
        x = [_mm(jnp.concatenate([pr["at"][:, s], pr["rtb"][:, s]], axis=0),
                 jnp.concatenate([_bd(pr["bt"][:, s], left), _bd(pr["kt"][:, s], left)], axis=0), NT)
             for pr, s in items]
        neg_ab = [jnp.where(strict, -t[:CORE, :LANES], 0.0) for t in x]
        a_rb = [_bf(jnp.where(incl, t[CORE:, :LANES], 0.0)) for t in x]
        a_ak = [_bf(jnp.where(strict, t[:CORE, LANES:], 0.0)) for t in x]
        a_rk = [_bf(jnp.where(incl, t[CORE:, LANES:], 0.0)) for t in x]
        tinv = _tri_inv_sbs(neg_ab, left)
        vbd = [_bd(pr["v"][:, s], left) for pr, s in items]
        akv = [_bf(_mm(p_, q_)) for p_, q_ in zip(a_ak, vbd)]
        pu = [_bf(_mm(_bf(t), jnp.concatenate([_bd(pr["at"][:, s], left), _bd(u, left)], axis=1)))
              for t, u, (pr, s) in zip(tinv, akv, items)]
        rbp = [_mm(p_, jnp.concatenate([_bd(u[:, :LANES], left), _bd(u[:, LANES:], left)], axis=1))
               for p_, u in zip(a_rb, pu)]
        rkv = [_mm(p_, q_) for p_, q_ in zip(a_rk, vbd)]
        bp = [_mm(u, pr["bh"][:, s], TN) for u, (pr, s) in zip(pu, items)]
        vk = [_mm(pr["v"][:, s], pr["kh"][:, s], TN) for pr, s in items]
        for n, (pr, s) in enumerate(items):
            qp_scr[pr["rows"], s] = pr["rt"][:, s] + rbp[n][:, :LANES]
            o_scr[pr["rows"], s] = rbp[n][:, LANES:] + rkv[n]
            m_scr[pr["rows"], s] = _diag(bp[n][:LANES], left)
            n_scr[pr["rows"], s] = _diag(bp[n][LANES:], left) + _diag(vk[n], left)
        return carry

    lax.fori_loop(0, nc // grp, phase1, 0)

    def phase2(c, carry):
        rows = pl.ds(pl.multiple_of(c * CORE, CORE), CORE)
        mc = _bf(m_scr[rows, :])
        nn = n_scr[rows, :]
        qpc = _bf(qp_scr[rows, :])
        wtc = wt_scr[pl.ds(c, 1), :]
        s_old = [s_scr[p] for p in range(PAIRS)]
        sb = [_bf(s) for s in s_old]
        od = [_mm(qpc[:, psl[p]], _bd(sb[p], left), NT) for p in range(PAIRS)]
        sm = [_mm(sb[p], _bd(mc[:, psl[p]], left)) for p in range(PAIRS)]
        for p in range(PAIRS):
            s_scr[p] = s_old[p] * wtc[:, psl[p]] + sm[p] + nn[:, psl[p]]
        o_scr[rows, :] = o_scr[rows, :] + jnp.concatenate(od, axis=1)
        return carry

    lax.fori_loop(0, nc, phase2, 0)

    o = o_scr[pl.ds(0, tb), :]
    mean = _dot1(o, gones) * (1.0 / RWKV_N)
    dlt = o - mean
    var = _dot1(dlt * dlt, gones) * (1.0 / RWKV_N)
    y = dlt * lax.rsqrt(var + RWKV_GN_EPS) * gn_ref[...]
    o_ref[...] = ((y + bonus) * g).astype(BF16)

    @pl.when(j == pl.num_programs(1) - 1)
    def _():
        _store_state_sbs(s_scr, sn_ref)


def _rwkv(x, g, w, mu, w0, w2, a0, a2, g2, kkw, ka, rk, gn, s0, sh0, tb):
    b, l, _ = x.shape
    st = (RWKV_HEADS, RWKV_N, RWKV_N)
    sst = (1, RWKV_SHIFT_CH)
    tbp = max(tb, CORE)
    row = _const_spec((1, RWKV_W))
    big = pltpu.VMEM((tbp, RWKV_W), F32)
    return pl.pallas_call(
        functools.partial(_rwkv_body, tb=tb),
        grid=(b, l // tb),
        in_specs=[
            pl.BlockSpec((None, tb, D_MODEL), lambda i, j: (i, j, 0)),
            _const_spec((1, D_MODEL)),
            _const_spec(w.shape),
            _const_spec((1, RWKV_SHIFT_CH)),
            row, _const_spec((LANES, RWKV_W)), row, _const_spec((LANES, RWKV_W)),
            _const_spec((LANES, RWKV_W)),
            row, row, row, row,
            pl.BlockSpec((None,) + st, lambda i, j: (i, 0, 0, 0)),
            pl.BlockSpec((None,) + sst, lambda i, j: (i, 0, 0)),
        ],
        out_specs=[
            pl.BlockSpec((None, tb, RWKV_W), lambda i, j: (i, j, 0)),
            pl.BlockSpec((None,) + st, lambda i, j: (i, 0, 0, 0)),
            pl.BlockSpec((None,) + sst, lambda i, j: (i, 0, 0)),
        ],
        out_shape=[jax.ShapeDtypeStruct((b, l, RWKV_W), BF16),
                   jax.ShapeDtypeStruct((b,) + st, F32),
                   jax.ShapeDtypeStruct((b,) + sst, F32)],
        scratch_shapes=[
            pltpu.VMEM((PAIRS, HEAD, LANES), F32),
            pltpu.VMEM((HALO, RWKV_SHIFT_CH), F32),
            big, big, big, big, big, big, big, big, big, big, big,
            pltpu.VMEM((max(tbp // CORE, SUBLANES), RWKV_W), F32),
        ],
        compiler_params=pltpu.CompilerParams(
            dimension_semantics=("arbitrary", "arbitrary"), vmem_limit_bytes=VMEM_LIMIT),
        name="rwkv7",
    )(x, g, w, mu, w0, w2, a0, a2, g2, kkw, ka, rk, gn, s0, sh0)


def _merge_body(x_ref, g_ref, wg_ref, oa_ref, ob_ref, oc_ref, wb_ref, wo_ref, o_ref):
    x = x_ref[...]
    h = _rms(x, g_ref[...]).astype(BF16)
    gate = jnp.dot(h, wg_ref[...], preferred_element_type=F32)
    m = None
    for i, br in enumerate((oa_ref, ob_ref, oc_ref)):
        t = _sigmoid(gate[:, i * D_MODEL:(i + 1) * D_MODEL]) * jnp.dot(
            br[...], wb_ref[i], preferred_element_type=F32)
        m = t if m is None else m + t
    o_ref[...] = x + jnp.dot(m.astype(BF16), wo_ref[...], preferred_element_type=F32)


def _merge(x2d, g, wg, oa, ob, oc, wb, wo):
    m = x2d.shape[0]
    tm = min(TOKEN_TILE, m)
    bw = oa.shape[1]
    tok = lambda w: pl.BlockSpec((tm, w), lambda i: (i, 0))
    return pl.pallas_call(
        _merge_body,
        grid=(m // tm,),
        in_specs=[tok(D_MODEL), _const_spec((1, D_MODEL)), _const_spec(wg.shape),
                  tok(bw), tok(bw), tok(bw), _const_spec(wb.shape), _const_spec(wo.shape)],
        out_specs=tok(D_MODEL),
        out_shape=jax.ShapeDtypeStruct((m, D_MODEL), F32),
        compiler_params=pltpu.CompilerParams(
            dimension_semantics=("arbitrary",), vmem_limit_bytes=VMEM_LIMIT),
        name="merge",
    )(x2d, g, wg, oa, ob, oc, wb, wo)


def _rope_tables(pos0, l):
    half = RET_DK // 2
    inv = ROPE_BASE ** (-jnp.arange(half, dtype=F32) / half)
    ang = (pos0 + jnp.arange(l)).astype(F32)[:, None] * inv[None, :]
    cos = jnp.cos(ang)
    sin = jnp.sin(ang)
    cos = jnp.tile(jnp.concatenate([cos, cos], axis=1), (1, RET_HEADS))
    sin = jnp.tile(jnp.concatenate([-sin, sin], axis=1), (1, RET_HEADS))
    return cos, sin


def _prep_weights(p):
    w_in = p["w_in"]
    depth = w_in.shape[0]
    o = 0
    pieces = {}
    for name, size in (("a_qkvg", 2 * RET_QK + 2 * RET_W), ("b_qkv", GDN_CONV_CH), ("b_g", GDN_V),
                       ("b_beta", GDN_HEADS), ("b_alpha", GDN_HEADS), ("c_z", RWKV_SHIFT_CH),
                       ("gate", 3 * D_MODEL)):
        pieces[name] = w_in[:, :, o:o + size]
        o += size
    pad_l = lambda t, n: jnp.pad(t, ((0, 0), (0, 0), (0, n - t.shape[2])))
    row = lambda t: t[:, None, :]
    zeros64 = jnp.zeros((depth, 64, RWKV_W), F32)
    return dict(
        norm_ffn1=row(p["norm_ffn1"]), norm_mix=row(p["norm_mix"]), norm_ffn2=row(p["norm_ffn2"]),
        ffn1_w13=p["ffn1_w13"].astype(BF16), ffn1_w2=p["ffn1_w2"].astype(BF16),
        ffn2_w13=p["ffn2_w13"].astype(BF16), ffn2_w2=p["ffn2_w2"].astype(BF16),
        w_ret=pieces["a_qkvg"].astype(BF16),
        w_gdn=jnp.concatenate([pieces["b_qkv"], pieces["b_g"], pad_l(pieces["b_beta"], LANES),
                               pad_l(pieces["b_alpha"], LANES)], axis=2).astype(BF16),
        w_rwkv=pieces["c_z"].astype(BF16),
        w_gate=pieces["gate"].astype(BF16),
        ret_gn=row(p["ret_gn"]),
        gdn_conv=p["gdn_conv"],
        gdn_a_log=row(jnp.pad(p["gdn_a_log"], ((0, 0), (0, LANES - GDN_HEADS)))),
        gdn_dt_bias=row(jnp.pad(p["gdn_dt_bias"], ((0, 0), (0, LANES - GDN_HEADS)))),
        gdn_norm=row(jnp.tile(p["gdn_norm"], (1, GDN_HEADS))),
        rwkv_mu=row(p["rwkv_mu"]),
        rwkv_w0=row(p["rwkv_w0"]),
        rwkv_w2=jnp.concatenate([p["rwkv_w2"], zeros64], axis=1),
        rwkv_a0=row(p["rwkv_a0"]),
        rwkv_a2=jnp.concatenate([zeros64, p["rwkv_a2"]], axis=1),
        rwkv_g2=p["rwkv_g2"],
        rwkv_kk=row(p["rwkv_kk"]), rwkv_ka=row(p["rwkv_ka"]), rwkv_rk=row(p["rwkv_rk"]),
        rwkv_gn=row(p["rwkv_gn"]),
        w_branch=p["w_branch"].astype(BF16), w_out=p["w_out"].astype(BF16),
    )


def _layer(x, st, lp, tables, gf, final_norm, tb, chunk):
    b, l, d = x.shape
    s_ret, s_delta, s_conv, s_wkv, s_shift = st
    cos, sin = tables
    x2 = _ffn(x.reshape(b * l, d), lp["norm_ffn1"], lp["ffn1_w13"], lp["ffn1_w2"], gf, False)
    x3 = x2.reshape(b, l, d)
    oa, n_ret = _retention(x3, lp["norm_mix"], lp["w_ret"], cos, sin, lp["ret_gn"], s_ret, tb, chunk)
    ob, n_delta, n_conv = _gdn(x3, lp["norm_mix"], lp["w_gdn"], lp["gdn_conv"], lp["gdn_a_log"],
                               lp["gdn_dt_bias"], lp["gdn_norm"], s_delta, s_conv, tb)
    oc, n_wkv, n_shift = _rwkv(x3, lp["norm_mix"], lp["w_rwkv"], lp["rwkv_mu"], lp["rwkv_w0"],
                               lp["rwkv_w2"], lp["rwkv_a0"], lp["rwkv_a2"], lp["rwkv_g2"],
                               lp["rwkv_kk"], lp["rwkv_ka"], lp["rwkv_rk"], lp["rwkv_gn"],
                               s_wkv, s_shift, tb)
    bw = oa.shape[-1]
    x4 = _merge(x2, lp["norm_mix"], lp["w_gate"], oa.reshape(b * l, bw), ob.reshape(b * l, bw),
                oc.reshape(b * l, bw), lp["w_branch"], lp["w_out"])
    x5 = _ffn(x4, lp["norm_ffn2"], lp["ffn2_w13"], lp["ffn2_w2"], gf, final_norm)
    return x5.reshape(b, l, d), (n_ret, n_delta, n_conv, n_wkv, n_shift)


def _run_stream(x, pos0, states, wts, gf, tb):
    depth = wts["w_ret"].shape[0]
    l = x.shape[1]
    chunk = min(CHUNK, l)
    tb = min(tb, l)
    tables = _rope_tables(pos0, l)
    new = [[] for _ in range(5)]
    for li in range(depth):
        lp = {k: v[li] for k, v in wts.items()}
        st = tuple(s[li] for s in states)
        x, nst = _layer(x, st, lp, tables, gf, li == depth - 1, tb, chunk)
        for jdx in range(5):
            new[jdx].append(nst[jdx])
    return x, tuple(jnp.stack(t, axis=0) for t in new)


def kernel(x_prompt, x_sample, state_ret, state_delta, state_conv, state_wkv, state_shift,
           norm_ffn1, ffn1_w13, ffn1_w2, norm_mix, w_in, ret_gn, gdn_conv, gdn_a_log, gdn_dt_bias,
           gdn_norm, rwkv_mu, rwkv_w0, rwkv_w2, rwkv_a0, rwkv_a2, rwkv_g2, rwkv_kk, rwkv_ka, rwkv_rk,
           rwkv_gn, w_branch, w_out, norm_ffn2, ffn2_w13, ffn2_w2, norm_final):
    params = dict(
        norm_ffn1=norm_ffn1, ffn1_w13=ffn1_w13, ffn1_w2=ffn1_w2, norm_mix=norm_mix, w_in=w_in,
        ret_gn=ret_gn, gdn_conv=gdn_conv, gdn_a_log=gdn_a_log, gdn_dt_bias=gdn_dt_bias,
        gdn_norm=gdn_norm, rwkv_mu=rwkv_mu, rwkv_w0=rwkv_w0, rwkv_w2=rwkv_w2, rwkv_a0=rwkv_a0,
        rwkv_a2=rwkv_a2, rwkv_g2=rwkv_g2, rwkv_kk=rwkv_kk, rwkv_ka=rwkv_ka, rwkv_rk=rwkv_rk,
        rwkv_gn=rwkv_gn, w_branch=w_branch, w_out=w_out, norm_ffn2=norm_ffn2, ffn2_w13=ffn2_w13,
        ffn2_w2=ffn2_w2)
    wts = _prep_weights(params)
    gf = norm_final[None, :]
    depth = w_in.shape[0]
    bp = x_prompt.shape[0]
    zero_states = (
        jnp.zeros((depth, bp, RET_HEADS, RET_DK, RET_DV), F32),
        jnp.zeros((depth, bp, GDN_HEADS, GDN_DK, GDN_DV), F32),
        jnp.zeros((depth, bp, CONV_W - 1, GDN_CONV_CH), F32),
        jnp.zeros((depth, bp, RWKV_HEADS, RWKV_N, RWKV_N), F32),
        jnp.zeros((depth, bp, 1, RWKV_SHIFT_CH), F32),
    )
    y_p, st_p = _run_stream(x_prompt, 0, zero_states, wts, gf, SEQ_BLOCK)
    y_s, st_s = _run_stream(x_sample, PAST_LEN, (state_ret, state_delta, state_conv, state_wkv,
                                                 state_shift), wts, gf, SEQ_BLOCK)
    return (y_p, y_s) + st_p + st_s
```

```python
import functools
import math

import jax
import jax.numpy as jnp
from jax import lax
from jax.experimental import pallas as pl
from jax.experimental.pallas import tpu as pltpu

F32 = jnp.float32
BF16 = jnp.bfloat16

D_MODEL = 1024
D_FF = 2816
EPS = 1e-6
CHUNK = 64
PAST_LEN = 2048
RET_HEADS, RET_DK, RET_DV = 4, 64, 128
ROPE_BASE = 10000.0
GDN_HEADS, GDN_DK, GDN_DV, CONV_W = 8, 64, 64, 4
RWKV_HEADS, RWKV_N = 8, 64
RWKV_GN_EPS = 64e-5
RET_QK = RET_HEADS * RET_DK
RET_W = RET_HEADS * RET_DV
GDN_QK = GDN_HEADS * GDN_DK
GDN_V = GDN_HEADS * GDN_DV
GDN_CONV_CH = 2 * GDN_QK + GDN_V
RWKV_W = RWKV_HEADS * RWKV_N
RWKV_SHIFT_CH = 3 * RWKV_W + 256
LANES = 128
SUBLANES = 8
VMEM_LIMIT = 56 * 1024 * 1024
HEAD = 64
PAIRS = 4
CORE = 64
SEQ_BLOCK = 256
TOKEN_TILE = 512
assert GDN_DK == GDN_DV == RWKV_N == HEAD and GDN_HEADS == RWKV_HEADS == 2 * PAIRS


NT = (((1,), (1,)), ((), ()))
TN = (((0,), (0,)), ((), ()))
NN = (((1,), (0,)), ((), ()))


def _bf(x):
    return x.astype(BF16)


def _mm(a, b, dims=NN):
    return lax.dot_general(a, b, dims, preferred_element_type=F32)


def _dot1(a, b, dims=NN):
    return _mm(_bf(a), _bf(b), dims)


def _split3(x):
    x0 = _bf(x)
    r1 = x - x0.astype(F32)
    x1 = _bf(r1)
    x2 = _bf(r1 - x1.astype(F32))
    return x0, x1, x2


def _dot_exact_rhs(a, b_bf16):
    a0, a1, a2 = _split3(a)
    return _mm(a0, b_bf16) + (_mm(a1, b_bf16) + _mm(a2, b_bf16))


def _dot_exact_lhs(a_bf16, b):
    b0, b1, b2 = _split3(b)
    return _mm(a_bf16, b0) + (_mm(a_bf16, b1) + _mm(a_bf16, b2))


def _rms(x, g):
    return x * lax.rsqrt(jnp.mean(x * x, axis=-1, keepdims=True) + EPS) * g


def _sigmoid(x):
    return 0.5 + 0.5 * jnp.tanh(0.5 * x)


def _silu(x):
    return x * _sigmoid(x)


def _softplus(x):
    return jnp.maximum(x, 0.0) + jnp.log(1.0 + jnp.exp(-jnp.abs(x)))


def _iota2(shape, dim):
    return lax.broadcasted_iota(jnp.int32, shape, dim)


def _group_ones(width, group):
    r = _iota2((width, width), 0) // group
    c = _iota2((width, width), 1) // group
    return jnp.where(r == c, 1.0, 0.0).astype(BF16)


def _head_expand(width):
    r = _iota2((LANES, width), 0)
    c = _iota2((LANES, width), 1) // HEAD
    return jnp.where(r == c, 1.0, 0.0).astype(BF16)


def _chunk_tril(tb, chunk):
    r = _iota2((tb, tb), 0)
    c = _iota2((tb, tb), 1)
    return jnp.where(((r // chunk) == (c // chunk)) & (c <= r), 1.0, 0.0).astype(BF16)


def _pad_rows(x, rows):
    if x.shape[0] == rows:
        return x
    return jnp.concatenate([x, jnp.zeros((rows - x.shape[0], x.shape[1]), x.dtype)], axis=0)


def _shift_rows(x, prev, i):
    rolled = pltpu.roll(x, i, 0)
    first = jnp.where(_iota2((SUBLANES, x.shape[1]), 0) < i, pltpu.roll(prev, i, 0),
                      rolled[:SUBLANES])
    return jnp.concatenate([first, rolled[SUBLANES:]], axis=0)


def _bd(y, left):
    z = jnp.zeros_like(y)
    return jnp.concatenate([jnp.where(left, y, z), jnp.where(left, z, y)], axis=0)


def _diag(r, left):
    return jnp.where(left, r[:HEAD], r[HEAD:])


def _tri_inv_sbs(a_list, left):
    r = _iota2((CORE, LANES), 0)
    c = _iota2((CORE, LANES), 1) % HEAD
    eye = jnp.where(r == c, 1.0, 0.0).astype(F32)
    base = 16
    blk = (r // base) == (c // base)
    a0f = [jnp.where(blk, a, 0.0) for a in a_list]
    a0 = [_bf(x) for x in a0f]
    bd0 = [_bd(x, left) for x in a0]
    a2 = [_bf(_mm(x, y)) for x, y in zip(a0, bd0)]
    m = [eye - x for x in a0f]
    pw = a2
    for _ in range(2):
        bdp = [_bd(x, left) for x in pw]
        both = [_mm(jnp.concatenate([x, _bf(mi)], axis=0), y) for x, mi, y in zip(pw, m, bdp)]
        pw = [_bf(t[:CORE]) for t in both]
        m = [mi + t[CORE:] for mi, t in zip(m, both)]
    m = [mi + _mm(_bf(mi), _bd(x, left)) for mi, x in zip(m, pw)]
    size = 2 * base
    while size <= CORE:
        half = size // 2
        sel = ((r // size) == (c // size)) & ((r // half) != (c // half))
        bdb = [_bd(_bf(jnp.where(sel, a, 0.0)), left) for a in a_list]
        mb = [_bf(mi) for mi in m]
        t = [_bf(_mm(x, y)) for x, y in zip(mb, bdb)]
        m = [mi - _mm(x, _bd(y, left)) for mi, x, y in zip(m, t, mb)]
        size *= 2
    return m


def _const_spec(shape):
    nd = len(shape)
    return pl.BlockSpec(shape, lambda *_: (0,) * nd, pipeline_mode=pl.Buffered(1))


def _ffn_body(x_ref, g_ref, w13_ref, w2_ref, gn_ref, *o_refs, tail):
    x = x_ref[...]
    h = _rms(x, g_ref[...]).astype(BF16)
    ab = jnp.dot(h, w13_ref[...], preferred_element_type=F32)
    act = (_silu(ab[:, :D_FF]) * ab[:, D_FF:]).astype(BF16)
    y = x + 0.5 * jnp.dot(act, w2_ref[...], preferred_element_type=F32)
    if tail == "normed":
        o_refs[0][...] = _rms(y, gn_ref[...])
    else:
        o_refs[0][...] = y
    if tail == "with_h":
        o_refs[1][...] = _rms(y, gn_ref[...]).astype(BF16)


def _ffn(x2d, g, w13, w2, gn, tail):
    m = x2d.shape[0]
    tm = min(TOKEN_TILE, m)
    tok = pl.BlockSpec((tm, D_MODEL), lambda i: (i, 0))
    out_specs = [tok]
    out_shape = [jax.ShapeDtypeStruct((m, D_MODEL), F32)]
    if tail == "with_h":
        out_specs.append(tok)
        out_shape.append(jax.ShapeDtypeStruct((m, D_MODEL), BF16))
    return pl.pallas_call(
        functools.partial(_ffn_body, tail=tail),
        grid=(m // tm,),
        in_specs=[
            tok,
            _const_spec((1, D_MODEL)),
            _const_spec((D_MODEL, 2 * D_FF)),
            _const_spec((D_FF, D_MODEL)),
            _const_spec((1, D_MODEL)),
        ],
        out_specs=out_specs,
        out_shape=out_shape,
        compiler_params=pltpu.CompilerParams(
            dimension_semantics=("arbitrary",), vmem_limit_bytes=VMEM_LIMIT),
        name="ffn",
    )(x2d, g, w13, w2, gn)


RET_PAIRS = RET_HEADS // 2
RET_LOG_G = [math.log1p(-(2.0 ** (-5.0 - hh))) for hh in range(RET_HEADS)]


def _ret_body(h_ref, w_ref, cos_ref, sin_ref, gn_ref, s0_ref,
              o_ref, sn_ref, s_scr, q_scr, k_scr, v_scr, o_scr, *, tb):
    j = pl.program_id(1)
    tbp = max(tb, CORE)
    nc = tbp // CORE
    grp = _chunk_groups(nc)
    valid = min(tb, CORE)
    zero_blk = jnp.zeros((RET_DK, RET_DV), F32)

    @pl.when(j == 0)
    def _():
        for p in range(RET_PAIRS):
            s_scr[p] = jnp.concatenate(
                [jnp.concatenate([s0_ref[2 * p], zero_blk], axis=1),
                 jnp.concatenate([zero_blk, s0_ref[2 * p + 1]], axis=1)], axis=0)

    z = jnp.dot(h_ref[...], w_ref[...], preferred_element_type=F32)
    q = z[:, :RET_QK]
    k = z[:, RET_QK:2 * RET_QK]
    cos = cos_ref[...]
    sin = sin_ref[...]
    first = (_iota2((tb, RET_QK), 1) % RET_DK) < (RET_DK // 2)

    def rot(t):
        swapped = jnp.where(first, pltpu.roll(t, RET_QK - RET_DK // 2, 1),
                            pltpu.roll(t, RET_DK // 2, 1))
        return t * cos + swapped * sin

    q_scr[...] = _pad_rows(rot(q), tbp)
    k_scr[...] = _pad_rows(rot(k) * (RET_DK ** -0.5), tbp)
    v_scr[...] = _pad_rows(z[:, 2 * RET_QK:2 * RET_QK + RET_W], tbp)

    left, incl, _ = _core_masks()
    ri = _iota2((CORE, LANES), 0)
    ci = _iota2((CORE, LANES), 1) % HEAD
    rf = ri.astype(F32)
    dfi = jnp.where(incl, (ri - ci).astype(F32), 0.0)
    row2 = _iota2((2 * RET_DK, 2 * RET_DV), 0) < RET_DK
    col2 = _iota2((2 * RET_DK, 2 * RET_DV), 1) < RET_DV
    blockmask = row2 == col2
    vtop = _iota2((CORE, 2 * RET_DV), 1) < RET_DV
    dmask, qdec, kdec, cdec = [], [], [], []
    for p in range(RET_PAIRS):
        lg = jnp.where(left, RET_LOG_G[2 * p], RET_LOG_G[2 * p + 1])
        dmask.append(jnp.where(incl, jnp.exp(dfi * lg), 0.0))
        qdec.append(jnp.exp((rf + 1.0) * lg))
        kdec.append(jnp.exp((valid - 1.0 - rf) * lg))
        cdec.append(jnp.where(row2, math.exp(valid * RET_LOG_G[2 * p]),
                              math.exp(valid * RET_LOG_G[2 * p + 1])))
    psl = [slice(p * LANES, (p + 1) * LANES) for p in range(RET_PAIRS)]
    vsl = [slice(p * 2 * RET_DV, (p + 1) * 2 * RET_DV) for p in range(RET_PAIRS)]

    def body(i, carry):
        chunks = []
        for gi in range(grp):
            rows = pl.ds(pl.multiple_of((i * grp + gi) * CORE, CORE), CORE)
            chunks.append((rows, q_scr[rows, :], k_scr[rows, :], v_scr[rows, :]))
        items = [(ch, p) for ch in chunks for p in range(RET_PAIRS)]
        scores = [_mm(_bf(qc[:, psl[p]]), _bd(_bf(kc[:, psl[p]]), left), NT) * dmask[p]
                  for (_, qc, kc, _), p in items]
        kv = [_mm(_bf(kc[:, psl[p]] * kdec[p]), _bf(vc[:, vsl[p]]), TN)
              for (_, _, kc, vc), p in items]
        state = [s_scr[p] for p in range(RET_PAIRS)]
        for n, ((rows, qc, _, vc), p) in enumerate(items):
            vp = _bf(vc[:, vsl[p]])
            zv = jnp.zeros_like(vp)
            rhs = jnp.concatenate([jnp.where(vtop, vp, zv), jnp.where(vtop, zv, vp), _bf(state[p])],
                                  axis=0)
            lhs = _bf(jnp.concatenate([scores[n], qc[:, psl[p]] * qdec[p]], axis=1))
            o_scr[rows, vsl[p]] = _mm(lhs, rhs)
            state[p] = cdec[p] * state[p] + jnp.where(blockmask, kv[n], 0.0)
        for p in range(RET_PAIRS):
            s_scr[p] = state[p]
        return carry

    lax.fori_loop(0, nc // grp, body, 0)

    gate = z[:, 2 * RET_QK + RET_W:]
    gn = gn_ref[...]
    for hh in range(RET_HEADS):
        sl = slice(hh * RET_DV, (hh + 1) * RET_DV)
        oh = o_scr[pl.ds(0, tb), sl]
        mu = jnp.mean(oh, axis=-1, keepdims=True)
        dlt = oh - mu
        var = jnp.mean(dlt * dlt, axis=-1, keepdims=True)
        y = dlt * lax.rsqrt(var + 1e-5) * gn[:, sl]
        o_ref[:, sl] = (y * _silu(gate[:, sl])).astype(BF16)

    @pl.when(j == pl.num_programs(1) - 1)
    def _():
        for p in range(RET_PAIRS):
            s = s_scr[p]
            sn_ref[2 * p] = s[:RET_DK, :RET_DV]
            sn_ref[2 * p + 1] = s[RET_DK:, RET_DV:]


def _retention(h, w, cos, sin, gn, s0, tb):
    b, l, _ = h.shape
    st = (RET_HEADS, RET_DK, RET_DV)
    tbp = max(tb, CORE)
    return pl.pallas_call(
        functools.partial(_ret_body, tb=tb),
        grid=(b, l // tb),
        in_specs=[
            pl.BlockSpec((None, tb, D_MODEL), lambda i, j: (i, j, 0)),
            _const_spec(w.shape),
            pl.BlockSpec((tb, RET_QK), lambda i, j: (j, 0)),
            pl.BlockSpec((tb, RET_QK), lambda i, j: (j, 0)),
            _const_spec((1, RET_W)),
            pl.BlockSpec((None,) + st, lambda i, j: (i, 0, 0, 0)),
        ],
        out_specs=[
            pl.BlockSpec((None, tb, RET_W), lambda i, j: (i, j, 0)),
            pl.BlockSpec((None,) + st, lambda i, j: (i, 0, 0, 0)),
        ],
        out_shape=[jax.ShapeDtypeStruct((b, l, RET_W), BF16),
                   jax.ShapeDtypeStruct((b,) + st, F32)],
        scratch_shapes=[
            pltpu.VMEM((RET_PAIRS, 2 * RET_DK, 2 * RET_DV), F32),
            pltpu.VMEM((tbp, RET_QK), F32),
            pltpu.VMEM((tbp, RET_QK), F32),
            pltpu.VMEM((tbp, RET_W), F32),
            pltpu.VMEM((tbp, RET_W), F32),
        ],
        compiler_params=pltpu.CompilerParams(
            dimension_semantics=("arbitrary", "arbitrary"), vmem_limit_bytes=VMEM_LIMIT),
        name="retention",
    )(h, w, cos, sin, gn, s0)


HALO = SUBLANES


def _core_masks():
    r = _iota2((CORE, LANES), 0)
    c = _iota2((CORE, LANES), 1) % HEAD
    left = _iota2((CORE, LANES), 1) < HEAD
    return left, r >= c, r > c


def _load_state_sbs(s0_ref, s_scr):
    for p in range(PAIRS):
        s_scr[p] = jnp.concatenate([s0_ref[2 * p], s0_ref[2 * p + 1]], axis=1)


def _store_state_sbs(s_scr, sn_ref):
    for p in range(PAIRS):
        s = s_scr[p]
        sn_ref[2 * p] = s[:, :HEAD]
        sn_ref[2 * p + 1] = s[:, HEAD:]


def _chunk_groups(nc):
    return 2 if nc % 2 == 0 else 1


GDN_COLS = GDN_CONV_CH + GDN_V + 2 * LANES


def _gdn_body(h_ref, w_ref, cw_ref, alog_ref, dtb_ref, nrm_ref, s0_ref, c0_ref,
              o_ref, sn_ref, cn_ref,
              s_scr, xp_scr, q_scr, k_scr, v_scr, be_scr, ga_scr, o_scr, m_scr, n_scr, qp_scr,
              gar_scr, egl_scr, *, tb):
    j = pl.program_id(1)
    tbp = max(tb, CORE)
    nc = tbp // CORE
    grp = _chunk_groups(nc)

    @pl.when(j == 0)
    def _():
        _load_state_sbs(s0_ref, s_scr)
        xp_scr[...] = jnp.zeros_like(xp_scr)
        xp_scr[pl.ds(HALO - (CONV_W - 1), CONV_W - 1), :] = c0_ref[...]

    z = jnp.dot(h_ref[...], w_ref[...], preferred_element_type=F32)
    zq = z[:, :GDN_CONV_CH]
    prev = xp_scr[...]
    cw = cw_ref[...]
    y = zq * cw[CONV_W - 1:CONV_W, :]
    for i in range(1, CONV_W):
        y = y + _shift_rows(zq, prev, i) * cw[CONV_W - 1 - i:CONV_W - i, :]
    xp_scr[...] = zq[tb - HALO:, :]

    @pl.when(j == pl.num_programs(1) - 1)
    def _():
        cn_ref[...] = xp_scr[pl.ds(HALO - (CONV_W - 1), CONV_W - 1), :]

    qkv = _silu(y)
    gones = _group_ones(GDN_QK, GDN_DK)
    q = qkv[:, :GDN_QK]
    k = qkv[:, GDN_QK:2 * GDN_QK]
    q = _pad_rows(q * lax.rsqrt(_dot1(q * q, gones) + 1e-6) * (GDN_DK ** -0.5), tbp)
    k = _pad_rows(k * lax.rsqrt(_dot1(k * k, gones) + 1e-6), tbp)
    v = _pad_rows(qkv[:, 2 * GDN_QK:], tbp)

    zb = z[:, GDN_CONV_CH + GDN_V:GDN_CONV_CH + GDN_V + LANES]
    za = z[:, GDN_CONV_CH + GDN_V + LANES:]
    beta = _pad_rows(_sigmoid(zb), tbp)
    log_a = _pad_rows(-jnp.exp(alog_ref[...]) * _softplus(za + dtb_ref[...]), tbp)
    ga = _dot_exact_lhs(_chunk_tril(tbp, CORE), log_a)
    expand = _head_expand(GDN_V)
    gae = _dot_exact_rhs(ga, expand)
    q_scr[...] = q
    k_scr[...] = k
    v_scr[...] = v
    be_scr[...] = _dot_exact_rhs(beta, expand)
    ga_scr[...] = gae
    rr = _iota2((tbp, GDN_V), 0)
    cc = _iota2((tbp, GDN_V), 1)
    pick = jnp.where((rr % CORE) == (cc % HEAD), gae, 0.0)
    ncp = gar_scr.shape[0]
    chunk_of = jnp.where(_iota2((ncp, tbp), 0) == _iota2((ncp, tbp), 1) // CORE, 1.0, 0.0)
    gar_scr[...] = _dot_exact_lhs(_bf(chunk_of), pick)

    left, incl, strict = _core_masks()
    incl4 = jnp.concatenate([incl] * PAIRS, axis=1)
    psl = [slice(p * LANES, (p + 1) * LANES) for p in range(PAIRS)]

    def phase1(i, carry):
        probs = []
        for gi in range(grp):
            c = i * grp + gi
            rows = pl.ds(pl.multiple_of(c * CORE, CORE), CORE)
            gac = ga_scr[rows, :]
            glast = gac[CORE - 1:CORE, :]
            egl_scr[pl.ds(c, 1), :] = jnp.exp(glast)
            eg = jnp.exp(gac)
            gam = jnp.where(incl4, jnp.exp(jnp.where(incl4, gac - gar_scr[pl.ds(c, 1), :], 0.0)), 0.0)
            qc = q_scr[rows, :]
            kc = k_scr[rows, :]
            bec = be_scr[rows, :]
            probs.append(dict(rows=rows, gam=gam, be=bec, q=_bf(qc), k=_bf(kc), qe=qc * eg,
                              kd=_bf(kc * jnp.exp(glast - gac)), bv=_bf(bec * v_scr[rows, :]),
                              bek=_bf(bec * eg * kc)))
        items = [(pr, s) for pr in probs for s in psl]
        qk = [_mm(jnp.concatenate([pr["q"][:, s], pr["k"][:, s]], axis=0), _bd(pr["k"][:, s], left), NT)
              for pr, s in items]
        pm = [_bf(x[:CORE] * pr["gam"][:, s]) for x, (pr, s) in zip(qk, items)]
        amat = [jnp.where(strict, pr["be"][:, s] * x[CORE:] * pr["gam"][:, s], 0.0)
                for x, (pr, s) in zip(qk, items)]
        tinv = _tri_inv_sbs(amat, left)
        sol = [_bf(_mm(_bf(t), jnp.concatenate([_bd(pr["bv"][:, s], left), _bd(pr["bek"][:, s], left)],
                                               axis=1)))
               for t, (pr, s) in zip(tinv, items)]
        kts = [_mm(pr["kd"][:, s], x, TN) for x, (pr, s) in zip(sol, items)]
        ps = [_mm(p_, jnp.concatenate([_bd(x[:, :LANES], left), _bd(x[:, LANES:], left)], axis=1))
              for p_, x in zip(pm, sol)]
        for n, (pr, s) in enumerate(items):
            n_scr[pr["rows"], s] = _diag(kts[n][:, :LANES], left)
            m_scr[pr["rows"], s] = _diag(kts[n][:, LANES:], left)
            o_scr[pr["rows"], s] = ps[n][:, :LANES]
            qp_scr[pr["rows"], s] = pr["qe"][:, s] - ps[n][:, LANES:]
        return carry

    lax.fori_loop(0, nc // grp, phase1, 0)

    def phase2(c, carry):
        rows = pl.ds(pl.multiple_of(c * CORE, CORE), CORE)
        mc = _bf(m_scr[rows, :])
        nn = n_scr[rows, :]
        qpc = _bf(qp_scr[rows, :])
        eglc = egl_scr[pl.ds(c, 1), :]
        s_old = [s_scr[p] for p in range(PAIRS)]
        res = [_mm(jnp.concatenate([mc[:, psl[p]], qpc[:, psl[p]]], axis=0), _bd(_bf(s_old[p]), left))
               for p in range(PAIRS)]
        for p in range(PAIRS):
            s_scr[p] = eglc[:, psl[p]] * s_old[p] - res[p][:CORE] + nn[:, psl[p]]
        o_scr[rows, :] = o_scr[rows, :] + jnp.concatenate([x[CORE:] for x in res], axis=1)
        return carry

    lax.fori_loop(0, nc, phase2, 0)

    o = o_scr[pl.ds(0, tb), :]
    ms = _dot1(o * o, _group_ones(GDN_V, GDN_DV)) * (1.0 / GDN_DV)
    gate = z[:, GDN_CONV_CH:GDN_CONV_CH + GDN_V]
    o_ref[...] = (o * lax.rsqrt(ms + EPS) * nrm_ref[...] * _silu(gate)).astype(BF16)

    @pl.when(j == pl.num_programs(1) - 1)
    def _():
        _store_state_sbs(s_scr, sn_ref)


def _gdn(h, w, cw, alog, dtb, nrm, s0, c0, tb):
    b, l, _ = h.shape
    st = (GDN_HEADS, GDN_DK, GDN_DV)
    cst = (CONV_W - 1, GDN_CONV_CH)
    tbp = max(tb, CORE)
    big = pltpu.VMEM((tbp, GDN_V), F32)
    small = pltpu.VMEM((max(tbp // CORE, SUBLANES), GDN_V), F32)
    return pl.pallas_call(
        functools.partial(_gdn_body, tb=tb),
        grid=(b, l // tb),
        in_specs=[
            pl.BlockSpec((None, tb, D_MODEL), lambda i, j: (i, j, 0)),
            _const_spec(w.shape),
            _const_spec(cw.shape),
            _const_spec((1, LANES)),
            _const_spec((1, LANES)),
            _const_spec((1, GDN_V)),
            pl.BlockSpec((None,) + st, lambda i, j: (i, 0, 0, 0)),
            pl.BlockSpec((None,) + cst, lambda i, j: (i, 0, 0)),
        ],
        out_specs=[
            pl.BlockSpec((None, tb, GDN_V), lambda i, j: (i, j, 0)),
            pl.BlockSpec((None,) + st, lambda i, j: (i, 0, 0, 0)),
            pl.BlockSpec((None,) + cst, lambda i, j: (i, 0, 0)),
        ],
        out_shape=[jax.ShapeDtypeStruct((b, l, GDN_V), BF16),
                   jax.ShapeDtypeStruct((b,) + st, F32),
                   jax.ShapeDtypeStruct((b,) + cst, F32)],
        scratch_shapes=[
            pltpu.VMEM((PAIRS, HEAD, LANES), F32),
            pltpu.VMEM((HALO, GDN_CONV_CH), F32),
            big, big, big, big, big, big, big, big, big,
            small, small,
        ],
        compiler_params=pltpu.CompilerParams(
            dimension_semantics=("arbitrary", "arbitrary"), vmem_limit_bytes=VMEM_LIMIT),
        name="gdn",
    )(h, w, cw, alog, dtb, nrm, s0, c0)


def _rwkv_body(h_ref, w_ref, mu_ref, w0_ref, w2_ref, a0_ref, a2_ref, g2_ref,
               kkw_ref, ka_ref, rk_ref, gn_ref, s0_ref, sh0_ref,
               o_ref, sn_ref, shn_ref,
               s_scr, zc_scr, r_scr, k2_scr, v_scr, kk_scr, kka_scr, lw_scr, cum_scr, o_scr,
               qp_scr, m_scr, n_scr, wt_scr, *, tb):
    j = pl.program_id(1)
    tbp = max(tb, CORE)
    nc = tbp // CORE
    grp = _chunk_groups(nc)

    @pl.when(j == 0)
    def _():
        _load_state_sbs(s0_ref, s_scr)
        zc_scr[...] = jnp.zeros_like(zc_scr)
        zc_scr[pl.ds(HALO - 1, 1), :] = sh0_ref[...]

    z = jnp.dot(h_ref[...], w_ref[...], preferred_element_type=F32)
    zp = _shift_rows(z, zc_scr[...], 1)
    zc_scr[...] = z[tb - HALO:, :]

    @pl.when(j == pl.num_programs(1) - 1)
    def _():
        shn_ref[...] = zc_scr[pl.ds(HALO - 1, 1), :]

    cs = z + (zp - z) * mu_ref[...]
    r = cs[:, :RWKV_W]
    k = cs[:, RWKV_W:2 * RWKV_W]
    v = cs[:, 2 * RWKV_W:3 * RWKV_W]
    zwa = cs[:, 3 * RWKV_W:3 * RWKV_W + LANES]
    zg = cs[:, 3 * RWKV_W + LANES:]
    w_raw = -_softplus(-(w0_ref[...] + _dot1(jnp.tanh(zwa), w2_ref[...]))) - 0.5
    lw = _pad_rows(-jnp.exp(w_raw), tbp)
    a = _sigmoid(a0_ref[...] + _dot1(zwa, a2_ref[...]))
    g = _dot1(_sigmoid(zg), g2_ref[...])
    gones = _group_ones(RWKV_W, RWKV_N)
    kkn = k * kkw_ref[...]
    kkn = kkn * lax.rsqrt(_dot1(kkn * kkn, gones) + 1e-6)
    k2 = k * (1.0 + (a - 1.0) * ka_ref[...])
    bonus = _dot1(r * k2 * rk_ref[...], gones) * v

    r_scr[...] = _pad_rows(r, tbp)
    k2_scr[...] = _pad_rows(k2, tbp)
    v_scr[...] = _pad_rows(v, tbp)
    kk_scr[...] = _pad_rows(kkn, tbp)
    kka_scr[...] = _pad_rows(kkn * a, tbp)
    lw_scr[...] = lw
    cum_scr[...] = _dot_exact_lhs(_chunk_tril(tbp, CORE), lw)

    left, incl, strict = _core_masks()
    psl = [slice(p * LANES, (p + 1) * LANES) for p in range(PAIRS)]

    def phase1(i, carry):
        probs = []
        for gi in range(grp):
            c = i * grp + gi
            rows = pl.ds(pl.multiple_of(c * CORE, CORE), CORE)
            cum = cum_scr[rows, :]
            tot = cum[CORE - 1:CORE, :]
            wt_scr[pl.ds(c, 1), :] = jnp.exp(tot)
            w_inv = jnp.exp(-cum)
            w_rem = jnp.exp(tot - cum)
            kkac = kka_scr[rows, :]
            k2c = k2_scr[rows, :]
            rt = r_scr[rows, :] * jnp.exp(cum)
            probs.append(dict(rows=rows, rt=rt, rtb=_bf(rt),
                              at=_bf(-kk_scr[rows, :] * jnp.exp(cum - lw_scr[rows, :])),
                              bt=_bf(kkac * w_inv), kt=_bf(k2c * w_inv), bh=_bf(kkac * w_rem),
                              kh=_bf(k2c * w_rem), v=_bf(v_scr[rows, :])))
        items = [(pr, s) for pr in probs for s in psl]
        x = [_mm(jnp.concatenate([pr["at"][:, s], pr["rtb"][:, s]], axis=0),
                 jnp.concatenate([_bd(pr["bt"][:, s], left), _bd(pr["kt"][:, s], left)], axis=0), NT)
             for pr, s in items]
        neg_ab = [jnp.where(strict, -t[:CORE, :LANES], 0.0) for t in x]
        a_rb = [_bf(jnp.where(incl, t[CORE:, :LANES], 0.0)) for t in x]
        a_ak = [_bf(jnp.where(strict, t[:CORE, LANES:], 0.0)) for t in x]
        a_rk = [_bf(jnp.where(incl, t[CORE:, LANES:], 0.0)) for t in x]
        tinv = _tri_inv_sbs(neg_ab, left)
        vbd = [_bd(pr["v"][:, s], left) for pr, s in items]
        akv = [_bf(_mm(p_, q_)) for p_, q_ in zip(a_ak, vbd)]
        pu = [_bf(_mm(_bf(t), jnp.concatenate([_bd(pr["at"][:, s], left), _bd(u, left)], axis=1)))
              for t, u, (pr, s) in zip(tinv, akv, items)]
        rbp = [_mm(p_, jnp.concatenate([_bd(u[:, :LANES], left), _bd(u[:, LANES:], left)], axis=1))
               for p_, u in zip(a_rb, pu)]
        rkv = [_mm(p_, q_) for p_, q_ in zip(a_rk, vbd)]
        bp = [_mm(u, pr["bh"][:, s], TN) for u, (pr, s) in zip(pu, items)]
        vk = [_mm(pr["v"][:, s], pr["kh"][:, s], TN) for pr, s in items]
        for n, (pr, s) in enumerate(items):
            qp_scr[pr["rows"], s] = pr["rt"][:, s] + rbp[n][:, :LANES]
            o_scr[pr["rows"], s] = rbp[n][:, LANES:] + rkv[n]
            m_scr[pr["rows"], s] = _diag(bp[n][:LANES], left)
            n_scr[pr["rows"], s] = _diag(bp[n][LANES:], left) + _diag(vk[n], left)
        return carry

    lax.fori_loop(0, nc // grp, phase1, 0)

    def phase2(c, carry):
        rows = pl.ds(pl.multiple_of(c * CORE, CORE), CORE)
        mc = _bf(m_scr[rows, :])
        nn = n_scr[rows, :]
        qpc = _bf(qp_scr[rows, :])
        wtc = wt_scr[pl.ds(c, 1), :]
        s_old = [s_scr[p] for p in range(PAIRS)]
        sb = [_bf(s) for s in s_old]
        od = [_mm(qpc[:, psl[p]], _bd(sb[p], left), NT) for p in range(PAIRS)]
        sm = [_mm(sb[p], _bd(mc[:, psl[p]], left)) for p in range(PAIRS)]
        for p in range(PAIRS):
            s_scr[p] = s_old[p] * wtc[:, psl[p]] + sm[p] + nn[:, psl[p]]
        o_scr[rows, :] = o_scr[rows, :] + jnp.concatenate(od, axis=1)
        return carry

    lax.fori_loop(0, nc, phase2, 0)

    o = o_scr[pl.ds(0, tb), :]
    mean = _dot1(o, gones) * (1.0 / RWKV_N)
    dlt = o - mean
    var = _dot1(dlt * dlt, gones) * (1.0 / RWKV_N)
    y = dlt * lax.rsqrt(var + RWKV_GN_EPS) * gn_ref[...]
    o_ref[...] = ((y + bonus) * g).astype(BF16)

    @pl.when(j == pl.num_programs(1) - 1)
    def _():
        _store_state_sbs(s_scr, sn_ref)


def _rwkv(h, w, mu, w0, w2, a0, a2, g2, kkw, ka, rk, gn, s0, sh0, tb):
    b, l, _ = h.shape
    st = (RWKV_HEADS, RWKV_N, RWKV_N)
    sst = (1, RWKV_SHIFT_CH)
    tbp = max(tb, CORE)
    row = _const_spec((1, RWKV_W))
    big = pltpu.VMEM((tbp, RWKV_W), F32)
    return pl.pallas_call(
        functools.partial(_rwkv_body, tb=tb),
        grid=(b, l // tb),
        in_specs=[
            pl.BlockSpec((None, tb, D_MODEL), lambda i, j: (i, j, 0)),
            _const_spec(w.shape),
            _const_spec((1, RWKV_SHIFT_CH)),
            row, _const_spec((LANES, RWKV_W)), row, _const_spec((LANES, RWKV_W)),
            _const_spec((LANES, RWKV_W)),
            row, row, row, row,
            pl.BlockSpec((None,) + st, lambda i, j: (i, 0, 0, 0)),
            pl.BlockSpec((None,) + sst, lambda i, j: (i, 0, 0)),
        ],
        out_specs=[
            pl.BlockSpec((None, tb, RWKV_W), lambda i, j: (i, j, 0)),
            pl.BlockSpec((None,) + st, lambda i, j: (i, 0, 0, 0)),
            pl.BlockSpec((None,) + sst, lambda i, j: (i, 0, 0)),
        ],
        out_shape=[jax.ShapeDtypeStruct((b, l, RWKV_W), BF16),
                   jax.ShapeDtypeStruct((b,) + st, F32),
                   jax.ShapeDtypeStruct((b,) + sst, F32)],
        scratch_shapes=[
            pltpu.VMEM((PAIRS, HEAD, LANES), F32),
            pltpu.VMEM((HALO, RWKV_SHIFT_CH), F32),
            big, big, big, big, big, big, big, big, big, big, big,
            pltpu.VMEM((max(tbp // CORE, SUBLANES), RWKV_W), F32),
        ],
        compiler_params=pltpu.CompilerParams(
            dimension_semantics=("arbitrary", "arbitrary"), vmem_limit_bytes=VMEM_LIMIT),
        name="rwkv7",
    )(h, w, mu, w0, w2, a0, a2, g2, kkw, ka, rk, gn, s0, sh0)


def _merge_body(x_ref, h_ref, wg_ref, oa_ref, ob_ref, oc_ref, wb_ref, wo_ref, o_ref):
    x = x_ref[...]
    gate = jnp.dot(h_ref[...], wg_ref[...], preferred_element_type=F32)
    m = None
    for i, br in enumerate((oa_ref, ob_ref, oc_ref)):
        t = _sigmoid(gate[:, i * D_MODEL:(i + 1) * D_MODEL]) * jnp.dot(
            br[...], wb_ref[i], preferred_element_type=F32)
        m = t if m is None else m + t
    o_ref[...] = x + jnp.dot(m.astype(BF16), wo_ref[...], preferred_element_type=F32)


def _merge(x2d, h2d, wg, oa, ob, oc, wb, wo):
    m = x2d.shape[0]
    tm = min(TOKEN_TILE, m)
    bw = oa.shape[1]
    tok = lambda w: pl.BlockSpec((tm, w), lambda i: (i, 0))
    return pl.pallas_call(
        _merge_body,
        grid=(m // tm,),
        in_specs=[tok(D_MODEL), tok(D_MODEL), _const_spec(wg.shape),
                  tok(bw), tok(bw), tok(bw), _const_spec(wb.shape), _const_spec(wo.shape)],
        out_specs=tok(D_MODEL),
        out_shape=jax.ShapeDtypeStruct((m, D_MODEL), F32),
        compiler_params=pltpu.CompilerParams(
            dimension_semantics=("arbitrary",), vmem_limit_bytes=VMEM_LIMIT),
        name="merge",
    )(x2d, h2d, wg, oa, ob, oc, wb, wo)


def _rope_tables(pos0, l):
    half = RET_DK // 2
    inv = ROPE_BASE ** (-jnp.arange(half, dtype=F32) / half)
    ang = (pos0 + jnp.arange(l)).astype(F32)[:, None] * inv[None, :]
    cos = jnp.cos(ang)
    sin = jnp.sin(ang)
    cos = jnp.tile(jnp.concatenate([cos, cos], axis=1), (1, RET_HEADS))
    sin = jnp.tile(jnp.concatenate([-sin, sin], axis=1), (1, RET_HEADS))
    return cos, sin


def _prep_weights(p):
    w_in = p["w_in"]
    depth = w_in.shape[0]
    o = 0
    pieces = {}
    for name, size in (("a_qkvg", 2 * RET_QK + 2 * RET_W), ("b_qkv", GDN_CONV_CH), ("b_g", GDN_V),
                       ("b_beta", GDN_HEADS), ("b_alpha", GDN_HEADS), ("c_z", RWKV_SHIFT_CH),
                       ("gate", 3 * D_MODEL)):
        pieces[name] = w_in[:, :, o:o + size]
        o += size
    pad_l = lambda t, n: jnp.pad(t, ((0, 0), (0, 0), (0, n - t.shape[2])))
    row = lambda t: t[:, None, :]
    zeros64 = jnp.zeros((depth, 64, RWKV_W), F32)
    return dict(
        norm_ffn1=row(p["norm_ffn1"]), norm_mix=row(p["norm_mix"]), norm_ffn2=row(p["norm_ffn2"]),
        ffn1_w13=p["ffn1_w13"].astype(BF16), ffn1_w2=p["ffn1_w2"].astype(BF16),
        ffn2_w13=p["ffn2_w13"].astype(BF16), ffn2_w2=p["ffn2_w2"].astype(BF16),
        w_ret=pieces["a_qkvg"].astype(BF16),
        w_gdn=jnp.concatenate([pieces["b_qkv"], pieces["b_g"], pad_l(pieces["b_beta"], LANES),
                               pad_l(pieces["b_alpha"], LANES)], axis=2).astype(BF16),
        w_rwkv=pieces["c_z"].astype(BF16),
        w_gate=pieces["gate"].astype(BF16),
        ret_gn=row(p["ret_gn"]),
        gdn_conv=p["gdn_conv"],
        gdn_a_log=row(jnp.pad(p["gdn_a_log"], ((0, 0), (0, LANES - GDN_HEADS)))),
        gdn_dt_bias=row(jnp.pad(p["gdn_dt_bias"], ((0, 0), (0, LANES - GDN_HEADS)))),
        gdn_norm=row(jnp.tile(p["gdn_norm"], (1, GDN_HEADS))),
        rwkv_mu=row(p["rwkv_mu"]),
        rwkv_w0=row(p["rwkv_w0"]),
        rwkv_w2=jnp.concatenate([p["rwkv_w2"], zeros64], axis=1),
        rwkv_a0=row(p["rwkv_a0"]),
        rwkv_a2=jnp.concatenate([zeros64, p["rwkv_a2"]], axis=1),
        rwkv_g2=p["rwkv_g2"],
        rwkv_kk=row(p["rwkv_kk"]), rwkv_ka=row(p["rwkv_ka"]), rwkv_rk=row(p["rwkv_rk"]),
        rwkv_gn=row(p["rwkv_gn"]),
        w_branch=p["w_branch"].astype(BF16), w_out=p["w_out"].astype(BF16),
    )


def _layer(x, st, lp, tables, gf, final_norm, tb):
    b, l, d = x.shape
    s_ret, s_delta, s_conv, s_wkv, s_shift = st
    cos, sin = tables
    x2, h2 = _ffn(x.reshape(b * l, d), lp["norm_ffn1"], lp["ffn1_w13"], lp["ffn1_w2"],
                  lp["norm_mix"], "with_h")
    h3 = h2.reshape(b, l, d)
    oa, n_ret = _retention(h3, lp["w_ret"], cos, sin, lp["ret_gn"], s_ret, tb)
    ob, n_delta, n_conv = _gdn(h3, lp["w_gdn"], lp["gdn_conv"], lp["gdn_a_log"],
                               lp["gdn_dt_bias"], lp["gdn_norm"], s_delta, s_conv, tb)
    oc, n_wkv, n_shift = _rwkv(h3, lp["w_rwkv"], lp["rwkv_mu"], lp["rwkv_w0"],
                               lp["rwkv_w2"], lp["rwkv_a0"], lp["rwkv_a2"], lp["rwkv_g2"],
                               lp["rwkv_kk"], lp["rwkv_ka"], lp["rwkv_rk"], lp["rwkv_gn"],
                               s_wkv, s_shift, tb)
    bw = oa.shape[-1]
    x4 = _merge(x2, h2, lp["w_gate"], oa.reshape(b * l, bw), ob.reshape(b * l, bw),
                oc.reshape(b * l, bw), lp["w_branch"], lp["w_out"])
    (x5,) = _ffn(x4, lp["norm_ffn2"], lp["ffn2_w13"], lp["ffn2_w2"], gf,
                 "normed" if final_norm else "plain")
    return x5.reshape(b, l, d), (n_ret, n_delta, n_conv, n_wkv, n_shift)


def _run_stream(x, pos0, states, wts, gf, tb):
    depth = wts["w_ret"].shape[0]
    l = x.shape[1]
    tb = min(tb, l)
    tables = _rope_tables(pos0, l)
    new = [[] for _ in range(5)]
    for li in range(depth):
        lp = {k: v[li] for k, v in wts.items()}
        st = tuple(s[li] for s in states)
        x, nst = _layer(x, st, lp, tables, gf, li == depth - 1, tb)
        for jdx in range(5):
            new[jdx].append(nst[jdx])
    return x, tuple(jnp.stack(t, axis=0) for t in new)


def kernel(x_prompt, x_sample, state_ret, state_delta, state_conv, state_wkv, state_shift,
           norm_ffn1, ffn1_w13, ffn1_w2, norm_mix, w_in, ret_gn, gdn_conv, gdn_a_log, gdn_dt_bias,
           gdn_norm, rwkv_mu, rwkv_w0, rwkv_w2, rwkv_a0, rwkv_a2, rwkv_g2, rwkv_kk, rwkv_ka, rwkv_rk,
           rwkv_gn, w_branch, w_out, norm_ffn2, ffn2_w13, ffn2_w2, norm_final):
    params = dict(
        norm_ffn1=norm_ffn1, ffn1_w13=ffn1_w13, ffn1_w2=ffn1_w2, norm_mix=norm_mix, w_in=w_in,
        ret_gn=ret_gn, gdn_conv=gdn_conv, gdn_a_log=gdn_a_log, gdn_dt_bias=gdn_dt_bias,
        gdn_norm=gdn_norm, rwkv_mu=rwkv_mu, rwkv_w0=rwkv_w0, rwkv_w2=rwkv_w2, rwkv_a0=rwkv_a0,
        rwkv_a2=rwkv_a2, rwkv_g2=rwkv_g2, rwkv_kk=rwkv_kk, rwkv_ka=rwkv_ka, rwkv_rk=rwkv_rk,
        rwkv_gn=rwkv_gn, w_branch=w_branch, w_out=w_out, norm_ffn2=norm_ffn2, ffn2_w13=ffn2_w13,
        ffn2_w2=ffn2_w2)
    wts = _prep_weights(params)
    gf = norm_final[None, :]
    depth = w_in.shape[0]
    bp = x_prompt.shape[0]
    zero_states = (
        jnp.zeros((depth, bp, RET_HEADS, RET_DK, RET_DV), F32),
        jnp.zeros((depth, bp, GDN_HEADS, GDN_DK, GDN_DV), F32),
        jnp.zeros((depth, bp, CONV_W - 1, GDN_CONV_CH), F32),
        jnp.zeros((depth, bp, RWKV_HEADS, RWKV_N, RWKV_N), F32),
        jnp.zeros((depth, bp, 1, RWKV_SHIFT_CH), F32),
    )
    y_p, st_p = _run_stream(x_prompt, 0, zero_states, wts, gf, SEQ_BLOCK)
    y_s, st_s = _run_stream(x_sample, PAST_LEN, (state_ret, state_delta, state_conv, state_wkv,
                                                 state_shift), wts, gf, SEQ_BLOCK)
    return (y_p, y_s) + st_p + st_s
```

```python
import functools
import math

import jax
import jax.numpy as jnp
from jax import lax
from jax.experimental import pallas as pl
from jax.experimental.pallas import tpu as pltpu

F32 = jnp.float32
BF16 = jnp.bfloat16

D_MODEL = 1024
D_FF = 2816
EPS = 1e-6
CHUNK = 64
PAST_LEN = 2048
RET_HEADS, RET_DK, RET_DV = 4, 64, 128
ROPE_BASE = 10000.0
GDN_HEADS, GDN_DK, GDN_DV, CONV_W = 8, 64, 64, 4
RWKV_HEADS, RWKV_N = 8, 64
RWKV_GN_EPS = 64e-5
RET_QK = RET_HEADS * RET_DK
RET_W = RET_HEADS * RET_DV
GDN_QK = GDN_HEADS * GDN_DK
GDN_V = GDN_HEADS * GDN_DV
GDN_CONV_CH = 2 * GDN_QK + GDN_V
RWKV_W = RWKV_HEADS * RWKV_N
RWKV_SHIFT_CH = 3 * RWKV_W + 256
LANES = 128
SUBLANES = 8
VMEM_LIMIT = 56 * 1024 * 1024
HEAD = 64
PAIRS = 4
CORE = 64
SEQ_BLOCK = 512
TOKEN_TILE = 512
assert GDN_DK == GDN_DV == RWKV_N == HEAD and GDN_HEADS == RWKV_HEADS == 2 * PAIRS


NT = (((1,), (1,)), ((), ()))
TN = (((0,), (0,)), ((), ()))
NN = (((1,), (0,)), ((), ()))


def _bf(x):
    return x.astype(BF16)


def _mm(a, b, dims=NN):
    return lax.dot_general(a, b, dims, preferred_element_type=F32)


def _dot1(a, b, dims=NN):
    return _mm(_bf(a), _bf(b), dims)


def _split3(x):
    x0 = _bf(x)
    r1 = x - x0.astype(F32)
    x1 = _bf(r1)
    x2 = _bf(r1 - x1.astype(F32))
    return x0, x1, x2


def _dot_exact_rhs(a, b_bf16):
    a0, a1, a2 = _split3(a)
    return _mm(a0, b_bf16) + (_mm(a1, b_bf16) + _mm(a2, b_bf16))


def _dot_exact_lhs(a_bf16, b):
    b0, b1, b2 = _split3(b)
    return _mm(a_bf16, b0) + (_mm(a_bf16, b1) + _mm(a_bf16, b2))


def _rms(x, g):
    return x * lax.rsqrt(jnp.mean(x * x, axis=-1, keepdims=True) + EPS) * g


def _sigmoid(x):
    return 0.5 + 0.5 * jnp.tanh(0.5 * x)


def _silu(x):
    return x * _sigmoid(x)


def _softplus(x):
    return jnp.maximum(x, 0.0) + jnp.log(1.0 + jnp.exp(-jnp.abs(x)))


def _iota2(shape, dim):
    return lax.broadcasted_iota(jnp.int32, shape, dim)


def _group_ones(width, group):
    r = _iota2((width, width), 0) // group
    c = _iota2((width, width), 1) // group
    return jnp.where(r == c, 1.0, 0.0).astype(BF16)


def _head_expand(width):
    r = _iota2((LANES, width), 0)
    c = _iota2((LANES, width), 1) // HEAD
    return jnp.where(r == c, 1.0, 0.0).astype(BF16)


def _chunk_tril(tb, chunk):
    r = _iota2((tb, tb), 0)
    c = _iota2((tb, tb), 1)
    return jnp.where(((r // chunk) == (c // chunk)) & (c <= r), 1.0, 0.0).astype(BF16)


def _pad_rows(x, rows):
    if x.shape[0] == rows:
        return x
    return jnp.concatenate([x, jnp.zeros((rows - x.shape[0], x.shape[1]), x.dtype)], axis=0)


def _shift_rows(x, prev, i):
    rolled = pltpu.roll(x, i, 0)
    first = jnp.where(_iota2((SUBLANES, x.shape[1]), 0) < i, pltpu.roll(prev, i, 0),
                      rolled[:SUBLANES])
    return jnp.concatenate([first, rolled[SUBLANES:]], axis=0)


def _bd(y, left):
    z = jnp.zeros_like(y)
    return jnp.concatenate([jnp.where(left, y, z), jnp.where(left, z, y)], axis=0)


def _diag(r, left):
    return jnp.where(left, r[:HEAD], r[HEAD:])


def _tri_inv_sbs(a_list, left):
    r = _iota2((CORE, LANES), 0)
    c = _iota2((CORE, LANES), 1) % HEAD
    eye = jnp.where(r == c, 1.0, 0.0).astype(F32)
    base = 16
    blk = (r // base) == (c // base)
    a0f = [jnp.where(blk, a, 0.0) for a in a_list]
    a0 = [_bf(x) for x in a0f]
    bd0 = [_bd(x, left) for x in a0]
    a2 = [_bf(_mm(x, y)) for x, y in zip(a0, bd0)]
    m = [eye - x for x in a0f]
    pw = a2
    for _ in range(2):
        bdp = [_bd(x, left) for x in pw]
        both = [_mm(jnp.concatenate([x, _bf(mi)], axis=0), y) for x, mi, y in zip(pw, m, bdp)]
        pw = [_bf(t[:CORE]) for t in both]
        m = [mi + t[CORE:] for mi, t in zip(m, both)]
    m = [mi + _mm(_bf(mi), _bd(x, left)) for mi, x in zip(m, pw)]
    size = 2 * base
    while size <= CORE:
        half = size // 2
        sel = ((r // size) == (c // size)) & ((r // half) != (c // half))
        bdb = [_bd(_bf(jnp.where(sel, a, 0.0)), left) for a in a_list]
        mb = [_bf(mi) for mi in m]
        t = [_bf(_mm(x, y)) for x, y in zip(mb, bdb)]
        m = [mi - _mm(x, _bd(y, left)) for mi, x, y in zip(m, t, mb)]
        size *= 2
    return m


def _const_spec(shape):
    nd = len(shape)
    return pl.BlockSpec(shape, lambda *_: (0,) * nd, pipeline_mode=pl.Buffered(1))


def _ffn_body(x_ref, g_ref, w13_ref, w2_ref, gn_ref, *o_refs, tail):
    x = x_ref[...]
    h = _rms(x, g_ref[...]).astype(BF16)
    ab = jnp.dot(h, w13_ref[...], preferred_element_type=F32)
    act = (_silu(ab[:, :D_FF]) * ab[:, D_FF:]).astype(BF16)
    y = x + 0.5 * jnp.dot(act, w2_ref[...], preferred_element_type=F32)
    if tail == "normed":
        o_refs[0][...] = _rms(y, gn_ref[...])
    else:
        o_refs[0][...] = y
    if tail == "with_h":
        o_refs[1][...] = _rms(y, gn_ref[...]).astype(BF16)


def _ffn(x2d, g, w13, w2, gn, tail):
    m = x2d.shape[0]
    tm = min(TOKEN_TILE, m)
    tok = pl.BlockSpec((tm, D_MODEL), lambda i: (i, 0))
    out_specs = [tok]
    out_shape = [jax.ShapeDtypeStruct((m, D_MODEL), F32)]
    if tail == "with_h":
        out_specs.append(tok)
        out_shape.append(jax.ShapeDtypeStruct((m, D_MODEL), BF16))
    return pl.pallas_call(
        functools.partial(_ffn_body, tail=tail),
        grid=(m // tm,),
        in_specs=[
            tok,
            _const_spec((1, D_MODEL)),
            _const_spec((D_MODEL, 2 * D_FF)),
            _const_spec((D_FF, D_MODEL)),
            _const_spec((1, D_MODEL)),
        ],
        out_specs=out_specs,
        out_shape=out_shape,
        compiler_params=pltpu.CompilerParams(
            dimension_semantics=("arbitrary",), vmem_limit_bytes=VMEM_LIMIT),
        name="ffn",
    )(x2d, g, w13, w2, gn)


RET_PAIRS = RET_HEADS // 2
RET_LOG_G = [math.log1p(-(2.0 ** (-5.0 - hh))) for hh in range(RET_HEADS)]


def _ret_body(h_ref, w_ref, cos_ref, sin_ref, gn_ref, s0_ref,
              o_ref, sn_ref, s_scr, q_scr, k_scr, v_scr, o_scr, *, tb):
    j = pl.program_id(1)
    tbp = max(tb, CORE)
    nc = tbp // CORE
    grp = _chunk_groups(nc)
    valid = min(tb, CORE)
    zero_blk = jnp.zeros((RET_DK, RET_DV), F32)

    @pl.when(j == 0)
    def _():
        for p in range(RET_PAIRS):
            s_scr[p] = jnp.concatenate(
                [jnp.concatenate([s0_ref[2 * p], zero_blk], axis=1),
                 jnp.concatenate([zero_blk, s0_ref[2 * p + 1]], axis=1)], axis=0)

    z = jnp.dot(h_ref[...], w_ref[...], preferred_element_type=F32)
    q = z[:, :RET_QK]
    k = z[:, RET_QK:2 * RET_QK]
    cos = cos_ref[...]
    sin = sin_ref[...]
    first = (_iota2((tb, RET_QK), 1) % RET_DK) < (RET_DK // 2)

    def rot(t):
        swapped = jnp.where(first, pltpu.roll(t, RET_QK - RET_DK // 2, 1),
                            pltpu.roll(t, RET_DK // 2, 1))
        return t * cos + swapped * sin

    q_scr[...] = _pad_rows(rot(q), tbp)
    k_scr[...] = _pad_rows(rot(k) * (RET_DK ** -0.5), tbp)
    v_scr[...] = _pad_rows(z[:, 2 * RET_QK:2 * RET_QK + RET_W], tbp)

    left, incl, _ = _core_masks()
    ri = _iota2((CORE, LANES), 0)
    ci = _iota2((CORE, LANES), 1) % HEAD
    rf = ri.astype(F32)
    dfi = jnp.where(incl, (ri - ci).astype(F32), 0.0)
    row2 = _iota2((2 * RET_DK, 2 * RET_DV), 0) < RET_DK
    col2 = _iota2((2 * RET_DK, 2 * RET_DV), 1) < RET_DV
    blockmask = row2 == col2
    vtop = _iota2((CORE, 2 * RET_DV), 1) < RET_DV
    dmask, qdec, kdec, cdec = [], [], [], []
    for p in range(RET_PAIRS):
        lg = jnp.where(left, RET_LOG_G[2 * p], RET_LOG_G[2 * p + 1])
        dmask.append(jnp.where(incl, jnp.exp(dfi * lg), 0.0))
        qdec.append(jnp.exp((rf + 1.0) * lg))
        kdec.append(jnp.exp((valid - 1.0 - rf) * lg))
        cdec.append(jnp.where(row2, math.exp(valid * RET_LOG_G[2 * p]),
                              math.exp(valid * RET_LOG_G[2 * p + 1])))
    psl = [slice(p * LANES, (p + 1) * LANES) for p in range(RET_PAIRS)]
    vsl = [slice(p * 2 * RET_DV, (p + 1) * 2 * RET_DV) for p in range(RET_PAIRS)]

    def body(i, carry):
        chunks = []
        for gi in range(grp):
            rows = pl.ds(pl.multiple_of((i * grp + gi) * CORE, CORE), CORE)
            chunks.append((rows, q_scr[rows, :], k_scr[rows, :], v_scr[rows, :]))
        items = [(ch, p) for ch in chunks for p in range(RET_PAIRS)]
        scores = [_mm(_bf(qc[:, psl[p]]), _bd(_bf(kc[:, psl[p]]), left), NT) * dmask[p]
                  for (_, qc, kc, _), p in items]
        kv = [_mm(_bf(kc[:, psl[p]] * kdec[p]), _bf(vc[:, vsl[p]]), TN)
              for (_, _, kc, vc), p in items]
        state = [s_scr[p] for p in range(RET_PAIRS)]
        for n, ((rows, qc, _, vc), p) in enumerate(items):
            vp = _bf(vc[:, vsl[p]])
            zv = jnp.zeros_like(vp)
            rhs = jnp.concatenate([jnp.where(vtop, vp, zv), jnp.where(vtop, zv, vp), _bf(state[p])],
                                  axis=0)
            lhs = _bf(jnp.concatenate([scores[n], qc[:, psl[p]] * qdec[p]], axis=1))
            o_scr[rows, vsl[p]] = _mm(lhs, rhs)
            state[p] = cdec[p] * state[p] + jnp.where(blockmask, kv[n], 0.0)
        for p in range(RET_PAIRS):
            s_scr[p] = state[p]
        return carry

    lax.fori_loop(0, nc // grp, body, 0)

    gate = z[:, 2 * RET_QK + RET_W:]
    gn = gn_ref[...]
    for hh in range(RET_HEADS):
        sl = slice(hh * RET_DV, (hh + 1) * RET_DV)
        oh = o_scr[pl.ds(0, tb), sl]
        mu = jnp.mean(oh, axis=-1, keepdims=True)
        dlt = oh - mu
        var = jnp.mean(dlt * dlt, axis=-1, keepdims=True)
        y = dlt * lax.rsqrt(var + 1e-5) * gn[:, sl]
        o_ref[:, sl] = (y * _silu(gate[:, sl])).astype(BF16)

    @pl.when(j == pl.num_programs(1) - 1)
    def _():
        for p in range(RET_PAIRS):
            s = s_scr[p]
            sn_ref[2 * p] = s[:RET_DK, :RET_DV]
            sn_ref[2 * p + 1] = s[RET_DK:, RET_DV:]


def _retention(h, w, cos, sin, gn, s0, tb):
    b, l, _ = h.shape
    st = (RET_HEADS, RET_DK, RET_DV)
    tbp = max(tb, CORE)
    return pl.pallas_call(
        functools.partial(_ret_body, tb=tb),
        grid=(b, l // tb),
        in_specs=[
            pl.BlockSpec((None, tb, D_MODEL), lambda i, j: (i, j, 0)),
            _const_spec(w.shape),
            pl.BlockSpec((tb, RET_QK), lambda i, j: (j, 0)),
            pl.BlockSpec((tb, RET_QK), lambda i, j: (j, 0)),
            _const_spec((1, RET_W)),
            pl.BlockSpec((None,) + st, lambda i, j: (i, 0, 0, 0)),
        ],
        out_specs=[
            pl.BlockSpec((None, tb, RET_W), lambda i, j: (i, j, 0)),
            pl.BlockSpec((None,) + st, lambda i, j: (i, 0, 0, 0)),
        ],
        out_shape=[jax.ShapeDtypeStruct((b, l, RET_W), BF16),
                   jax.ShapeDtypeStruct((b,) + st, F32)],
        scratch_shapes=[
            pltpu.VMEM((RET_PAIRS, 2 * RET_DK, 2 * RET_DV), F32),
            pltpu.VMEM((tbp, RET_QK), F32),
            pltpu.VMEM((tbp, RET_QK), F32),
            pltpu.VMEM((tbp, RET_W), F32),
            pltpu.VMEM((tbp, RET_W), F32),
        ],
        compiler_params=pltpu.CompilerParams(
            dimension_semantics=("arbitrary", "arbitrary"), vmem_limit_bytes=VMEM_LIMIT),
        name="retention",
    )(h, w, cos, sin, gn, s0)


HALO = SUBLANES


def _core_masks():
    r = _iota2((CORE, LANES), 0)
    c = _iota2((CORE, LANES), 1) % HEAD
    left = _iota2((CORE, LANES), 1) < HEAD
    return left, r >= c, r > c


def _load_state_sbs(s0_ref, s_scr):
    for p in range(PAIRS):
        s_scr[p] = jnp.concatenate([s0_ref[2 * p], s0_ref[2 * p + 1]], axis=1)


def _store_state_sbs(s_scr, sn_ref):
    for p in range(PAIRS):
        s = s_scr[p]
        sn_ref[2 * p] = s[:, :HEAD]
        sn_ref[2 * p + 1] = s[:, HEAD:]


def _chunk_groups(nc):
    return 8 if nc % 8 == 0 else (4 if nc % 4 == 0 else (2 if nc % 2 == 0 else 1))


GDN_COLS = GDN_CONV_CH + GDN_V + 2 * LANES


def _gdn_body(h_ref, w_ref, cw_ref, alog_ref, dtb_ref, nrm_ref, s0_ref, c0_ref,
              o_ref, sn_ref, cn_ref,
              s_scr, xp_scr, q_scr, k_scr, v_scr, be_scr, ga_scr, o_scr, m_scr, n_scr, qp_scr,
              gar_scr, egl_scr, *, tb):
    j = pl.program_id(1)
    tbp = max(tb, CORE)
    nc = tbp // CORE
    grp = _chunk_groups(nc)

    @pl.when(j == 0)
    def _():
        _load_state_sbs(s0_ref, s_scr)
        xp_scr[...] = jnp.zeros_like(xp_scr)
        xp_scr[pl.ds(HALO - (CONV_W - 1), CONV_W - 1), :] = c0_ref[...]

    z = jnp.dot(h_ref[...], w_ref[...], preferred_element_type=F32)
    zq = z[:, :GDN_CONV_CH]
    prev = xp_scr[...]
    cw = cw_ref[...]
    y = zq * cw[CONV_W - 1:CONV_W, :]
    for i in range(1, CONV_W):
        y = y + _shift_rows(zq, prev, i) * cw[CONV_W - 1 - i:CONV_W - i, :]
    xp_scr[...] = zq[tb - HALO:, :]

    @pl.when(j == pl.num_programs(1) - 1)
    def _():
        cn_ref[...] = xp_scr[pl.ds(HALO - (CONV_W - 1), CONV_W - 1), :]

    qkv = _silu(y)
    gones = _group_ones(GDN_QK, GDN_DK)
    q = qkv[:, :GDN_QK]
    k = qkv[:, GDN_QK:2 * GDN_QK]
    q = _pad_rows(q * lax.rsqrt(_dot1(q * q, gones) + 1e-6) * (GDN_DK ** -0.5), tbp)
    k = _pad_rows(k * lax.rsqrt(_dot1(k * k, gones) + 1e-6), tbp)
    v = _pad_rows(qkv[:, 2 * GDN_QK:], tbp)

    zb = z[:, GDN_CONV_CH + GDN_V:GDN_CONV_CH + GDN_V + LANES]
    za = z[:, GDN_CONV_CH + GDN_V + LANES:]
    beta = _pad_rows(_sigmoid(zb), tbp)
    log_a = _pad_rows(-jnp.exp(alog_ref[...]) * _softplus(za + dtb_ref[...]), tbp)
    ga = _dot_exact_lhs(_chunk_tril(tbp, CORE), log_a)
    expand = _head_expand(GDN_V)
    gae = _dot_exact_rhs(ga, expand)
    q_scr[...] = q
    k_scr[...] = k
    v_scr[...] = v
    be_scr[...] = _dot_exact_rhs(beta, expand)
    ga_scr[...] = gae
    rr = _iota2((tbp, GDN_V), 0)
    cc = _iota2((tbp, GDN_V), 1)
    pick = jnp.where((rr % CORE) == (cc % HEAD), gae, 0.0)
    ncp = gar_scr.shape[0]
    chunk_of = jnp.where(_iota2((ncp, tbp), 0) == _iota2((ncp, tbp), 1) // CORE, 1.0, 0.0)
    gar_scr[...] = _dot_exact_lhs(_bf(chunk_of), pick)

    left, incl, strict = _core_masks()
    incl4 = jnp.concatenate([incl] * PAIRS, axis=1)
    psl = [slice(p * LANES, (p + 1) * LANES) for p in range(PAIRS)]

    def phase1(i, carry):
        probs = []
        for gi in range(grp):
            c = i * grp + gi
            rows = pl.ds(pl.multiple_of(c * CORE, CORE), CORE)
            gac = ga_scr[rows, :]
            glast = gac[CORE - 1:CORE, :]
            egl_scr[pl.ds(c, 1), :] = jnp.exp(glast)
            eg = jnp.exp(gac)
            gam = jnp.where(incl4, jnp.exp(jnp.where(incl4, gac - gar_scr[pl.ds(c, 1), :], 0.0)), 0.0)
            qc = q_scr[rows, :]
            kc = k_scr[rows, :]
            bec = be_scr[rows, :]
            probs.append(dict(rows=rows, gam=gam, be=bec, q=_bf(qc), k=_bf(kc), qe=qc * eg,
                              kd=_bf(kc * jnp.exp(glast - gac)), bv=_bf(bec * v_scr[rows, :]),
                              bek=_bf(bec * eg * kc)))
        items = [(pr, s) for pr in probs for s in psl]
        qk = [_mm(jnp.concatenate([pr["q"][:, s], pr["k"][:, s]], axis=0), _bd(pr["k"][:, s], left), NT)
              for pr, s in items]
        pm = [_bf(x[:CORE] * pr["gam"][:, s]) for x, (pr, s) in zip(qk, items)]
        amat = [jnp.where(strict, pr["be"][:, s] * x[CORE:] * pr["gam"][:, s], 0.0)
                for x, (pr, s) in zip(qk, items)]
        tinv = _tri_inv_sbs(amat, left)
        sol = [_bf(_mm(_bf(t), jnp.concatenate([_bd(pr["bv"][:, s], left), _bd(pr["bek"][:, s], left)],
                                               axis=1)))
               for t, (pr, s) in zip(tinv, items)]
        kts = [_mm(pr["kd"][:, s], x, TN) for x, (pr, s) in zip(sol, items)]
        ps = [_mm(p_, jnp.concatenate([_bd(x[:, :LANES], left), _bd(x[:, LANES:], left)], axis=1))
              for p_, x in zip(pm, sol)]
        for n, (pr, s) in enumerate(items):
            n_scr[pr["rows"], s] = _diag(kts[n][:, :LANES], left)
            m_scr[pr["rows"], s] = _diag(kts[n][:, LANES:], left)
            o_scr[pr["rows"], s] = ps[n][:, :LANES]
            qp_scr[pr["rows"], s] = pr["qe"][:, s] - ps[n][:, LANES:]
        return carry

    lax.fori_loop(0, nc // grp, phase1, 0, unroll=True)

    def phase2(c, carry):
        rows = pl.ds(pl.multiple_of(c * CORE, CORE), CORE)
        mc = _bf(m_scr[rows, :])
        nn = n_scr[rows, :]
        qpc = _bf(qp_scr[rows, :])
        eglc = egl_scr[pl.ds(c, 1), :]
        s_old = [s_scr[p] for p in range(PAIRS)]
        res = [_mm(jnp.concatenate([mc[:, psl[p]], qpc[:, psl[p]]], axis=0), _bd(_bf(s_old[p]), left))
               for p in range(PAIRS)]
        for p in range(PAIRS):
            s_scr[p] = eglc[:, psl[p]] * s_old[p] - res[p][:CORE] + nn[:, psl[p]]
        o_scr[rows, :] = o_scr[rows, :] + jnp.concatenate([x[CORE:] for x in res], axis=1)
        return carry

    lax.fori_loop(0, nc, phase2, 0, unroll=True)

    o = o_scr[pl.ds(0, tb), :]
    ms = _dot1(o * o, _group_ones(GDN_V, GDN_DV)) * (1.0 / GDN_DV)
    gate = z[:, GDN_CONV_CH:GDN_CONV_CH + GDN_V]
    o_ref[...] = (o * lax.rsqrt(ms + EPS) * nrm_ref[...] * _silu(gate)).astype(BF16)

    @pl.when(j == pl.num_programs(1) - 1)
    def _():
        _store_state_sbs(s_scr, sn_ref)


def _gdn(h, w, cw, alog, dtb, nrm, s0, c0, tb):
    b, l, _ = h.shape
    st = (GDN_HEADS, GDN_DK, GDN_DV)
    cst = (CONV_W - 1, GDN_CONV_CH)
    tbp = max(tb, CORE)
    big = pltpu.VMEM((tbp, GDN_V), F32)
    small = pltpu.VMEM((max(tbp // CORE, SUBLANES), GDN_V), F32)
    return pl.pallas_call(
        functools.partial(_gdn_body, tb=tb),
        grid=(b, l // tb),
        in_specs=[
            pl.BlockSpec((None, tb, D_MODEL), lambda i, j: (i, j, 0)),
            _const_spec(w.shape),
            _const_spec(cw.shape),
            _const_spec((1, LANES)),
            _const_spec((1, LANES)),
            _const_spec((1, GDN_V)),
            pl.BlockSpec((None,) + st, lambda i, j: (i, 0, 0, 0)),
            pl.BlockSpec((None,) + cst, lambda i, j: (i, 0, 0)),
        ],
        out_specs=[
            pl.BlockSpec((None, tb, GDN_V), lambda i, j: (i, j, 0)),
            pl.BlockSpec((None,) + st, lambda i, j: (i, 0, 0, 0)),
            pl.BlockSpec((None,) + cst, lambda i, j: (i, 0, 0)),
        ],
        out_shape=[jax.ShapeDtypeStruct((b, l, GDN_V), BF16),
                   jax.ShapeDtypeStruct((b,) + st, F32),
                   jax.ShapeDtypeStruct((b,) + cst, F32)],
        scratch_shapes=[
            pltpu.VMEM((PAIRS, HEAD, LANES), F32),
            pltpu.VMEM((HALO, GDN_CONV_CH), F32),
            big, big, big, big, big, big, big, big, big,
            small, small,
        ],
        compiler_params=pltpu.CompilerParams(
            dimension_semantics=("arbitrary", "arbitrary"), vmem_limit_bytes=VMEM_LIMIT),
        name="gdn",
    )(h, w, cw, alog, dtb, nrm, s0, c0)


def _rwkv_body(h_ref, w_ref, mu_ref, w0_ref, w2_ref, a0_ref, a2_ref, g2_ref,
               kkw_ref, ka_ref, rk_ref, gn_ref, s0_ref, sh0_ref,
               o_ref, sn_ref, shn_ref,
               s_scr, zc_scr, r_scr, k2_scr, v_scr, kk_scr, kka_scr, lw_scr, cum_scr, o_scr,
               qp_scr, m_scr, n_scr, wt_scr, *, tb):
    j = pl.program_id(1)
    tbp = max(tb, CORE)
    nc = tbp // CORE
    grp = _chunk_groups(nc)

    @pl.when(j == 0)
    def _():
        _load_state_sbs(s0_ref, s_scr)
        zc_scr[...] = jnp.zeros_like(zc_scr)
        zc_scr[pl.ds(HALO - 1, 1), :] = sh0_ref[...]

    z = jnp.dot(h_ref[...], w_ref[...], preferred_element_type=F32)
    zp = _shift_rows(z, zc_scr[...], 1)
    zc_scr[...] = z[tb - HALO:, :]

    @pl.when(j == pl.num_programs(1) - 1)
    def _():
        shn_ref[...] = zc_scr[pl.ds(HALO - 1, 1), :]

    cs = z + (zp - z) * mu_ref[...]
    r = cs[:, :RWKV_W]
    k = cs[:, RWKV_W:2 * RWKV_W]
    v = cs[:, 2 * RWKV_W:3 * RWKV_W]
    zwa = cs[:, 3 * RWKV_W:3 * RWKV_W + LANES]
    zg = cs[:, 3 * RWKV_W + LANES:]
    w_raw = -_softplus(-(w0_ref[...] + _dot1(jnp.tanh(zwa), w2_ref[...]))) - 0.5
    lw = _pad_rows(-jnp.exp(w_raw), tbp)
    a = _sigmoid(a0_ref[...] + _dot1(zwa, a2_ref[...]))
    g = _dot1(_sigmoid(zg), g2_ref[...])
    gones = _group_ones(RWKV_W, RWKV_N)
    kkn = k * kkw_ref[...]
    kkn = kkn * lax.rsqrt(_dot1(kkn * kkn, gones) + 1e-6)
    k2 = k * (1.0 + (a - 1.0) * ka_ref[...])
    bonus = _dot1(r * k2 * rk_ref[...], gones) * v

    r_scr[...] = _pad_rows(r, tbp)
    k2_scr[...] = _pad_rows(k2, tbp)
    v_scr[...] = _pad_rows(v, tbp)
    kk_scr[...] = _pad_rows(kkn, tbp)
    kka_scr[...] = _pad_rows(kkn * a, tbp)
    lw_scr[...] = lw
    cum_scr[...] = _dot_exact_lhs(_chunk_tril(tbp, CORE), lw)

    left, incl, strict = _core_masks()
    psl = [slice(p * LANES, (p + 1) * LANES) for p in range(PAIRS)]

    def phase1(i, carry):
        probs = []
        for gi in range(grp):
            c = i * grp + gi
            rows = pl.ds(pl.multiple_of(c * CORE, CORE), CORE)
            cum = cum_scr[rows, :]
            tot = cum[CORE - 1:CORE, :]
            wt_scr[pl.ds(c, 1), :] = jnp.exp(tot)
            w_inv = jnp.exp(-cum)
            w_rem = jnp.exp(tot - cum)
            kkac = kka_scr[rows, :]
            k2c = k2_scr[rows, :]
            rt = r_scr[rows, :] * jnp.exp(cum)
            probs.append(dict(rows=rows, rt=rt, rtb=_bf(rt),
                              at=_bf(-kk_scr[rows, :] * jnp.exp(cum - lw_scr[rows, :])),
                              bt=_bf(kkac * w_inv), kt=_bf(k2c * w_inv), bh=_bf(kkac * w_rem),
                              kh=_bf(k2c * w_rem), v=_bf(v_scr[rows, :])))
        items = [(pr, s) for pr in probs for s in psl]
        x = [_mm(jnp.concatenate([pr["at"][:, s], pr["rtb"][:, s]], axis=0),
                 jnp.concatenate([_bd(pr["bt"][:, s], left), _bd(pr["kt"][:, s], left)], axis=0), NT)
             for pr, s in items]
        neg_ab = [jnp.where(strict, -t[:CORE, :LANES], 0.0) for t in x]
        a_rb = [_bf(jnp.where(incl, t[CORE:, :LANES], 0.0)) for t in x]
        a_ak = [_bf(jnp.where(strict, t[:CORE, LANES:], 0.0)) for t in x]
        a_rk = [_bf(jnp.where(incl, t[CORE:, LANES:], 0.0)) for t in x]
        tinv = _tri_inv_sbs(neg_ab, left)
        vbd = [_bd(pr["v"][:, s], left) for pr, s in items]
        akv = [_bf(_mm(p_, q_)) for p_, q_ in zip(a_ak, vbd)]
        pu = [_bf(_mm(_bf(t), jnp.concatenate([_bd(pr["at"][:, s], left), _bd(u, left)], axis=1)))
              for t, u, (pr, s) in zip(tinv, akv, items)]
        rbp = [_mm(p_, jnp.concatenate([_bd(u[:, :LANES], left), _bd(u[:, LANES:], left)], axis=1))
               for p_, u in zip(a_rb, pu)]
        rkv = [_mm(p_, q_) for p_, q_ in zip(a_rk, vbd)]
        bp = [_mm(u, pr["bh"][:, s], TN) for u, (pr, s) in zip(pu, items)]
        vk = [_mm(pr["v"][:, s], pr["kh"][:, s], TN) for pr, s in items]
        for n, (pr, s) in enumerate(items):
            qp_scr[pr["rows"], s] = pr["rt"][:, s] + rbp[n][:, :LANES]
            o_scr[pr["rows"], s] = rbp[n][:, LANES:] + rkv[n]
            m_scr[pr["rows"], s] = _diag(bp[n][:LANES], left)
            n_scr[pr["rows"], s] = _diag(bp[n][LANES:], left) + _diag(vk[n], left)
        return carry

    lax.fori_loop(0, nc // grp, phase1, 0, unroll=True)

    def phase2(c, carry):
        rows = pl.ds(pl.multiple_of(c * CORE, CORE), CORE)
        mc = _bf(m_scr[rows, :])
        nn = n_scr[rows, :]
        qpc = _bf(qp_scr[rows, :])
        wtc = wt_scr[pl.ds(c, 1), :]
        s_old = [s_scr[p] for p in range(PAIRS)]
        sb = [_bf(s) for s in s_old]
        od = [_mm(qpc[:, psl[p]], _bd(sb[p], left), NT) for p in range(PAIRS)]
        sm = [_mm(sb[p], _bd(mc[:, psl[p]], left)) for p in range(PAIRS)]
        for p in range(PAIRS):
            s_scr[p] = s_old[p] * wtc[:, psl[p]] + sm[p] + nn[:, psl[p]]
        o_scr[rows, :] = o_scr[rows, :] + jnp.concatenate(od, axis=1)
        return carry

    lax.fori_loop(0, nc, phase2, 0, unroll=True)

    o = o_scr[pl.ds(0, tb), :]
    mean = _dot1(o, gones) * (1.0 / RWKV_N)
    dlt = o - mean
    var = _dot1(dlt * dlt, gones) * (1.0 / RWKV_N)
    y = dlt * lax.rsqrt(var + RWKV_GN_EPS) * gn_ref[...]
    o_ref[...] = ((y + bonus) * g).astype(BF16)

    @pl.when(j == pl.num_programs(1) - 1)
    def _():
        _store_state_sbs(s_scr, sn_ref)


def _rwkv(h, w, mu, w0, w2, a0, a2, g2, kkw, ka, rk, gn, s0, sh0, tb):
    b, l, _ = h.shape
    st = (RWKV_HEADS, RWKV_N, RWKV_N)
    sst = (1, RWKV_SHIFT_CH)
    tbp = max(tb, CORE)
    row = _const_spec((1, RWKV_W))
    big = pltpu.VMEM((tbp, RWKV_W), F32)
    return pl.pallas_call(
        functools.partial(_rwkv_body, tb=tb),
        grid=(b, l // tb),
        in_specs=[
            pl.BlockSpec((None, tb, D_MODEL), lambda i, j: (i, j, 0)),
            _const_spec(w.shape),
            _const_spec((1, RWKV_SHIFT_CH)),
            row, _const_spec((LANES, RWKV_W)), row, _const_spec((LANES, RWKV_W)),
            _const_spec((LANES, RWKV_W)),
            row, row, row, row,
            pl.BlockSpec((None,) + st, lambda i, j: (i, 0, 0, 0)),
            pl.BlockSpec((None,) + sst, lambda i, j: (i, 0, 0)),
        ],
        out_specs=[
            pl.BlockSpec((None, tb, RWKV_W), lambda i, j: (i, j, 0)),
            pl.BlockSpec((None,) + st, lambda i, j: (i, 0, 0, 0)),
            pl.BlockSpec((None,) + sst, lambda i, j: (i, 0, 0)),
        ],
        out_shape=[jax.ShapeDtypeStruct((b, l, RWKV_W), BF16),
                   jax.ShapeDtypeStruct((b,) + st, F32),
                   jax.ShapeDtypeStruct((b,) + sst, F32)],
        scratch_shapes=[
            pltpu.VMEM((PAIRS, HEAD, LANES), F32),
            pltpu.VMEM((HALO, RWKV_SHIFT_CH), F32),
            big, big, big, big, big, big, big, big, big, big, big,
            pltpu.VMEM((max(tbp // CORE, SUBLANES), RWKV_W), F32),
        ],
        compiler_params=pltpu.CompilerParams(
            dimension_semantics=("arbitrary", "arbitrary"), vmem_limit_bytes=VMEM_LIMIT),
        name="rwkv7",
    )(h, w, mu, w0, w2, a0, a2, g2, kkw, ka, rk, gn, s0, sh0)


def _merge_body(x_ref, h_ref, wg_ref, oa_ref, ob_ref, oc_ref, wb_ref, wo_ref, o_ref):
    x = x_ref[...]
    gate = jnp.dot(h_ref[...], wg_ref[...], preferred_element_type=F32)
    m = None
    for i, br in enumerate((oa_ref, ob_ref, oc_ref)):
        t = _sigmoid(gate[:, i * D_MODEL:(i + 1) * D_MODEL]) * jnp.dot(
            br[...], wb_ref[i], preferred_element_type=F32)
        m = t if m is None else m + t
    o_ref[...] = x + jnp.dot(m.astype(BF16), wo_ref[...], preferred_element_type=F32)


def _merge(x2d, h2d, wg, oa, ob, oc, wb, wo):
    m = x2d.shape[0]
    tm = min(TOKEN_TILE, m)
    bw = oa.shape[1]
    tok = lambda w: pl.BlockSpec((tm, w), lambda i: (i, 0))
    return pl.pallas_call(
        _merge_body,
        grid=(m // tm,),
        in_specs=[tok(D_MODEL), tok(D_MODEL), _const_spec(wg.shape),
                  tok(bw), tok(bw), tok(bw), _const_spec(wb.shape), _const_spec(wo.shape)],
        out_specs=tok(D_MODEL),
        out_shape=jax.ShapeDtypeStruct((m, D_MODEL), F32),
        compiler_params=pltpu.CompilerParams(
            dimension_semantics=("arbitrary",), vmem_limit_bytes=VMEM_LIMIT),
        name="merge",
    )(x2d, h2d, wg, oa, ob, oc, wb, wo)


def _rope_tables(pos0, l):
    half = RET_DK // 2
    inv = ROPE_BASE ** (-jnp.arange(half, dtype=F32) / half)
    ang = (pos0 + jnp.arange(l)).astype(F32)[:, None] * inv[None, :]
    cos = jnp.cos(ang)
    sin = jnp.sin(ang)
    cos = jnp.tile(jnp.concatenate([cos, cos], axis=1), (1, RET_HEADS))
    sin = jnp.tile(jnp.concatenate([-sin, sin], axis=1), (1, RET_HEADS))
    return cos, sin


def _prep_weights(p):
    w_in = p["w_in"]
    depth = w_in.shape[0]
    o = 0
    pieces = {}
    for name, size in (("a_qkvg", 2 * RET_QK + 2 * RET_W), ("b_qkv", GDN_CONV_CH), ("b_g", GDN_V),
                       ("b_beta", GDN_HEADS), ("b_alpha", GDN_HEADS), ("c_z", RWKV_SHIFT_CH),
                       ("gate", 3 * D_MODEL)):
        pieces[name] = w_in[:, :, o:o + size]
        o += size
    pad_l = lambda t, n: jnp.pad(t, ((0, 0), (0, 0), (0, n - t.shape[2])))
    row = lambda t: t[:, None, :]
    zeros64 = jnp.zeros((depth, 64, RWKV_W), F32)
    return dict(
        norm_ffn1=row(p["norm_ffn1"]), norm_mix=row(p["norm_mix"]), norm_ffn2=row(p["norm_ffn2"]),
        ffn1_w13=p["ffn1_w13"].astype(BF16), ffn1_w2=p["ffn1_w2"].astype(BF16),
        ffn2_w13=p["ffn2_w13"].astype(BF16), ffn2_w2=p["ffn2_w2"].astype(BF16),
        w_ret=pieces["a_qkvg"].astype(BF16),
        w_gdn=jnp.concatenate([pieces["b_qkv"], pieces["b_g"], pad_l(pieces["b_beta"], LANES),
                               pad_l(pieces["b_alpha"], LANES)], axis=2).astype(BF16),
        w_rwkv=pieces["c_z"].astype(BF16),
        w_gate=pieces["gate"].astype(BF16),
        ret_gn=row(p["ret_gn"]),
        gdn_conv=p["gdn_conv"],
        gdn_a_log=row(jnp.pad(p["gdn_a_log"], ((0, 0), (0, LANES - GDN_HEADS)))),
        gdn_dt_bias=row(jnp.pad(p["gdn_dt_bias"], ((0, 0), (0, LANES - GDN_HEADS)))),
        gdn_norm=row(jnp.tile(p["gdn_norm"], (1, GDN_HEADS))),
        rwkv_mu=row(p["rwkv_mu"]),
        rwkv_w0=row(p["rwkv_w0"]),
        rwkv_w2=jnp.concatenate([p["rwkv_w2"], zeros64], axis=1),
        rwkv_a0=row(p["rwkv_a0"]),
        rwkv_a2=jnp.concatenate([zeros64, p["rwkv_a2"]], axis=1),
        rwkv_g2=p["rwkv_g2"],
        rwkv_kk=row(p["rwkv_kk"]), rwkv_ka=row(p["rwkv_ka"]), rwkv_rk=row(p["rwkv_rk"]),
        rwkv_gn=row(p["rwkv_gn"]),
        w_branch=p["w_branch"].astype(BF16), w_out=p["w_out"].astype(BF16),
    )


def _layer(x, st, lp, tables, gf, final_norm, tb):
    b, l, d = x.shape
    s_ret, s_delta, s_conv, s_wkv, s_shift = st
    cos, sin = tables
    x2, h2 = _ffn(x.reshape(b * l, d), lp["norm_ffn1"], lp["ffn1_w13"], lp["ffn1_w2"],
                  lp["norm_mix"], "with_h")
    h3 = h2.reshape(b, l, d)
    oa, n_ret = _retention(h3, lp["w_ret"], cos, sin, lp["ret_gn"], s_ret, tb)
    ob, n_delta, n_conv = _gdn(h3, lp["w_gdn"], lp["gdn_conv"], lp["gdn_a_log"],
                               lp["gdn_dt_bias"], lp["gdn_norm"], s_delta, s_conv, tb)
    oc, n_wkv, n_shift = _rwkv(h3, lp["w_rwkv"], lp["rwkv_mu"], lp["rwkv_w0"],
                               lp["rwkv_w2"], lp["rwkv_a0"], lp["rwkv_a2"], lp["rwkv_g2"],
                               lp["rwkv_kk"], lp["rwkv_ka"], lp["rwkv_rk"], lp["rwkv_gn"],
                               s_wkv, s_shift, tb)
    bw = oa.shape[-1]
    x4 = _merge(x2, h2, lp["w_gate"], oa.reshape(b * l, bw), ob.reshape(b * l, bw),
                oc.reshape(b * l, bw), lp["w_branch"], lp["w_out"])
    (x5,) = _ffn(x4, lp["norm_ffn2"], lp["ffn2_w13"], lp["ffn2_w2"], gf,
                 "normed" if final_norm else "plain")
    return x5.reshape(b, l, d), (n_ret, n_delta, n_conv, n_wkv, n_shift)


def _run_stream(x, pos0, states, wts, gf, tb):
    depth = wts["w_ret"].shape[0]
    l = x.shape[1]
    tb = min(tb, l)
    tables = _rope_tables(pos0, l)
    new = [[] for _ in range(5)]
    for li in range(depth):
        lp = {k: v[li] for k, v in wts.items()}
        st = tuple(s[li] for s in states)
        x, nst = _layer(x, st, lp, tables, gf, li == depth - 1, tb)
        for jdx in range(5):
            new[jdx].append(nst[jdx])
    return x, tuple(jnp.stack(t, axis=0) for t in new)


def kernel(x_prompt, x_sample, state_ret, state_delta, state_conv, state_wkv, state_shift,
           norm_ffn1, ffn1_w13, ffn1_w2, norm_mix, w_in, ret_gn, gdn_conv, gdn_a_log, gdn_dt_bias,
           gdn_norm, rwkv_mu, rwkv_w0, rwkv_w2, rwkv_a0, rwkv_a2, rwkv_g2, rwkv_kk, rwkv_ka, rwkv_rk,
           rwkv_gn, w_branch, w_out, norm_ffn2, ffn2_w13, ffn2_w2, norm_final):
    params = dict(
        norm_ffn1=norm_ffn1, ffn1_w13=ffn1_w13, ffn1_w2=ffn1_w2, norm_mix=norm_mix, w_in=w_in,
        ret_gn=ret_gn, gdn_conv=gdn_conv, gdn_a_log=gdn_a_log, gdn_dt_bias=gdn_dt_bias,
        gdn_norm=gdn_norm, rwkv_mu=rwkv_mu, rwkv_w0=rwkv_w0, rwkv_w2=rwkv_w2, rwkv_a0=rwkv_a0,
        rwkv_a2=rwkv_a2, rwkv_g2=rwkv_g2, rwkv_kk=rwkv_kk, rwkv_ka=rwkv_ka, rwkv_rk=rwkv_rk,
        rwkv_gn=rwkv_gn, w_branch=w_branch, w_out=w_out, norm_ffn2=norm_ffn2, ffn2_w13=ffn2_w13,
        ffn2_w2=ffn2_w2)
    wts = _prep_weights(params)
    gf = norm_final[None, :]
    depth = w_in.shape[0]
    bp = x_prompt.shape[0]
    zero_states = (
        jnp.zeros((depth, bp, RET_HEADS, RET_DK, RET_DV), F32),
        jnp.zeros((depth, bp, GDN_HEADS, GDN_DK, GDN_DV), F32),
        jnp.zeros((depth, bp, CONV_W - 1, GDN_CONV_CH), F32),
        jnp.zeros((depth, bp, RWKV_HEADS, RWKV_N, RWKV_N), F32),
        jnp.zeros((depth, bp, 1, RWKV_SHIFT_CH), F32),
    )
    y_p, st_p = _run_stream(x_prompt, 0, zero_states, wts, gf, SEQ_BLOCK)
    y_s, st_s = _run_stream(x_sample, PAST_LEN, (state_ret, state_delta, state_conv, state_wkv,
                                                 state_shift), wts, gf, SEQ_BLOCK)
    return (y_p, y_s) + st_p + st_s
```

```python
import functools
import math

import jax
import jax.numpy as jnp
from jax import lax
from jax.experimental import pallas as pl
from jax.experimental.pallas import tpu as pltpu

F32 = jnp.float32
BF16 = jnp.bfloat16

D_MODEL = 1024
D_FF = 2816
EPS = 1e-6
CHUNK = 64
PAST_LEN = 2048
RET_HEADS, RET_DK, RET_DV = 4, 64, 128
ROPE_BASE = 10000.0
GDN_HEADS, GDN_DK, GDN_DV, CONV_W = 8, 64, 64, 4
RWKV_HEADS, RWKV_N = 8, 64
RWKV_GN_EPS = 64e-5
RET_QK = RET_HEADS * RET_DK
RET_W = RET_HEADS * RET_DV
GDN_QK = GDN_HEADS * GDN_DK
GDN_V = GDN_HEADS * GDN_DV
GDN_CONV_CH = 2 * GDN_QK + GDN_V
RWKV_W = RWKV_HEADS * RWKV_N
RWKV_SHIFT_CH = 3 * RWKV_W + 256
LANES = 128
SUBLANES = 8
VMEM_LIMIT = 56 * 1024 * 1024
HEAD = 64
PAIRS = 4
CORE = 64
SEQ_BLOCK = 512
TOKEN_TILE = 512
assert GDN_DK == GDN_DV == RWKV_N == HEAD and GDN_HEADS == RWKV_HEADS == 2 * PAIRS


NT = (((1,), (1,)), ((), ()))
TN = (((0,), (0,)), ((), ()))
NN = (((1,), (0,)), ((), ()))


def _bf(x):
    return x.astype(BF16)


def _mm(a, b, dims=NN):
    return lax.dot_general(a, b, dims, preferred_element_type=F32)


def _dot1(a, b, dims=NN):
    return _mm(_bf(a), _bf(b), dims)


def _split3(x):
    x0 = _bf(x)
    r1 = x - x0.astype(F32)
    x1 = _bf(r1)
    x2 = _bf(r1 - x1.astype(F32))
    return x0, x1, x2


def _dot_exact_rhs(a, b_bf16):
    a0, a1, a2 = _split3(a)
    return _mm(a0, b_bf16) + (_mm(a1, b_bf16) + _mm(a2, b_bf16))


def _dot_exact_lhs(a_bf16, b):
    b0, b1, b2 = _split3(b)
    return _mm(a_bf16, b0) + (_mm(a_bf16, b1) + _mm(a_bf16, b2))


def _rms(x, g):
    return x * lax.rsqrt(jnp.mean(x * x, axis=-1, keepdims=True) + EPS) * g


def _sigmoid(x):
    return 0.5 + 0.5 * jnp.tanh(0.5 * x)


def _silu(x):
    return x * _sigmoid(x)


def _softplus(x):
    return jnp.maximum(x, 0.0) + jnp.log(1.0 + jnp.exp(-jnp.abs(x)))


def _iota2(shape, dim):
    return lax.broadcasted_iota(jnp.int32, shape, dim)


def _group_ones(width, group):
    r = _iota2((width, width), 0) // group
    c = _iota2((width, width), 1) // group
    return jnp.where(r == c, 1.0, 0.0).astype(BF16)


def _head_expand(width):
    r = _iota2((LANES, width), 0)
    c = _iota2((LANES, width), 1) // HEAD
    return jnp.where(r == c, 1.0, 0.0).astype(BF16)


def _chunk_tril(tb, chunk):
    r = _iota2((tb, tb), 0)
    c = _iota2((tb, tb), 1)
    return jnp.where(((r // chunk) == (c // chunk)) & (c <= r), 1.0, 0.0).astype(BF16)


def _pad_rows(x, rows):
    if x.shape[0] == rows:
        return x
    return jnp.concatenate([x, jnp.zeros((rows - x.shape[0], x.shape[1]), x.dtype)], axis=0)


def _shift_rows(x, prev, i):
    rolled = pltpu.roll(x, i, 0)
    first = jnp.where(_iota2((SUBLANES, x.shape[1]), 0) < i, pltpu.roll(prev, i, 0),
                      rolled[:SUBLANES])
    return jnp.concatenate([first, rolled[SUBLANES:]], axis=0)


def _bd(y, left):
    z = jnp.zeros_like(y)
    return jnp.concatenate([jnp.where(left, y, z), jnp.where(left, z, y)], axis=0)


def _diag(r, left):
    return jnp.where(left, r[:HEAD], r[HEAD:])


BASE = 16


def _bd8(y):
    g = _iota2((BASE, LANES), 1) // BASE
    z = jnp.zeros_like(y)
    return jnp.concatenate([jnp.where(g == i, y, z) for i in range(LANES // BASE)], axis=0)


def _tri_inv_sbs(a_list, left, tick=lambda: None):
    r = _iota2((CORE, LANES), 0)
    c = _iota2((CORE, LANES), 1) % HEAD
    nb = CORE // BASE
    lane16 = _iota2((BASE, LANES), 1)
    blk_of_lane = (lane16 % HEAD) // BASE
    eye16 = jnp.where(_iota2((BASE, LANES), 0) == lane16 % BASE, 1.0, 0.0).astype(F32)

    def diag_blocks(a):
        d = a[(nb - 1) * BASE:]
        for b in range(nb - 2, -1, -1):
            d = jnp.where(blk_of_lane == b, a[b * BASE:(b + 1) * BASE], d)
        return d

    d0f = [diag_blocks(a) for a in a_list]
    d0 = [_bf(x) for x in d0f]
    pw = [_bf(_mm(x, _bd8(x))) for x in d0]
    tick()
    m = [eye16 - x for x in d0f]
    for _ in range(2):
        bdp = [_bd8(x) for x in pw]
        both = [_mm(jnp.concatenate([x, _bf(mi)], axis=0), y) for x, mi, y in zip(pw, m, bdp)]
        tick()
        pw = [_bf(t[:BASE]) for t in both]
        m = [mi + t[BASE:] for mi, t in zip(m, both)]
    m = [mi + _mm(_bf(mi), _bd8(x)) for mi, x in zip(m, pw)]
    tick()
    m = [jnp.concatenate([jnp.where(blk_of_lane == b, mi, 0.0) for b in range(nb)], axis=0) for mi in m]
    size = 2 * BASE
    while size <= CORE:
        half = size // 2
        nh = CORE // half
        sel = ((r // size) == (c // size)) & ((r // half) != (c // half))
        bdb = [_bd(_bf(jnp.where(sel, a, 0.0)), left) for a in a_list]
        mb = [_bf(mi) for mi in m]
        low = [jnp.concatenate([x[b * half:(b + 1) * half] for b in range(1, nh, 2)], axis=0) for x in mb]
        t = [_bf(_mm(x, y)) for x, y in zip(low, bdb)]
        tick()
        upd = [_mm(x, _bd(y, left)) for x, y in zip(t, mb)]
        tick()
        m = [jnp.concatenate([mi[b * half:(b + 1) * half] - u[(b // 2) * half:(b // 2 + 1) * half]
                              if b % 2 else mi[b * half:(b + 1) * half] for b in range(nh)], axis=0)
             for mi, u in zip(m, upd)]
        size *= 2
    return m


def _const_spec(shape):
    nd = len(shape)
    return pl.BlockSpec(shape, lambda *_: (0,) * nd, pipeline_mode=pl.Buffered(1))


def _ffn_body(x_ref, g_ref, w13_ref, w2_ref, gn_ref, *o_refs, tail):
    x = x_ref[...]
    h = _rms(x, g_ref[...]).astype(BF16)
    ab = jnp.dot(h, w13_ref[...], preferred_element_type=F32)
    act = (_silu(ab[:, :D_FF]) * ab[:, D_FF:]).astype(BF16)
    y = x + 0.5 * jnp.dot(act, w2_ref[...], preferred_element_type=F32)
    if tail == "normed":
        o_refs[0][...] = _rms(y, gn_ref[...])
    else:
        o_refs[0][...] = y
    if tail == "with_h":
        o_refs[1][...] = _rms(y, gn_ref[...]).astype(BF16)


def _ffn(x2d, g, w13, w2, gn, tail):
    m = x2d.shape[0]
    tm = min(TOKEN_TILE, m)
    tok = pl.BlockSpec((tm, D_MODEL), lambda i: (i, 0))
    out_specs = [tok]
    out_shape = [jax.ShapeDtypeStruct((m, D_MODEL), F32)]
    if tail == "with_h":
        out_specs.append(tok)
        out_shape.append(jax.ShapeDtypeStruct((m, D_MODEL), BF16))
    return pl.pallas_call(
        functools.partial(_ffn_body, tail=tail),
        grid=(m // tm,),
        in_specs=[
            tok,
            _const_spec((1, D_MODEL)),
            _const_spec((D_MODEL, 2 * D_FF)),
            _const_spec((D_FF, D_MODEL)),
            _const_spec((1, D_MODEL)),
        ],
        out_specs=out_specs,
        out_shape=out_shape,
        compiler_params=pltpu.CompilerParams(
            dimension_semantics=("arbitrary",), vmem_limit_bytes=VMEM_LIMIT),
        name="ffn",
    )(x2d, g, w13, w2, gn)


RET_PAIRS = RET_HEADS // 2
RET_LOG_G = [math.log1p(-(2.0 ** (-5.0 - hh))) for hh in range(RET_HEADS)]


def _ret_body(h_ref, w_ref, cos_ref, sin_ref, gn_ref, s0_ref,
              o_ref, sn_ref, s_scr, q_scr, k_scr, v_scr, o_scr, *, tb):
    j = pl.program_id(1)
    tbp = max(tb, CORE)
    nc = tbp // CORE
    grp = _chunk_groups(nc)
    valid = min(tb, CORE)
    zero_blk = jnp.zeros((RET_DK, RET_DV), F32)

    @pl.when(j == 0)
    def _():
        for p in range(RET_PAIRS):
            s_scr[p] = jnp.concatenate(
                [jnp.concatenate([s0_ref[2 * p], zero_blk], axis=1),
                 jnp.concatenate([zero_blk, s0_ref[2 * p + 1]], axis=1)], axis=0)

    z = jnp.dot(h_ref[...], w_ref[...], preferred_element_type=F32)
    q = z[:, :RET_QK]
    k = z[:, RET_QK:2 * RET_QK]
    cos = cos_ref[...]
    sin = sin_ref[...]
    first = (_iota2((tb, RET_QK), 1) % RET_DK) < (RET_DK // 2)

    def rot(t):
        swapped = jnp.where(first, pltpu.roll(t, RET_QK - RET_DK // 2, 1),
                            pltpu.roll(t, RET_DK // 2, 1))
        return t * cos + swapped * sin

    q_scr[...] = _pad_rows(rot(q), tbp)
    k_scr[...] = _pad_rows(rot(k) * (RET_DK ** -0.5), tbp)
    v_scr[...] = _pad_rows(z[:, 2 * RET_QK:2 * RET_QK + RET_W], tbp)

    left, incl, _ = _core_masks()
    ri = _iota2((CORE, LANES), 0)
    ci = _iota2((CORE, LANES), 1) % HEAD
    rf = ri.astype(F32)
    dfi = jnp.where(incl, (ri - ci).astype(F32), 0.0)
    row2 = _iota2((2 * RET_DK, 2 * RET_DV), 0) < RET_DK
    col2 = _iota2((2 * RET_DK, 2 * RET_DV), 1) < RET_DV
    blockmask = row2 == col2
    vtop = _iota2((CORE, 2 * RET_DV), 1) < RET_DV
    dmask, qdec, kdec, cdec = [], [], [], []
    for p in range(RET_PAIRS):
        lg = jnp.where(left, RET_LOG_G[2 * p], RET_LOG_G[2 * p + 1])
        dmask.append(jnp.where(incl, jnp.exp(dfi * lg), 0.0))
        qdec.append(jnp.exp((rf + 1.0) * lg))
        kdec.append(jnp.exp((valid - 1.0 - rf) * lg))
        cdec.append(jnp.where(row2, math.exp(valid * RET_LOG_G[2 * p]),
                              math.exp(valid * RET_LOG_G[2 * p + 1])))
    psl = [slice(p * LANES, (p + 1) * LANES) for p in range(RET_PAIRS)]
    vsl = [slice(p * 2 * RET_DV, (p + 1) * 2 * RET_DV) for p in range(RET_PAIRS)]

    def body(i, carry):
        chunks = []
        for gi in range(grp):
            rows = pl.ds(pl.multiple_of((i * grp + gi) * CORE, CORE), CORE)
            chunks.append((rows, q_scr[rows, :], k_scr[rows, :], v_scr[rows, :]))
        items = [(ch, p) for ch in chunks for p in range(RET_PAIRS)]
        scores = [_mm(_bf(qc[:, psl[p]]), _bd(_bf(kc[:, psl[p]]), left), NT) * dmask[p]
                  for (_, qc, kc, _), p in items]
        kv = [_mm(_bf(kc[:, psl[p]] * kdec[p]), _bf(vc[:, vsl[p]]), TN)
              for (_, _, kc, vc), p in items]
        state = [s_scr[p] for p in range(RET_PAIRS)]
        for n, ((rows, qc, _, vc), p) in enumerate(items):
            vp = _bf(vc[:, vsl[p]])
            zv = jnp.zeros_like(vp)
            rhs = jnp.concatenate([jnp.where(vtop, vp, zv), jnp.where(vtop, zv, vp), _bf(state[p])],
                                  axis=0)
            lhs = _bf(jnp.concatenate([scores[n], qc[:, psl[p]] * qdec[p]], axis=1))
            o_scr[rows, vsl[p]] = _mm(lhs, rhs)
            state[p] = cdec[p] * state[p] + jnp.where(blockmask, kv[n], 0.0)
        for p in range(RET_PAIRS):
            s_scr[p] = state[p]
        return carry

    lax.fori_loop(0, nc // grp, body, 0)

    gate = z[:, 2 * RET_QK + RET_W:]
    gn = gn_ref[...]
    for hh in range(RET_HEADS):
        sl = slice(hh * RET_DV, (hh + 1) * RET_DV)
        oh = o_scr[pl.ds(0, tb), sl]
        mu = jnp.mean(oh, axis=-1, keepdims=True)
        dlt = oh - mu
        var = jnp.mean(dlt * dlt, axis=-1, keepdims=True)
        y = dlt * lax.rsqrt(var + 1e-5) * gn[:, sl]
        o_ref[:, sl] = (y * _silu(gate[:, sl])).astype(BF16)

    @pl.when(j == pl.num_programs(1) - 1)
    def _():
        for p in range(RET_PAIRS):
            s = s_scr[p]
            sn_ref[2 * p] = s[:RET_DK, :RET_DV]
            sn_ref[2 * p + 1] = s[RET_DK:, RET_DV:]


def _retention(h, w, cos, sin, gn, s0, tb):
    b, l, _ = h.shape
    st = (RET_HEADS, RET_DK, RET_DV)
    tbp = max(tb, CORE)
    return pl.pallas_call(
        functools.partial(_ret_body, tb=tb),
        grid=(b, l // tb),
        in_specs=[
            pl.BlockSpec((None, tb, D_MODEL), lambda i, j: (i, j, 0)),
            _const_spec(w.shape),
            pl.BlockSpec((tb, RET_QK), lambda i, j: (j, 0)),
            pl.BlockSpec((tb, RET_QK), lambda i, j: (j, 0)),
            _const_spec((1, RET_W)),
            pl.BlockSpec((None,) + st, lambda i, j: (i, 0, 0, 0)),
        ],
        out_specs=[
            pl.BlockSpec((None, tb, RET_W), lambda i, j: (i, j, 0)),
            pl.BlockSpec((None,) + st, lambda i, j: (i, 0, 0, 0)),
        ],
        out_shape=[jax.ShapeDtypeStruct((b, l, RET_W), BF16),
                   jax.ShapeDtypeStruct((b,) + st, F32)],
        scratch_shapes=[
            pltpu.VMEM((RET_PAIRS, 2 * RET_DK, 2 * RET_DV), F32),
            pltpu.VMEM((tbp, RET_QK), F32),
            pltpu.VMEM((tbp, RET_QK), F32),
            pltpu.VMEM((tbp, RET_W), F32),
            pltpu.VMEM((tbp, RET_W), F32),
        ],
        compiler_params=pltpu.CompilerParams(
            dimension_semantics=("arbitrary", "arbitrary"), vmem_limit_bytes=VMEM_LIMIT),
        name="retention",
    )(h, w, cos, sin, gn, s0)


HALO = SUBLANES


def _core_masks():
    r = _iota2((CORE, LANES), 0)
    c = _iota2((CORE, LANES), 1) % HEAD
    left = _iota2((CORE, LANES), 1) < HEAD
    return left, r >= c, r > c


def _load_state_sbs(s0_ref, s_scr):
    for p in range(PAIRS):
        s_scr[p] = jnp.concatenate([s0_ref[2 * p], s0_ref[2 * p + 1]], axis=1)


def _store_state_sbs(s_scr, sn_ref):
    for p in range(PAIRS):
        s = s_scr[p]
        sn_ref[2 * p] = s[:, :HEAD]
        sn_ref[2 * p + 1] = s[:, HEAD:]


def _chunk_groups(nc):
    return 8 if nc % 8 == 0 else (4 if nc % 4 == 0 else (2 if nc % 2 == 0 else 1))


GDN_COLS = GDN_CONV_CH + GDN_V + 2 * LANES


def _gdn_body(h_ref, w_ref, cw_ref, alog_ref, dtb_ref, nrm_ref, s0_ref, c0_ref,
              o_ref, sn_ref, cn_ref,
              s_scr, xp_scr, q_scr, k_scr, v_scr, be_scr, ga_scr, o_scr, m_scr, n_scr, qp_scr,
              gar_scr, egl_scr, *, tb):
    j = pl.program_id(1)
    tbp = max(tb, CORE)
    nc = tbp // CORE
    grp = _chunk_groups(nc)

    @pl.when(j == 0)
    def _():
        _load_state_sbs(s0_ref, s_scr)
        xp_scr[...] = jnp.zeros_like(xp_scr)
        xp_scr[pl.ds(HALO - (CONV_W - 1), CONV_W - 1), :] = c0_ref[...]

    z = jnp.dot(h_ref[...], w_ref[...], preferred_element_type=F32)
    zq = z[:, :GDN_CONV_CH]
    prev = xp_scr[...]
    cw = cw_ref[...]
    y = zq * cw[CONV_W - 1:CONV_W, :]
    for i in range(1, CONV_W):
        y = y + _shift_rows(zq, prev, i) * cw[CONV_W - 1 - i:CONV_W - i, :]
    xp_scr[...] = zq[tb - HALO:, :]

    @pl.when(j == pl.num_programs(1) - 1)
    def _():
        cn_ref[...] = xp_scr[pl.ds(HALO - (CONV_W - 1), CONV_W - 1), :]

    qkv = _silu(y)
    gones = _group_ones(GDN_QK, GDN_DK)
    q = qkv[:, :GDN_QK]
    k = qkv[:, GDN_QK:2 * GDN_QK]
    q = _pad_rows(q * lax.rsqrt(_dot1(q * q, gones) + 1e-6) * (GDN_DK ** -0.5), tbp)
    k = _pad_rows(k * lax.rsqrt(_dot1(k * k, gones) + 1e-6), tbp)
    v = _pad_rows(qkv[:, 2 * GDN_QK:], tbp)

    zb = z[:, GDN_CONV_CH + GDN_V:GDN_CONV_CH + GDN_V + LANES]
    za = z[:, GDN_CONV_CH + GDN_V + LANES:]
    beta = _pad_rows(_sigmoid(zb), tbp)
    log_a = _pad_rows(-jnp.exp(alog_ref[...]) * _softplus(za + dtb_ref[...]), tbp)
    ga = _dot_exact_lhs(_chunk_tril(tbp, CORE), log_a)
    expand = _head_expand(GDN_V)
    gae = _dot_exact_rhs(ga, expand)
    bee = _dot_exact_rhs(beta, expand)
    rr = _iota2((tbp, GDN_V), 0)
    cc = _iota2((tbp, GDN_V), 1)
    pick = jnp.where((rr % CORE) == (cc % HEAD), gae, 0.0)
    ncp = gar_scr.shape[0]
    chunk_of = jnp.where(_iota2((ncp, tbp), 0) == _iota2((ncp, tbp), 1) // CORE, 1.0, 0.0)
    garv = _dot_exact_lhs(_bf(chunk_of), pick)

    left, incl, strict = _core_masks()
    incl4 = jnp.concatenate([incl] * PAIRS, axis=1)
    psl = [slice(p * LANES, (p + 1) * LANES) for p in range(PAIRS)]

    def phase1(i, carry):
        probs = []
        for gi in range(grp):
            c = i * grp + gi
            rows = pl.ds(pl.multiple_of(c * CORE, CORE), CORE)
            gac = ga_scr[rows, :]
            glast = gac[CORE - 1:CORE, :]
            egl_scr[pl.ds(c, 1), :] = jnp.exp(glast)
            eg = jnp.exp(gac)
            gam = jnp.where(incl4, jnp.exp(jnp.where(incl4, gac - gar_scr[pl.ds(c, 1), :], 0.0)), 0.0)
            qc = q_scr[rows, :]
            kc = k_scr[rows, :]
            bec = be_scr[rows, :]
            probs.append(dict(rows=rows, gam=gam, be=bec, q=_bf(qc), k=_bf(kc), qe=qc * eg,
                              kd=_bf(kc * jnp.exp(glast - gac)), bv=_bf(bec * v_scr[rows, :]),
                              bek=_bf(bec * eg * kc)))
        items = [(pr, s) for pr in probs for s in psl]
        qk = [_mm(jnp.concatenate([pr["q"][:, s], pr["k"][:, s]], axis=0), _bd(pr["k"][:, s], left), NT)
              for pr, s in items]
        pm = [_bf(x[:CORE] * pr["gam"][:, s]) for x, (pr, s) in zip(qk, items)]
        amat = [jnp.where(strict, pr["be"][:, s] * x[CORE:] * pr["gam"][:, s], 0.0)
                for x, (pr, s) in zip(qk, items)]
        tinv = _tri_inv_sbs(amat, left)
        sol = [_bf(_mm(_bf(t), jnp.concatenate([_bd(pr["bv"][:, s], left), _bd(pr["bek"][:, s], left)],
                                               axis=1)))
               for t, (pr, s) in zip(tinv, items)]
        kts = [_mm(pr["kd"][:, s], x, TN) for x, (pr, s) in zip(sol, items)]
        ps = [_mm(p_, jnp.concatenate([_bd(x[:, :LANES], left), _bd(x[:, LANES:], left)], axis=1))
              for p_, x in zip(pm, sol)]
        for n, (pr, s) in enumerate(items):
            n_scr[pr["rows"], s] = _diag(kts[n][:, :LANES], left)
            m_scr[pr["rows"], s] = _diag(kts[n][:, LANES:], left)
            o_scr[pr["rows"], s] = ps[n][:, :LANES]
            qp_scr[pr["rows"], s] = pr["qe"][:, s] - ps[n][:, LANES:]
        return carry

    lax.fori_loop(0, nc // grp, phase1, 0, unroll=True)

    def phase2(c, carry):
        rows = pl.ds(pl.multiple_of(c * CORE, CORE), CORE)
        mc = _bf(m_scr[rows, :])
        nn = n_scr[rows, :]
        qpc = _bf(qp_scr[rows, :])
        eglc = egl_scr[pl.ds(c, 1), :]
        s_old = [s_scr[p] for p in range(PAIRS)]
        res = [_mm(jnp.concatenate([mc[:, psl[p]], qpc[:, psl[p]]], axis=0), _bd(_bf(s_old[p]), left))
               for p in range(PAIRS)]
        for p in range(PAIRS):
            s_scr[p] = eglc[:, psl[p]] * s_old[p] - res[p][:CORE] + nn[:, psl[p]]
        o_scr[rows, :] = o_scr[rows, :] + jnp.concatenate([x[CORE:] for x in res], axis=1)
        return carry

    lax.fori_loop(0, nc, phase2, 0, unroll=True)

    q_scr[...] = q
    k_scr[...] = k
    v_scr[...] = v
    be_scr[...] = bee
    ga_scr[...] = gae
    gar_scr[...] = garv

    o = o_scr[pl.ds(0, tb), :]
    ms = _dot1(o * o, _group_ones(GDN_V, GDN_DV)) * (1.0 / GDN_DV)
    gate = z[:, GDN_CONV_CH:GDN_CONV_CH + GDN_V]
    o_ref[...] = (o * lax.rsqrt(ms + EPS) * nrm_ref[...] * _silu(gate)).astype(BF16)

    @pl.when(j == pl.num_programs(1) - 1)
    def _():
        _store_state_sbs(s_scr, sn_ref)


def _gdn(h, w, cw, alog, dtb, nrm, s0, c0, tb):
    b, l, _ = h.shape
    st = (GDN_HEADS, GDN_DK, GDN_DV)
    cst = (CONV_W - 1, GDN_CONV_CH)
    tbp = max(tb, CORE)
    big = pltpu.VMEM((tbp, GDN_V), F32)
    small = pltpu.VMEM((max(tbp // CORE, SUBLANES), GDN_V), F32)
    return pl.pallas_call(
        functools.partial(_gdn_body, tb=tb),
        grid=(b, l // tb),
        in_specs=[
            pl.BlockSpec((None, tb, D_MODEL), lambda i, j: (i, j, 0)),
            _const_spec(w.shape),
            _const_spec(cw.shape),
            _const_spec((1, LANES)),
            _const_spec((1, LANES)),
            _const_spec((1, GDN_V)),
            pl.BlockSpec((None,) + st, lambda i, j: (i, 0, 0, 0)),
            pl.BlockSpec((None,) + cst, lambda i, j: (i, 0, 0)),
        ],
        out_specs=[
            pl.BlockSpec((None, tb, GDN_V), lambda i, j: (i, j, 0)),
            pl.BlockSpec((None,) + st, lambda i, j: (i, 0, 0, 0)),
            pl.BlockSpec((None,) + cst, lambda i, j: (i, 0, 0)),
        ],
        out_shape=[jax.ShapeDtypeStruct((b, l, GDN_V), BF16),
                   jax.ShapeDtypeStruct((b,) + st, F32),
                   jax.ShapeDtypeStruct((b,) + cst, F32)],
        scratch_shapes=[
            pltpu.VMEM((PAIRS, HEAD, LANES), F32),
            pltpu.VMEM((HALO, GDN_CONV_CH), F32),
            big, big, big, big, big, big, big, big, big,
            small, small,
        ],
        compiler_params=pltpu.CompilerParams(
            dimension_semantics=("arbitrary", "arbitrary"), vmem_limit_bytes=VMEM_LIMIT),
        name="gdn",
    )(h, w, cw, alog, dtb, nrm, s0, c0)


SLAB = 128


def _pipe_maps(nblk, total):
    def in_blk(t):
        ta = jnp.minimum(t, total - 1)
        return ta // nblk, ta % nblk

    def out_blk(t):
        tp = jnp.maximum(t - 1, 0)
        return tp // nblk, tp % nblk

    return in_blk, out_blk


def _gdn_pipe_body(h_ref, w_ref, cw_ref, alog_ref, dtb_ref, nrm_ref, s0_ref, c0_ref,
                   o_ref, sn_ref, cn_ref,
                   s_scr, xp_scr, q_scr, k_scr, v_scr, be_scr, ga_scr, gate_scr, gar_scr,
                   o_scr, m_scr, n_scr, qp_scr, egl_scr, *, tb, nblk):
    t = pl.program_id(0)
    total = pl.num_programs(0) - 1
    ja = jnp.minimum(t, total - 1) % nblk
    jp = jnp.maximum(t - 1, 0) % nblk
    wslot = t % 2
    rslot = 1 - wslot
    tbp = max(tb, CORE)
    nc = tbp // CORE
    slab = min(SLAB, tb)
    slabp = max(slab, CORE)
    nslab = tb // slab

    @pl.when(t == 0)
    def _():
        for ref in (q_scr, k_scr, v_scr, be_scr, ga_scr, gate_scr, gar_scr):
            ref[...] = jnp.zeros_like(ref)

    @pl.when(ja == 0)
    def _():
        xp_scr[...] = jnp.zeros_like(xp_scr)
        xp_scr[pl.ds(HALO - (CONV_W - 1), CONV_W - 1), :] = c0_ref[...]

    @pl.when(jp == 0)
    def _():
        _load_state_sbs(s0_ref, s_scr)

    gones = _group_ones(GDN_QK, GDN_DK)
    expand = _head_expand(GDN_V)
    tril = _chunk_tril(slabp, CORE)
    ncp = gar_scr.shape[1]
    rr = _iota2((slabp, GDN_V), 0)
    cc = _iota2((slabp, GDN_V), 1)
    on_diag = (rr % CORE) == (cc % HEAD)

    def input_stage():
        prev = xp_scr[...]
        garv = jnp.zeros((ncp, GDN_V), F32)
        for s in range(nslab):
            z = jnp.dot(h_ref[pl.ds(s * slab, slab), :], w_ref[...], preferred_element_type=F32)
            yield
            zq = z[:, :GDN_CONV_CH]
            cw = cw_ref[...]
            y = zq * cw[CONV_W - 1:CONV_W, :]
            for i in range(1, CONV_W):
                y = y + _shift_rows(zq, prev, i) * cw[CONV_W - 1 - i:CONV_W - i, :]
            prev = zq[slab - HALO:, :]
            yield
            qkv = _silu(y)
            q = qkv[:, :GDN_QK]
            k = qkv[:, GDN_QK:2 * GDN_QK]
            q = _pad_rows(q * lax.rsqrt(_dot1(q * q, gones) + 1e-6) * (GDN_DK ** -0.5), slabp)
            k = _pad_rows(k * lax.rsqrt(_dot1(k * k, gones) + 1e-6), slabp)
            v = _pad_rows(qkv[:, 2 * GDN_QK:], slabp)
            yield
            zb = z[:, GDN_CONV_CH + GDN_V:GDN_CONV_CH + GDN_V + LANES]
            za = z[:, GDN_CONV_CH + GDN_V + LANES:]
            beta = _pad_rows(_sigmoid(zb), slabp)
            log_a = _pad_rows(-jnp.exp(alog_ref[...]) * _softplus(za + dtb_ref[...]), slabp)
            ga = _dot_exact_lhs(tril, log_a)
            gae = _dot_exact_rhs(ga, expand)
            bee = _dot_exact_rhs(beta, expand)
            pick = jnp.where(on_diag, gae, 0.0)
            chunk_of = jnp.where(_iota2((ncp, slabp), 0)
                                 == _iota2((ncp, slabp), 1) // CORE + s * (slabp // CORE), 1.0, 0.0)
            garv = garv + _dot_exact_lhs(_bf(chunk_of), pick)
            yield
            rows = pl.ds(s * slabp, slabp)
            q_scr[wslot, rows, :] = q
            k_scr[wslot, rows, :] = k
            v_scr[wslot, rows, :] = v
            be_scr[wslot, rows, :] = bee
            ga_scr[wslot, rows, :] = gae
            gate_scr[wslot, pl.ds(s * slab, slab), :] = z[:, GDN_CONV_CH:GDN_CONV_CH + GDN_V]
            yield
        xp_scr[...] = prev
        gar_scr[wslot] = garv
        yield

    gen = input_stage()

    def tick():
        next(gen, None)

    left, incl, strict = _core_masks()
    incl4 = jnp.concatenate([incl] * PAIRS, axis=1)
    psl = [slice(p * LANES, (p + 1) * LANES) for p in range(PAIRS)]

    probs = []
    for c in range(nc):
        rows = pl.ds(c * CORE, CORE)
        gac = ga_scr[rslot, rows, :]
        glast = gac[CORE - 1:CORE, :]
        egl_scr[pl.ds(c, 1), :] = jnp.exp(glast)
        eg = jnp.exp(gac)
        gam = jnp.where(incl4, jnp.exp(jnp.where(incl4, gac - gar_scr[rslot, pl.ds(c, 1), :], 0.0)), 0.0)
        qc = q_scr[rslot, rows, :]
        kc = k_scr[rslot, rows, :]
        bec = be_scr[rslot, rows, :]
        probs.append(dict(rows=rows, gam=gam, be=bec, q=_bf(qc), k=_bf(kc), qe=qc * eg,
                          kd=_bf(kc * jnp.exp(glast - gac)), bv=_bf(bec * v_scr[rslot, rows, :]),
                          bek=_bf(bec * eg * kc)))
    tick()
    items = [(pr, s) for pr in probs for s in psl]
    qk = [_mm(jnp.concatenate([pr["q"][:, s], pr["k"][:, s]], axis=0), _bd(pr["k"][:, s], left), NT)
          for pr, s in items]
    tick()
    pm = [_bf(x[:CORE] * pr["gam"][:, s]) for x, (pr, s) in zip(qk, items)]
    amat = [jnp.where(strict, pr["be"][:, s] * x[CORE:] * pr["gam"][:, s], 0.0)
            for x, (pr, s) in zip(qk, items)]
    tinv = _tri_inv_sbs(amat, left, tick)
    sol = [_bf(_mm(_bf(tm), jnp.concatenate([_bd(pr["bv"][:, s], left), _bd(pr["bek"][:, s], left)],
                                            axis=1)))
           for tm, (pr, s) in zip(tinv, items)]
    tick()
    kts = [_mm(pr["kd"][:, s], x, TN) for x, (pr, s) in zip(sol, items)]
    tick()
    ps = [_mm(p_, jnp.concatenate([_bd(x[:, :LANES], left), _bd(x[:, LANES:], left)], axis=1))
          for p_, x in zip(pm, sol)]
    tick()
    for n, (pr, s) in enumerate(items):
        n_scr[pr["rows"], s] = _diag(kts[n][:, :LANES], left)
        m_scr[pr["rows"], s] = _diag(kts[n][:, LANES:], left)
        o_scr[pr["rows"], s] = ps[n][:, :LANES]
        qp_scr[pr["rows"], s] = pr["qe"][:, s] - ps[n][:, LANES:]

    for c in range(nc):
        rows = pl.ds(c * CORE, CORE)
        mc = _bf(m_scr[rows, :])
        nn = n_scr[rows, :]
        qpc = _bf(qp_scr[rows, :])
        eglc = egl_scr[pl.ds(c, 1), :]
        s_old = [s_scr[p] for p in range(PAIRS)]
        res = [_mm(jnp.concatenate([mc[:, psl[p]], qpc[:, psl[p]]], axis=0), _bd(_bf(s_old[p]), left))
               for p in range(PAIRS)]
        for p in range(PAIRS):
            s_scr[p] = eglc[:, psl[p]] * s_old[p] - res[p][:CORE] + nn[:, psl[p]]
        o_scr[rows, :] = o_scr[rows, :] + jnp.concatenate([x[CORE:] for x in res], axis=1)
        tick()
    for _ in gen:
        pass

    o = o_scr[pl.ds(0, tb), :]
    ms = _dot1(o * o, gones) * (1.0 / GDN_DV)
    gate = gate_scr[rslot]
    o_ref[...] = (o * lax.rsqrt(ms + EPS) * nrm_ref[...] * _silu(gate)).astype(BF16)

    @pl.when((jp == nblk - 1) & (t > 0))
    def _():
        _store_state_sbs(s_scr, sn_ref)

    @pl.when(ja == nblk - 1)
    def _():
        cn_ref[...] = xp_scr[pl.ds(HALO - (CONV_W - 1), CONV_W - 1), :]


def _gdn_pipe(h, w, cw, alog, dtb, nrm, s0, c0, tb):
    b, l, _ = h.shape
    nblk = l // tb
    total = b * nblk
    in_blk, out_blk = _pipe_maps(nblk, total)
    st = (GDN_HEADS, GDN_DK, GDN_DV)
    cst = (CONV_W - 1, GDN_CONV_CH)
    tbp = max(tb, CORE)
    dbl = pltpu.VMEM((2, tbp, GDN_V), F32)
    big = pltpu.VMEM((tbp, GDN_V), F32)
    ncp = max(tbp // CORE, SUBLANES)
    return pl.pallas_call(
        functools.partial(_gdn_pipe_body, tb=tb, nblk=nblk),
        grid=(total + 1,),
        in_specs=[
            pl.BlockSpec((None, tb, D_MODEL), lambda t: in_blk(t) + (0,)),
            _const_spec(w.shape),
            _const_spec(cw.shape),
            _const_spec((1, LANES)),
            _const_spec((1, LANES)),
            _const_spec((1, GDN_V)),
            pl.BlockSpec((None,) + st, lambda t: (out_blk(t)[0], 0, 0, 0)),
            pl.BlockSpec((None,) + cst, lambda t: (in_blk(t)[0], 0, 0)),
        ],
        out_specs=[
            pl.BlockSpec((None, tb, GDN_V), lambda t: out_blk(t) + (0,)),
            pl.BlockSpec((None,) + st, lambda t: (out_blk(t)[0], 0, 0, 0)),
            pl.BlockSpec((None,) + cst, lambda t: (in_blk(t)[0], 0, 0)),
        ],
        out_shape=[jax.ShapeDtypeStruct((b, l, GDN_V), BF16),
                   jax.ShapeDtypeStruct((b,) + st, F32),
                   jax.ShapeDtypeStruct((b,) + cst, F32)],
        scratch_shapes=[
            pltpu.VMEM((PAIRS, HEAD, LANES), F32),
            pltpu.VMEM((HALO, GDN_CONV_CH), F32),
            dbl, dbl, dbl, dbl, dbl,
            pltpu.VMEM((2, tb, GDN_V), F32),
            pltpu.VMEM((2, ncp, GDN_V), F32),
            big, big, big, big,
            pltpu.VMEM((ncp, GDN_V), F32),
        ],
        compiler_params=pltpu.CompilerParams(
            dimension_semantics=("arbitrary",), vmem_limit_bytes=VMEM_LIMIT),
        name="gdn",
    )(h, w, cw, alog, dtb, nrm, s0, c0)


def _rwkv_body(h_ref, w_ref, mu_ref, w0_ref, w2_ref, a0_ref, a2_ref, g2_ref,
               kkw_ref, ka_ref, rk_ref, gn_ref, s0_ref, sh0_ref,
               o_ref, sn_ref, shn_ref,
               s_scr, zc_scr, r_scr, k2_scr, v_scr, kk_scr, kka_scr, lw_scr, cum_scr, o_scr,
               qp_scr, m_scr, n_scr, wt_scr, *, tb):
    j = pl.program_id(1)
    tbp = max(tb, CORE)
    nc = tbp // CORE
    grp = _chunk_groups(nc)

    @pl.when(j == 0)
    def _():
        _load_state_sbs(s0_ref, s_scr)
        zc_scr[...] = jnp.zeros_like(zc_scr)
        zc_scr[pl.ds(HALO - 1, 1), :] = sh0_ref[...]

    z = jnp.dot(h_ref[...], w_ref[...], preferred_element_type=F32)
    zp = _shift_rows(z, zc_scr[...], 1)
    zc_scr[...] = z[tb - HALO:, :]

    @pl.when(j == pl.num_programs(1) - 1)
    def _():
        shn_ref[...] = zc_scr[pl.ds(HALO - 1, 1), :]

    cs = z + (zp - z) * mu_ref[...]
    r = cs[:, :RWKV_W]
    k = cs[:, RWKV_W:2 * RWKV_W]
    v = cs[:, 2 * RWKV_W:3 * RWKV_W]
    zwa = cs[:, 3 * RWKV_W:3 * RWKV_W + LANES]
    zg = cs[:, 3 * RWKV_W + LANES:]
    w_raw = -_softplus(-(w0_ref[...] + _dot1(jnp.tanh(zwa), w2_ref[...]))) - 0.5
    lw = _pad_rows(-jnp.exp(w_raw), tbp)
    a = _sigmoid(a0_ref[...] + _dot1(zwa, a2_ref[...]))
    g = _dot1(_sigmoid(zg), g2_ref[...])
    gones = _group_ones(RWKV_W, RWKV_N)
    kkn = k * kkw_ref[...]
    kkn = kkn * lax.rsqrt(_dot1(kkn * kkn, gones) + 1e-6)
    k2 = k * (1.0 + (a - 1.0) * ka_ref[...])
    bonus = _dot1(r * k2 * rk_ref[...], gones) * v

    r_scr[...] = _pad_rows(r, tbp)
    k2_scr[...] = _pad_rows(k2, tbp)
    v_scr[...] = _pad_rows(v, tbp)
    kk_scr[...] = _pad_rows(kkn, tbp)
    kka_scr[...] = _pad_rows(kkn * a, tbp)
    lw_scr[...] = lw
    cum_scr[...] = _dot_exact_lhs(_chunk_tril(tbp, CORE), lw)

    left, incl, strict = _core_masks()
    psl = [slice(p * LANES, (p + 1) * LANES) for p in range(PAIRS)]

    def phase1(i, carry):
        probs = []
        for gi in range(grp):
            c = i * grp + gi
            rows = pl.ds(pl.multiple_of(c * CORE, CORE), CORE)
            cum = cum_scr[rows, :]
            tot = cum[CORE - 1:CORE, :]
            wt_scr[pl.ds(c, 1), :] = jnp.exp(tot)
            w_inv = jnp.exp(-cum)
            w_rem = jnp.exp(tot - cum)
            kkac = kka_scr[rows, :]
            k2c = k2_scr[rows, :]
            rt = r_scr[rows, :] * jnp.exp(cum)
            probs.append(dict(rows=rows, rt=rt, rtb=_bf(rt),
                              at=_bf(-kk_scr[rows, :] * jnp.exp(cum - lw_scr[rows, :])),
                              bt=_bf(kkac * w_inv), kt=_bf(k2c * w_inv), bh=_bf(kkac * w_rem),
                              kh=_bf(k2c * w_rem), v=_bf(v_scr[rows, :])))
        items = [(pr, s) for pr in probs for s in psl]
        x = [_mm(jnp.concatenate([pr["at"][:, s], pr["rtb"][:, s]], axis=0),
                 jnp.concatenate([_bd(pr["bt"][:, s], left), _bd(pr["kt"][:, s], left)], axis=0), NT)
             for pr, s in items]
        neg_ab = [jnp.where(strict, -t[:CORE, :LANES], 0.0) for t in x]
        a_rb = [_bf(jnp.where(incl, t[CORE:, :LANES], 0.0)) for t in x]
        a_ak = [_bf(jnp.where(strict, t[:CORE, LANES:], 0.0)) for t in x]
        a_rk = [_bf(jnp.where(incl, t[CORE:, LANES:], 0.0)) for t in x]
        tinv = _tri_inv_sbs(neg_ab, left)
        vbd = [_bd(pr["v"][:, s], left) for pr, s in items]
        akv = [_bf(_mm(p_, q_)) for p_, q_ in zip(a_ak, vbd)]
        pu = [_bf(_mm(_bf(t), jnp.concatenate([_bd(pr["at"][:, s], left), _bd(u, left)], axis=1)))
              for t, u, (pr, s) in zip(tinv, akv, items)]
        rbp = [_mm(p_, jnp.concatenate([_bd(u[:, :LANES], left), _bd(u[:, LANES:], left)], axis=1))
               for p_, u in zip(a_rb, pu)]
        rkv = [_mm(p_, q_) for p_, q_ in zip(a_rk, vbd)]
        bp = [_mm(u, pr["bh"][:, s], TN) for u, (pr, s) in zip(pu, items)]
        vk = [_mm(pr["v"][:, s], pr["kh"][:, s], TN) for pr, s in items]
        for n, (pr, s) in enumerate(items):
            qp_scr[pr["rows"], s] = pr["rt"][:, s] + rbp[n][:, :LANES]
            o_scr[pr["rows"], s] = rbp[n][:, LANES:] + rkv[n]
            m_scr[pr["rows"], s] = _diag(bp[n][:LANES], left)
            n_scr[pr["rows"], s] = _diag(bp[n][LANES:], left) + _diag(vk[n], left)
        return carry

    lax.fori_loop(0, nc // grp, phase1, 0, unroll=True)

    def phase2(c, carry):
        rows = pl.ds(pl.multiple_of(c * CORE, CORE), CORE)
        mc = _bf(m_scr[rows, :])
        nn = n_scr[rows, :]
        qpc = _bf(qp_scr[rows, :])
        wtc = wt_scr[pl.ds(c, 1), :]
        s_old = [s_scr[p] for p in range(PAIRS)]
        sb = [_bf(s) for s in s_old]
        od = [_mm(qpc[:, psl[p]], _bd(sb[p], left), NT) for p in range(PAIRS)]
        sm = [_mm(sb[p], _bd(mc[:, psl[p]], left)) for p in range(PAIRS)]
        for p in range(PAIRS):
            s_scr[p] = s_old[p] * wtc[:, psl[p]] + sm[p] + nn[:, psl[p]]
        o_scr[rows, :] = o_scr[rows, :] + jnp.concatenate(od, axis=1)
        return carry

    lax.fori_loop(0, nc, phase2, 0, unroll=True)

    o = o_scr[pl.ds(0, tb), :]
    mean = _dot1(o, gones) * (1.0 / RWKV_N)
    dlt = o - mean
    var = _dot1(dlt * dlt, gones) * (1.0 / RWKV_N)
    y = dlt * lax.rsqrt(var + RWKV_GN_EPS) * gn_ref[...]
    o_ref[...] = ((y + bonus) * g).astype(BF16)

    @pl.when(j == pl.num_programs(1) - 1)
    def _():
        _store_state_sbs(s_scr, sn_ref)


def _rwkv(h, w, mu, w0, w2, a0, a2, g2, kkw, ka, rk, gn, s0, sh0, tb):
    b, l, _ = h.shape
    st = (RWKV_HEADS, RWKV_N, RWKV_N)
    sst = (1, RWKV_SHIFT_CH)
    tbp = max(tb, CORE)
    row = _const_spec((1, RWKV_W))
    big = pltpu.VMEM((tbp, RWKV_W), F32)
    return pl.pallas_call(
        functools.partial(_rwkv_body, tb=tb),
        grid=(b, l // tb),
        in_specs=[
            pl.BlockSpec((None, tb, D_MODEL), lambda i, j: (i, j, 0)),
            _const_spec(w.shape),
            _const_spec((1, RWKV_SHIFT_CH)),
            row, _const_spec((LANES, RWKV_W)), row, _const_spec((LANES, RWKV_W)),
            _const_spec((LANES, RWKV_W)),
            row, row, row, row,
            pl.BlockSpec((None,) + st, lambda i, j: (i, 0, 0, 0)),
            pl.BlockSpec((None,) + sst, lambda i, j: (i, 0, 0)),
        ],
        out_specs=[
            pl.BlockSpec((None, tb, RWKV_W), lambda i, j: (i, j, 0)),
            pl.BlockSpec((None,) + st, lambda i, j: (i, 0, 0, 0)),
            pl.BlockSpec((None,) + sst, lambda i, j: (i, 0, 0)),
        ],
        out_shape=[jax.ShapeDtypeStruct((b, l, RWKV_W), BF16),
                   jax.ShapeDtypeStruct((b,) + st, F32),
                   jax.ShapeDtypeStruct((b,) + sst, F32)],
        scratch_shapes=[
            pltpu.VMEM((PAIRS, HEAD, LANES), F32),
            pltpu.VMEM((HALO, RWKV_SHIFT_CH), F32),
            big, big, big, big, big, big, big, big, big, big, big,
            pltpu.VMEM((max(tbp // CORE, SUBLANES), RWKV_W), F32),
        ],
        compiler_params=pltpu.CompilerParams(
            dimension_semantics=("arbitrary", "arbitrary"), vmem_limit_bytes=VMEM_LIMIT),
        name="rwkv7",
    )(h, w, mu, w0, w2, a0, a2, g2, kkw, ka, rk, gn, s0, sh0)


def _rwkv_pipe_body(h_ref, w_ref, mu_ref, w0_ref, w2_ref, a0_ref, a2_ref, g2_ref,
                    kkw_ref, ka_ref, rk_ref, gn_ref, s0_ref, sh0_ref,
                    o_ref, sn_ref, shn_ref,
                    s_scr, zc_scr, r_scr, k2_scr, v_scr, kk_scr, kka_scr, lw_scr, cum_scr,
                    bonus_scr, g_scr, o_scr, qp_scr, m_scr, n_scr, wt_scr, *, tb, nblk):
    t = pl.program_id(0)
    total = pl.num_programs(0) - 1
    ja = jnp.minimum(t, total - 1) % nblk
    jp = jnp.maximum(t - 1, 0) % nblk
    wslot = t % 2
    rslot = 1 - wslot
    tbp = max(tb, CORE)
    nc = tbp // CORE
    slab = min(SLAB, tb)
    slabp = max(slab, CORE)
    nslab = tb // slab

    @pl.when(t == 0)
    def _():
        for ref in (r_scr, k2_scr, v_scr, kk_scr, kka_scr, lw_scr, cum_scr, bonus_scr, g_scr):
            ref[...] = jnp.zeros_like(ref)

    @pl.when(ja == 0)
    def _():
        zc_scr[...] = jnp.zeros_like(zc_scr)
        zc_scr[pl.ds(HALO - 1, 1), :] = sh0_ref[...]

    @pl.when(jp == 0)
    def _():
        _load_state_sbs(s0_ref, s_scr)

    gones = _group_ones(RWKV_W, RWKV_N)
    tril = _chunk_tril(slabp, CORE)

    def input_stage():
        prev = zc_scr[...]
        for s in range(nslab):
            z = jnp.dot(h_ref[pl.ds(s * slab, slab), :], w_ref[...], preferred_element_type=F32)
            yield
            cs = z + (_shift_rows(z, prev, 1) - z) * mu_ref[...]
            prev = z[slab - HALO:, :]
            yield
            r = cs[:, :RWKV_W]
            k = cs[:, RWKV_W:2 * RWKV_W]
            v = cs[:, 2 * RWKV_W:3 * RWKV_W]
            zwa = cs[:, 3 * RWKV_W:3 * RWKV_W + LANES]
            zg = cs[:, 3 * RWKV_W + LANES:]
            w_raw = -_softplus(-(w0_ref[...] + _dot1(jnp.tanh(zwa), w2_ref[...]))) - 0.5
            lw = _pad_rows(-jnp.exp(w_raw), slabp)
            a = _sigmoid(a0_ref[...] + _dot1(zwa, a2_ref[...]))
            g = _dot1(_sigmoid(zg), g2_ref[...])
            yield
            kkn = k * kkw_ref[...]
            kkn = kkn * lax.rsqrt(_dot1(kkn * kkn, gones) + 1e-6)
            k2 = k * (1.0 + (a - 1.0) * ka_ref[...])
            bonus = _dot1(r * k2 * rk_ref[...], gones) * v
            yield
            rows = pl.ds(s * slabp, slabp)
            r_scr[wslot, rows, :] = _pad_rows(r, slabp)
            k2_scr[wslot, rows, :] = _pad_rows(k2, slabp)
            v_scr[wslot, rows, :] = _pad_rows(v, slabp)
            kk_scr[wslot, rows, :] = _pad_rows(kkn, slabp)
            kka_scr[wslot, rows, :] = _pad_rows(kkn * a, slabp)
            lw_scr[wslot, rows, :] = lw
            cum_scr[wslot, rows, :] = _dot_exact_lhs(tril, lw)
            bonus_scr[wslot, pl.ds(s * slab, slab), :] = bonus
            g_scr[wslot, pl.ds(s * slab, slab), :] = g
            yield
        zc_scr[...] = prev
        yield

    gen = input_stage()

    def tick():
        next(gen, None)

    left, incl, strict = _core_masks()
    psl = [slice(p * LANES, (p + 1) * LANES) for p in range(PAIRS)]

    probs = []
    for c in range(nc):
        rows = pl.ds(c * CORE, CORE)
        cum = cum_scr[rslot, rows, :]
        tot = cum[CORE - 1:CORE, :]
        wt_scr[pl.ds(c, 1), :] = jnp.exp(tot)
        w_inv = jnp.exp(-cum)
        w_rem = jnp.exp(tot - cum)
        kkac = kka_scr[rslot, rows, :]
        k2c = k2_scr[rslot, rows, :]
        rt = r_scr[rslot, rows, :] * jnp.exp(cum)
        probs.append(dict(rows=rows, rt=rt, rtb=_bf(rt),
                          at=_bf(-kk_scr[rslot, rows, :] * jnp.exp(cum - lw_scr[rslot, rows, :])),
                          bt=_bf(kkac * w_inv), kt=_bf(k2c * w_inv), bh=_bf(kkac * w_rem),
                          kh=_bf(k2c * w_rem), v=_bf(v_scr[rslot, rows, :])))
    tick()
    items = [(pr, s) for pr in probs for s in psl]
    x = [_mm(jnp.concatenate([pr["at"][:, s], pr["rtb"][:, s]], axis=0),
             jnp.concatenate([_bd(pr["bt"][:, s], left), _bd(pr["kt"][:, s], left)], axis=0), NT)
         for pr, s in items]
    tick()
    neg_ab = [jnp.where(strict, -tt[:CORE, :LANES], 0.0) for tt in x]
    a_rb = [_bf(jnp.where(incl, tt[CORE:, :LANES], 0.0)) for tt in x]
    a_ak = [_bf(jnp.where(strict, tt[:CORE, LANES:], 0.0)) for tt in x]
    a_rk = [_bf(jnp.where(incl, tt[CORE:, LANES:], 0.0)) for tt in x]
    tinv = _tri_inv_sbs(neg_ab, left, tick)
    vbd = [_bd(pr["v"][:, s], left) for pr, s in items]
    akv = [_bf(_mm(p_, q_)) for p_, q_ in zip(a_ak, vbd)]
    tick()
    pu = [_bf(_mm(_bf(tm), jnp.concatenate([_bd(pr["at"][:, s], left), _bd(u, left)], axis=1)))
          for tm, u, (pr, s) in zip(tinv, akv, items)]
    tick()
    rbp = [_mm(p_, jnp.concatenate([_bd(u[:, :LANES], left), _bd(u[:, LANES:], left)], axis=1))
           for p_, u in zip(a_rb, pu)]
    tick()
    rkv = [_mm(p_, q_) for p_, q_ in zip(a_rk, vbd)]
    tick()
    bp = [_mm(u, pr["bh"][:, s], TN) for u, (pr, s) in zip(pu, items)]
    tick()
    vk = [_mm(pr["v"][:, s], pr["kh"][:, s], TN) for pr, s in items]
    tick()
    for n, (pr, s) in enumerate(items):
        qp_scr[pr["rows"], s] = pr["rt"][:, s] + rbp[n][:, :LANES]
        o_scr[pr["rows"], s] = rbp[n][:, LANES:] + rkv[n]
        m_scr[pr["rows"], s] = _diag(bp[n][:LANES], left)
        n_scr[pr["rows"], s] = _diag(bp[n][LANES:], left) + _diag(vk[n], left)

    for c in range(nc):
        rows = pl.ds(c * CORE, CORE)
        mc = _bf(m_scr[rows, :])
        nn = n_scr[rows, :]
        qpc = _bf(qp_scr[rows, :])
        wtc = wt_scr[pl.ds(c, 1), :]
        s_old = [s_scr[p] for p in range(PAIRS)]
        sb = [_bf(s) for s in s_old]
        od = [_mm(qpc[:, psl[p]], _bd(sb[p], left), NT) for p in range(PAIRS)]
        sm = [_mm(sb[p], _bd(mc[:, psl[p]], left)) for p in range(PAIRS)]
        for p in range(PAIRS):
            s_scr[p] = s_old[p] * wtc[:, psl[p]] + sm[p] + nn[:, psl[p]]
        o_scr[rows, :] = o_scr[rows, :] + jnp.concatenate(od, axis=1)
        tick()
    for _ in gen:
        pass

    o = o_scr[pl.ds(0, tb), :]
    mean = _dot1(o, gones) * (1.0 / RWKV_N)
    dlt = o - mean
    var = _dot1(dlt * dlt, gones) * (1.0 / RWKV_N)
    y = dlt * lax.rsqrt(var + RWKV_GN_EPS) * gn_ref[...]
    o_ref[...] = ((y + bonus_scr[rslot]) * g_scr[rslot]).astype(BF16)

    @pl.when((jp == nblk - 1) & (t > 0))
    def _():
        _store_state_sbs(s_scr, sn_ref)

    @pl.when(ja == nblk - 1)
    def _():
        shn_ref[...] = zc_scr[pl.ds(HALO - 1, 1), :]


def _rwkv_pipe(h, w, mu, w0, w2, a0, a2, g2, kkw, ka, rk, gn, s0, sh0, tb):
    b, l, _ = h.shape
    nblk = l // tb
    total = b * nblk
    in_blk, out_blk = _pipe_maps(nblk, total)
    st = (RWKV_HEADS, RWKV_N, RWKV_N)
    sst = (1, RWKV_SHIFT_CH)
    tbp = max(tb, CORE)
    row = _const_spec((1, RWKV_W))
    dbl = pltpu.VMEM((2, tbp, RWKV_W), F32)
    dbl_tb = pltpu.VMEM((2, tb, RWKV_W), F32)
    big = pltpu.VMEM((tbp, RWKV_W), F32)
    return pl.pallas_call(
        functools.partial(_rwkv_pipe_body, tb=tb, nblk=nblk),
        grid=(total + 1,),
        in_specs=[
            pl.BlockSpec((None, tb, D_MODEL), lambda t: in_blk(t) + (0,)),
            _const_spec(w.shape),
            _const_spec((1, RWKV_SHIFT_CH)),
            row, _const_spec((LANES, RWKV_W)), row, _const_spec((LANES, RWKV_W)),
            _const_spec((LANES, RWKV_W)),
            row, row, row, row,
            pl.BlockSpec((None,) + st, lambda t: (out_blk(t)[0], 0, 0, 0)),
            pl.BlockSpec((None,) + sst, lambda t: (in_blk(t)[0], 0, 0)),
        ],
        out_specs=[
            pl.BlockSpec((None, tb, RWKV_W), lambda t: out_blk(t) + (0,)),
            pl.BlockSpec((None,) + st, lambda t: (out_blk(t)[0], 0, 0, 0)),
            pl.BlockSpec((None,) + sst, lambda t: (in_blk(t)[0], 0, 0)),
        ],
        out_shape=[jax.ShapeDtypeStruct((b, l, RWKV_W), BF16),
                   jax.ShapeDtypeStruct((b,) + st, F32),
                   jax.ShapeDtypeStruct((b,) + sst, F32)],
        scratch_shapes=[
            pltpu.VMEM((PAIRS, HEAD, LANES), F32),
            pltpu.VMEM((HALO, RWKV_SHIFT_CH), F32),
            dbl, dbl, dbl, dbl, dbl, dbl, dbl,
            dbl_tb, dbl_tb,
            big, big, big, big,
            pltpu.VMEM((max(tbp // CORE, SUBLANES), RWKV_W), F32),
        ],
        compiler_params=pltpu.CompilerParams(
            dimension_semantics=("arbitrary",), vmem_limit_bytes=VMEM_LIMIT),
        name="rwkv7",
    )(h, w, mu, w0, w2, a0, a2, g2, kkw, ka, rk, gn, s0, sh0)


def _merge_body(x_ref, h_ref, wg_ref, oa_ref, ob_ref, oc_ref, wb_ref, wo_ref, o_ref):
    x = x_ref[...]
    gate = jnp.dot(h_ref[...], wg_ref[...], preferred_element_type=F32)
    m = None
    for i, br in enumerate((oa_ref, ob_ref, oc_ref)):
        t = _sigmoid(gate[:, i * D_MODEL:(i + 1) * D_MODEL]) * jnp.dot(
            br[...], wb_ref[i], preferred_element_type=F32)
        m = t if m is None else m + t
    o_ref[...] = x + jnp.dot(m.astype(BF16), wo_ref[...], preferred_element_type=F32)


def _merge(x2d, h2d, wg, oa, ob, oc, wb, wo):
    m = x2d.shape[0]
    tm = min(TOKEN_TILE, m)
    bw = oa.shape[1]
    tok = lambda w: pl.BlockSpec((tm, w), lambda i: (i, 0))
    return pl.pallas_call(
        _merge_body,
        grid=(m // tm,),
        in_specs=[tok(D_MODEL), tok(D_MODEL), _const_spec(wg.shape),
                  tok(bw), tok(bw), tok(bw), _const_spec(wb.shape), _const_spec(wo.shape)],
        out_specs=tok(D_MODEL),
        out_shape=jax.ShapeDtypeStruct((m, D_MODEL), F32),
        compiler_params=pltpu.CompilerParams(
            dimension_semantics=("arbitrary",), vmem_limit_bytes=VMEM_LIMIT),
        name="merge",
    )(x2d, h2d, wg, oa, ob, oc, wb, wo)


def _rope_tables(pos0, l):
    half = RET_DK // 2
    inv = ROPE_BASE ** (-jnp.arange(half, dtype=F32) / half)
    ang = (pos0 + jnp.arange(l)).astype(F32)[:, None] * inv[None, :]
    cos = jnp.cos(ang)
    sin = jnp.sin(ang)
    cos = jnp.tile(jnp.concatenate([cos, cos], axis=1), (1, RET_HEADS))
    sin = jnp.tile(jnp.concatenate([-sin, sin], axis=1), (1, RET_HEADS))
    return cos, sin


def _prep_weights(p):
    w_in = p["w_in"]
    depth = w_in.shape[0]
    o = 0
    pieces = {}
    for name, size in (("a_qkvg", 2 * RET_QK + 2 * RET_W), ("b_qkv", GDN_CONV_CH), ("b_g", GDN_V),
                       ("b_beta", GDN_HEADS), ("b_alpha", GDN_HEADS), ("c_z", RWKV_SHIFT_CH),
                       ("gate", 3 * D_MODEL)):
        pieces[name] = w_in[:, :, o:o + size]
        o += size
    pad_l = lambda t, n: jnp.pad(t, ((0, 0), (0, 0), (0, n - t.shape[2])))
    row = lambda t: t[:, None, :]
    zeros64 = jnp.zeros((depth, 64, RWKV_W), F32)
    return dict(
        norm_ffn1=row(p["norm_ffn1"]), norm_mix=row(p["norm_mix"]), norm_ffn2=row(p["norm_ffn2"]),
        ffn1_w13=p["ffn1_w13"].astype(BF16), ffn1_w2=p["ffn1_w2"].astype(BF16),
        ffn2_w13=p["ffn2_w13"].astype(BF16), ffn2_w2=p["ffn2_w2"].astype(BF16),
        w_ret=pieces["a_qkvg"].astype(BF16),
        w_gdn=jnp.concatenate([pieces["b_qkv"], pieces["b_g"], pad_l(pieces["b_beta"], LANES),
                               pad_l(pieces["b_alpha"], LANES)], axis=2).astype(BF16),
        w_rwkv=pieces["c_z"].astype(BF16),
        w_gate=pieces["gate"].astype(BF16),
        ret_gn=row(p["ret_gn"]),
        gdn_conv=p["gdn_conv"],
        gdn_a_log=row(jnp.pad(p["gdn_a_log"], ((0, 0), (0, LANES - GDN_HEADS)))),
        gdn_dt_bias=row(jnp.pad(p["gdn_dt_bias"], ((0, 0), (0, LANES - GDN_HEADS)))),
        gdn_norm=row(jnp.tile(p["gdn_norm"], (1, GDN_HEADS))),
        rwkv_mu=row(p["rwkv_mu"]),
        rwkv_w0=row(p["rwkv_w0"]),
        rwkv_w2=jnp.concatenate([p["rwkv_w2"], zeros64], axis=1),
        rwkv_a0=row(p["rwkv_a0"]),
        rwkv_a2=jnp.concatenate([zeros64, p["rwkv_a2"]], axis=1),
        rwkv_g2=p["rwkv_g2"],
        rwkv_kk=row(p["rwkv_kk"]), rwkv_ka=row(p["rwkv_ka"]), rwkv_rk=row(p["rwkv_rk"]),
        rwkv_gn=row(p["rwkv_gn"]),
        w_branch=p["w_branch"].astype(BF16), w_out=p["w_out"].astype(BF16),
    )


def _layer(x, st, lp, tables, gf, final_norm, tb):
    b, l, d = x.shape
    s_ret, s_delta, s_conv, s_wkv, s_shift = st
    cos, sin = tables
    x2, h2 = _ffn(x.reshape(b * l, d), lp["norm_ffn1"], lp["ffn1_w13"], lp["ffn1_w2"],
                  lp["norm_mix"], "with_h")
    h3 = h2.reshape(b, l, d)
    oa, n_ret = _retention(h3, lp["w_ret"], cos, sin, lp["ret_gn"], s_ret, tb)
    ob, n_delta, n_conv = _gdn_pipe(h3, lp["w_gdn"], lp["gdn_conv"], lp["gdn_a_log"],
                               lp["gdn_dt_bias"], lp["gdn_norm"], s_delta, s_conv, tb)
    oc, n_wkv, n_shift = _rwkv_pipe(h3, lp["w_rwkv"], lp["rwkv_mu"], lp["rwkv_w0"],
                               lp["rwkv_w2"], lp["rwkv_a0"], lp["rwkv_a2"], lp["rwkv_g2"],
                               lp["rwkv_kk"], lp["rwkv_ka"], lp["rwkv_rk"], lp["rwkv_gn"],
                               s_wkv, s_shift, tb)
    bw = oa.shape[-1]
    x4 = _merge(x2, h2, lp["w_gate"], oa.reshape(b * l, bw), ob.reshape(b * l, bw),
                oc.reshape(b * l, bw), lp["w_branch"], lp["w_out"])
    (x5,) = _ffn(x4, lp["norm_ffn2"], lp["ffn2_w13"], lp["ffn2_w2"], gf,
                 "normed" if final_norm else "plain")
    return x5.reshape(b, l, d), (n_ret, n_delta, n_conv, n_wkv, n_shift)


def _run_stream(x, pos0, states, wts, gf, tb):
    depth = wts["w_ret"].shape[0]
    l = x.shape[1]
    tb = min(tb, l)
    tables = _rope_tables(pos0, l)
    new = [[] for _ in range(5)]
    for li in range(depth):
        lp = {k: v[li] for k, v in wts.items()}
        st = tuple(s[li] for s in states)
        x, nst = _layer(x, st, lp, tables, gf, li == depth - 1, tb)
        for jdx in range(5):
            new[jdx].append(nst[jdx])
    return x, tuple(jnp.stack(t, axis=0) for t in new)


def kernel(x_prompt, x_sample, state_ret, state_delta, state_conv, state_wkv, state_shift,
           norm_ffn1, ffn1_w13, ffn1_w2, norm_mix, w_in, ret_gn, gdn_conv, gdn_a_log, gdn_dt_bias,
           gdn_norm, rwkv_mu, rwkv_w0, rwkv_w2, rwkv_a0, rwkv_a2, rwkv_g2, rwkv_kk, rwkv_ka, rwkv_rk,
           rwkv_gn, w_branch, w_out, norm_ffn2, ffn2_w13, ffn2_w2, norm_final):
    params = dict(
        norm_ffn1=norm_ffn1, ffn1_w13=ffn1_w13, ffn1_w2=ffn1_w2, norm_mix=norm_mix, w_in=w_in,
        ret_gn=ret_gn, gdn_conv=gdn_conv, gdn_a_log=gdn_a_log, gdn_dt_bias=gdn_dt_bias,
        gdn_norm=gdn_norm, rwkv_mu=rwkv_mu, rwkv_w0=rwkv_w0, rwkv_w2=rwkv_w2, rwkv_a0=rwkv_a0,
        rwkv_a2=rwkv_a2, rwkv_g2=rwkv_g2, rwkv_kk=rwkv_kk, rwkv_ka=rwkv_ka, rwkv_rk=rwkv_rk,
        rwkv_gn=rwkv_gn, w_branch=w_branch, w_out=w_out, norm_ffn2=norm_ffn2, ffn2_w13=ffn2_w13,
        ffn2_w2=ffn2_w2)
    wts = _prep_weights(params)
    gf = norm_final[None, :]
    depth = w_in.shape[0]
    bp = x_prompt.shape[0]
    zero_states = (
        jnp.zeros((depth, bp, RET_HEADS, RET_DK, RET_DV), F32),
        jnp.zeros((depth, bp, GDN_HEADS, GDN_DK, GDN_DV), F32),
        jnp.zeros((depth, bp, CONV_W - 1, GDN_CONV_CH), F32),
        jnp.zeros((depth, bp, RWKV_HEADS, RWKV_N, RWKV_N), F32),
        jnp.zeros((depth, bp, 1, RWKV_SHIFT_CH), F32),
    )
    y_p, st_p = _run_stream(x_prompt, 0, zero_states, wts, gf, SEQ_BLOCK)
    y_s, st_s = _run_stream(x_sample, PAST_LEN, (state_ret, state_delta, state_conv, state_wkv,
                                                 state_shift), wts, gf, SEQ_BLOCK)
    return (y_p, y_s) + st_p + st_s
```

```python
import functools
import math

import jax
import jax.numpy as jnp
from jax import lax
from jax.experimental import pallas as pl
from jax.experimental.pallas import tpu as pltpu

F32 = jnp.float32
BF16 = jnp.bfloat16

D_MODEL = 1024
D_FF = 2816
EPS = 1e-6
PAST_LEN = 2048
RET_HEADS, RET_DK, RET_DV = 4, 64, 128
ROPE_BASE = 10000.0
GDN_HEADS, GDN_DK, GDN_DV, CONV_W = 8, 64, 64, 4
RWKV_HEADS, RWKV_N = 8, 64
RWKV_GN_EPS = 64e-5
RET_QK = RET_HEADS * RET_DK
RET_W = RET_HEADS * RET_DV
GDN_QK = GDN_HEADS * GDN_DK
GDN_V = GDN_HEADS * GDN_DV
GDN_CONV_CH = 2 * GDN_QK + GDN_V
RWKV_W = RWKV_HEADS * RWKV_N
RWKV_SHIFT_CH = 3 * RWKV_W + 256
LANES = 128
SUBLANES = 8
VMEM_LIMIT = 56 * 1024 * 1024
HEAD = 64
PAIRS = 4
CORE = 64
SEQ_BLOCK = 512
TOKEN_TILE = 512
assert GDN_DK == GDN_DV == RWKV_N == HEAD and GDN_HEADS == RWKV_HEADS == 2 * PAIRS


NT = (((1,), (1,)), ((), ()))
TN = (((0,), (0,)), ((), ()))
NN = (((1,), (0,)), ((), ()))


def _bf(x):
    return x.astype(BF16)


def _mm(a, b, dims=NN):
    return lax.dot_general(a, b, dims, preferred_element_type=F32)


def _dot1(a, b, dims=NN):
    return _mm(_bf(a), _bf(b), dims)


def _split3(x):
    x0 = _bf(x)
    r1 = x - x0.astype(F32)
    x1 = _bf(r1)
    x2 = _bf(r1 - x1.astype(F32))
    return x0, x1, x2


def _dot_exact_rhs(a, b_bf16):
    a0, a1, a2 = _split3(a)
    return _mm(a0, b_bf16) + (_mm(a1, b_bf16) + _mm(a2, b_bf16))


def _dot_exact_lhs(a_bf16, b):
    b0, b1, b2 = _split3(b)
    return _mm(a_bf16, b0) + (_mm(a_bf16, b1) + _mm(a_bf16, b2))


def _rms(x, g):
    return x * lax.rsqrt(jnp.mean(x * x, axis=-1, keepdims=True) + EPS) * g


def _sigmoid(x):
    return 0.5 + 0.5 * jnp.tanh(0.5 * x)


def _silu(x):
    return x * _sigmoid(x)


def _softplus(x):
    return jnp.maximum(x, 0.0) + jnp.log(1.0 + jnp.exp(-jnp.abs(x)))


def _iota2(shape, dim):
    return lax.broadcasted_iota(jnp.int32, shape, dim)


def _group_ones(width, group):
    r = _iota2((width, width), 0) // group
    c = _iota2((width, width), 1) // group
    return jnp.where(r == c, 1.0, 0.0).astype(BF16)


def _head_expand(width):
    r = _iota2((LANES, width), 0)
    c = _iota2((LANES, width), 1) // HEAD
    return jnp.where(r == c, 1.0, 0.0).astype(BF16)


def _chunk_tril(tb, chunk):
    r = _iota2((tb, tb), 0)
    c = _iota2((tb, tb), 1)
    return jnp.where(((r // chunk) == (c // chunk)) & (c <= r), 1.0, 0.0).astype(BF16)


def _pad_rows(x, rows):
    if x.shape[0] == rows:
        return x
    return jnp.concatenate([x, jnp.zeros((rows - x.shape[0], x.shape[1]), x.dtype)], axis=0)


def _shift_rows(x, prev, i):
    rolled = pltpu.roll(x, i, 0)
    first = jnp.where(_iota2((SUBLANES, x.shape[1]), 0) < i, pltpu.roll(prev, i, 0),
                      rolled[:SUBLANES])
    return jnp.concatenate([first, rolled[SUBLANES:]], axis=0)


def _bd(y, left):
    z = jnp.zeros_like(y)
    return jnp.concatenate([jnp.where(left, y, z), jnp.where(left, z, y)], axis=0)


def _diag(r, left):
    return jnp.where(left, r[:HEAD], r[HEAD:])


BASE = 16


def _bd8(y):
    g = _iota2((BASE, LANES), 1) // BASE
    z = jnp.zeros_like(y)
    return jnp.concatenate([jnp.where(g == i, y, z) for i in range(LANES // BASE)], axis=0)


def _tri_inv_sbs(a_list, left, tick=lambda: None):
    r = _iota2((CORE, LANES), 0)
    c = _iota2((CORE, LANES), 1) % HEAD
    nb = CORE // BASE
    lane16 = _iota2((BASE, LANES), 1)
    blk_of_lane = (lane16 % HEAD) // BASE
    eye16 = jnp.where(_iota2((BASE, LANES), 0) == lane16 % BASE, 1.0, 0.0).astype(F32)

    def diag_blocks(a):
        d = a[(nb - 1) * BASE:]
        for b in range(nb - 2, -1, -1):
            d = jnp.where(blk_of_lane == b, a[b * BASE:(b + 1) * BASE], d)
        return d

    d0f = [diag_blocks(a) for a in a_list]
    d0 = [_bf(x) for x in d0f]
    pw = [_bf(_mm(x, _bd8(x))) for x in d0]
    tick()
    m = [eye16 - x for x in d0f]
    for _ in range(2):
        bdp = [_bd8(x) for x in pw]
        both = [_mm(jnp.concatenate([x, _bf(mi)], axis=0), y) for x, mi, y in zip(pw, m, bdp)]
        tick()
        pw = [_bf(t[:BASE]) for t in both]
        m = [mi + t[BASE:] for mi, t in zip(m, both)]
    m = [mi + _mm(_bf(mi), _bd8(x)) for mi, x in zip(m, pw)]
    tick()
    m = [jnp.concatenate([jnp.where(blk_of_lane == b, mi, 0.0) for b in range(nb)], axis=0) for mi in m]
    size = 2 * BASE
    while size <= CORE:
        half = size // 2
        nh = CORE // half
        sel = ((r // size) == (c // size)) & ((r // half) != (c // half))
        bdb = [_bd(_bf(jnp.where(sel, a, 0.0)), left) for a in a_list]
        mb = [_bf(mi) for mi in m]
        low = [jnp.concatenate([x[b * half:(b + 1) * half] for b in range(1, nh, 2)], axis=0) for x in mb]
        t = [_bf(_mm(x, y)) for x, y in zip(low, bdb)]
        tick()
        upd = [_mm(x, _bd(y, left)) for x, y in zip(t, mb)]
        tick()
        m = [jnp.concatenate([mi[b * half:(b + 1) * half] - u[(b // 2) * half:(b // 2 + 1) * half]
                              if b % 2 else mi[b * half:(b + 1) * half] for b in range(nh)], axis=0)
             for mi, u in zip(m, upd)]
        size *= 2
    return m


def _const_spec(shape):
    nd = len(shape)
    return pl.BlockSpec(shape, lambda *_: (0,) * nd, pipeline_mode=pl.Buffered(1))


def _ffn_body(x_ref, g_ref, w13_ref, w2_ref, gn_ref, *o_refs, tail):
    x = x_ref[...]
    h = _rms(x, g_ref[...]).astype(BF16)
    ab = jnp.dot(h, w13_ref[...], preferred_element_type=F32)
    act = (_silu(ab[:, :D_FF]) * ab[:, D_FF:]).astype(BF16)
    y = x + 0.5 * jnp.dot(act, w2_ref[...], preferred_element_type=F32)
    if tail == "normed":
        o_refs[0][...] = _rms(y, gn_ref[...])
    else:
        o_refs[0][...] = y
    if tail == "with_h":
        o_refs[1][...] = _rms(y, gn_ref[...]).astype(BF16)


def _ffn(x2d, g, w13, w2, gn, tail):
    m = x2d.shape[0]
    tm = min(TOKEN_TILE, m)
    tok = pl.BlockSpec((tm, D_MODEL), lambda i: (i, 0))
    out_specs = [tok]
    out_shape = [jax.ShapeDtypeStruct((m, D_MODEL), F32)]
    if tail == "with_h":
        out_specs.append(tok)
        out_shape.append(jax.ShapeDtypeStruct((m, D_MODEL), BF16))
    return pl.pallas_call(
        functools.partial(_ffn_body, tail=tail),
        grid=(m // tm,),
        in_specs=[
            tok,
            _const_spec((1, D_MODEL)),
            _const_spec((D_MODEL, 2 * D_FF)),
            _const_spec((D_FF, D_MODEL)),
            _const_spec((1, D_MODEL)),
        ],
        out_specs=out_specs,
        out_shape=out_shape,
        compiler_params=pltpu.CompilerParams(
            dimension_semantics=("arbitrary",), vmem_limit_bytes=VMEM_LIMIT),
        name="ffn",
    )(x2d, g, w13, w2, gn)


RET_PAIRS = RET_HEADS // 2
RET_LOG_G = [math.log1p(-(2.0 ** (-5.0 - hh))) for hh in range(RET_HEADS)]


def _ret_body(h_ref, w_ref, cos_ref, sin_ref, gn_ref, s0_ref,
              o_ref, sn_ref, s_scr, q_scr, k_scr, v_scr, o_scr, *, tb):
    j = pl.program_id(1)
    tbp = max(tb, CORE)
    nc = tbp // CORE
    grp = _chunk_groups(nc)
    valid = min(tb, CORE)
    zero_blk = jnp.zeros((RET_DK, RET_DV), F32)

    @pl.when(j == 0)
    def _():
        for p in range(RET_PAIRS):
            s_scr[p] = jnp.concatenate(
                [jnp.concatenate([s0_ref[2 * p], zero_blk], axis=1),
                 jnp.concatenate([zero_blk, s0_ref[2 * p + 1]], axis=1)], axis=0)

    z = jnp.dot(h_ref[...], w_ref[...], preferred_element_type=F32)
    q = z[:, :RET_QK]
    k = z[:, RET_QK:2 * RET_QK]
    cos = cos_ref[...]
    sin = sin_ref[...]
    first = (_iota2((tb, RET_QK), 1) % RET_DK) < (RET_DK // 2)

    def rot(t):
        swapped = jnp.where(first, pltpu.roll(t, RET_QK - RET_DK // 2, 1),
                            pltpu.roll(t, RET_DK // 2, 1))
        return t * cos + swapped * sin

    q_scr[...] = _pad_rows(rot(q), tbp)
    k_scr[...] = _pad_rows(rot(k) * (RET_DK ** -0.5), tbp)
    v_scr[...] = _pad_rows(z[:, 2 * RET_QK:2 * RET_QK + RET_W], tbp)

    left, incl, _ = _core_masks()
    ri = _iota2((CORE, LANES), 0)
    ci = _iota2((CORE, LANES), 1) % HEAD
    rf = ri.astype(F32)
    dfi = jnp.where(incl, (ri - ci).astype(F32), 0.0)
    row2 = _iota2((2 * RET_DK, 2 * RET_DV), 0) < RET_DK
    col2 = _iota2((2 * RET_DK, 2 * RET_DV), 1) < RET_DV
    blockmask = row2 == col2
    vtop = _iota2((CORE, 2 * RET_DV), 1) < RET_DV
    dmask, qdec, kdec, cdec = [], [], [], []
    for p in range(RET_PAIRS):
        lg = jnp.where(left, RET_LOG_G[2 * p], RET_LOG_G[2 * p + 1])
        dmask.append(jnp.where(incl, jnp.exp(dfi * lg), 0.0))
        qdec.append(jnp.exp((rf + 1.0) * lg))
        kdec.append(jnp.exp((valid - 1.0 - rf) * lg))
        cdec.append(jnp.where(row2, math.exp(valid * RET_LOG_G[2 * p]),
                              math.exp(valid * RET_LOG_G[2 * p + 1])))
    psl = [slice(p * LANES, (p + 1) * LANES) for p in range(RET_PAIRS)]
    vsl = [slice(p * 2 * RET_DV, (p + 1) * 2 * RET_DV) for p in range(RET_PAIRS)]

    def body(i, carry):
        chunks = []
        for gi in range(grp):
            rows = pl.ds(pl.multiple_of((i * grp + gi) * CORE, CORE), CORE)
            chunks.append((rows, q_scr[rows, :], k_scr[rows, :], v_scr[rows, :]))
        items = [(ch, p) for ch in chunks for p in range(RET_PAIRS)]
        scores = [_mm(_bf(qc[:, psl[p]]), _bd(_bf(kc[:, psl[p]]), left), NT) * dmask[p]
                  for (_, qc, kc, _), p in items]
        kv = [_mm(_bf(kc[:, psl[p]] * kdec[p]), _bf(vc[:, vsl[p]]), TN)
              for (_, _, kc, vc), p in items]
        state = [s_scr[p] for p in range(RET_PAIRS)]
        for n, ((rows, qc, _, vc), p) in enumerate(items):
            vp = _bf(vc[:, vsl[p]])
            zv = jnp.zeros_like(vp)
            rhs = jnp.concatenate([jnp.where(vtop, vp, zv), jnp.where(vtop, zv, vp), _bf(state[p])],
                                  axis=0)
            lhs = _bf(jnp.concatenate([scores[n], qc[:, psl[p]] * qdec[p]], axis=1))
            o_scr[rows, vsl[p]] = _mm(lhs, rhs)
            state[p] = cdec[p] * state[p] + jnp.where(blockmask, kv[n], 0.0)
        for p in range(RET_PAIRS):
            s_scr[p] = state[p]
        return carry

    lax.fori_loop(0, nc // grp, body, 0)

    gate = z[:, 2 * RET_QK + RET_W:]
    gn = gn_ref[...]
    for hh in range(RET_HEADS):
        sl = slice(hh * RET_DV, (hh + 1) * RET_DV)
        oh = o_scr[pl.ds(0, tb), sl]
        mu = jnp.mean(oh, axis=-1, keepdims=True)
        dlt = oh - mu
        var = jnp.mean(dlt * dlt, axis=-1, keepdims=True)
        y = dlt * lax.rsqrt(var + 1e-5) * gn[:, sl]
        o_ref[:, sl] = (y * _silu(gate[:, sl])).astype(BF16)

    @pl.when(j == pl.num_programs(1) - 1)
    def _():
        for p in range(RET_PAIRS):
            s = s_scr[p]
            sn_ref[2 * p] = s[:RET_DK, :RET_DV]
            sn_ref[2 * p + 1] = s[RET_DK:, RET_DV:]


def _retention(h, w, cos, sin, gn, s0, tb):
    b, l, _ = h.shape
    st = (RET_HEADS, RET_DK, RET_DV)
    tbp = max(tb, CORE)
    return pl.pallas_call(
        functools.partial(_ret_body, tb=tb),
        grid=(b, l // tb),
        in_specs=[
            pl.BlockSpec((None, tb, D_MODEL), lambda i, j: (i, j, 0)),
            _const_spec(w.shape),
            pl.BlockSpec((tb, RET_QK), lambda i, j: (j, 0)),
            pl.BlockSpec((tb, RET_QK), lambda i, j: (j, 0)),
            _const_spec((1, RET_W)),
            pl.BlockSpec((None,) + st, lambda i, j: (i, 0, 0, 0)),
        ],
        out_specs=[
            pl.BlockSpec((None, tb, RET_W), lambda i, j: (i, j, 0)),
            pl.BlockSpec((None,) + st, lambda i, j: (i, 0, 0, 0)),
        ],
        out_shape=[jax.ShapeDtypeStruct((b, l, RET_W), BF16),
                   jax.ShapeDtypeStruct((b,) + st, F32)],
        scratch_shapes=[
            pltpu.VMEM((RET_PAIRS, 2 * RET_DK, 2 * RET_DV), F32),
            pltpu.VMEM((tbp, RET_QK), F32),
            pltpu.VMEM((tbp, RET_QK), F32),
            pltpu.VMEM((tbp, RET_W), F32),
            pltpu.VMEM((tbp, RET_W), F32),
        ],
        compiler_params=pltpu.CompilerParams(
            dimension_semantics=("arbitrary", "arbitrary"), vmem_limit_bytes=VMEM_LIMIT),
        name="retention",
    )(h, w, cos, sin, gn, s0)


HALO = SUBLANES


def _core_masks():
    r = _iota2((CORE, LANES), 0)
    c = _iota2((CORE, LANES), 1) % HEAD
    left = _iota2((CORE, LANES), 1) < HEAD
    return left, r >= c, r > c


def _load_state_sbs(s0_ref, s_scr):
    for p in range(PAIRS):
        s_scr[p] = jnp.concatenate([s0_ref[2 * p], s0_ref[2 * p + 1]], axis=1)


def _store_state_sbs(s_scr, sn_ref):
    for p in range(PAIRS):
        s = s_scr[p]
        sn_ref[2 * p] = s[:, :HEAD]
        sn_ref[2 * p + 1] = s[:, HEAD:]


def _chunk_groups(nc):
    return 8 if nc % 8 == 0 else (4 if nc % 4 == 0 else (2 if nc % 2 == 0 else 1))


SLAB = 128


def _pipe_maps(nblk, total):
    def in_blk(t):
        ta = jnp.minimum(t, total - 1)
        return ta // nblk, ta % nblk

    def out_blk(t):
        tp = jnp.maximum(t - 1, 0)
        return tp // nblk, tp % nblk

    return in_blk, out_blk


def _gdn_pipe_body(h_ref, w_ref, cw_ref, alog_ref, dtb_ref, nrm_ref, s0_ref, c0_ref,
                   o_ref, sn_ref, cn_ref,
                   s_scr, xp_scr, q_scr, k_scr, v_scr, be_scr, ga_scr, gate_scr, gar_scr,
                   o_scr, m_scr, n_scr, qp_scr, egl_scr, *, tb, nblk):
    t = pl.program_id(0)
    total = pl.num_programs(0) - 1
    ja = jnp.minimum(t, total - 1) % nblk
    jp = jnp.maximum(t - 1, 0) % nblk
    wslot = t % 2
    rslot = 1 - wslot
    tbp = max(tb, CORE)
    nc = tbp // CORE
    slab = min(SLAB, tb)
    slabp = max(slab, CORE)
    nslab = tb // slab

    @pl.when(t == 0)
    def _():
        for ref in (q_scr, k_scr, v_scr, be_scr, ga_scr, gate_scr, gar_scr):
            ref[...] = jnp.zeros_like(ref)

    @pl.when(ja == 0)
    def _():
        xp_scr[...] = jnp.zeros_like(xp_scr)
        xp_scr[pl.ds(HALO - (CONV_W - 1), CONV_W - 1), :] = c0_ref[...]

    @pl.when(jp == 0)
    def _():
        _load_state_sbs(s0_ref, s_scr)

    gones = _group_ones(GDN_QK, GDN_DK)
    expand = _head_expand(GDN_V)
    tril = _chunk_tril(slabp, CORE)
    ncp = gar_scr.shape[1]
    rr = _iota2((slabp, GDN_V), 0)
    cc = _iota2((slabp, GDN_V), 1)
    on_diag = (rr % CORE) == (cc % HEAD)

    def input_stage():
        prev = xp_scr[...]
        garv = jnp.zeros((ncp, GDN_V), F32)
        for s in range(nslab):
            z = jnp.dot(h_ref[pl.ds(s * slab, slab), :], w_ref[...], preferred_element_type=F32)
            yield
            zq = z[:, :GDN_CONV_CH]
            cw = cw_ref[...]
            y = zq * cw[CONV_W - 1:CONV_W, :]
            for i in range(1, CONV_W):
                y = y + _shift_rows(zq, prev, i) * cw[CONV_W - 1 - i:CONV_W - i, :]
            prev = zq[slab - HALO:, :]
            yield
            qkv = _silu(y)
            q = qkv[:, :GDN_QK]
            k = qkv[:, GDN_QK:2 * GDN_QK]
            q = _pad_rows(q * lax.rsqrt(_dot1(q * q, gones) + 1e-6) * (GDN_DK ** -0.5), slabp)
            k = _pad_rows(k * lax.rsqrt(_dot1(k * k, gones) + 1e-6), slabp)
            v = _pad_rows(qkv[:, 2 * GDN_QK:], slabp)
            yield
            zb = z[:, GDN_CONV_CH + GDN_V:GDN_CONV_CH + GDN_V + LANES]
            za = z[:, GDN_CONV_CH + GDN_V + LANES:]
            beta = _pad_rows(_sigmoid(zb), slabp)
            log_a = _pad_rows(-jnp.exp(alog_ref[...]) * _softplus(za + dtb_ref[...]), slabp)
            ga = _dot_exact_lhs(tril, log_a)
            gae = _dot_exact_rhs(ga, expand)
            bee = _dot_exact_rhs(beta, expand)
            pick = jnp.where(on_diag, gae, 0.0)
            chunk_of = jnp.where(_iota2((ncp, slabp), 0)
                                 == _iota2((ncp, slabp), 1) // CORE + s * (slabp // CORE), 1.0, 0.0)
            garv = garv + _dot_exact_lhs(_bf(chunk_of), pick)
            yield
            rows = pl.ds(s * slabp, slabp)
            q_scr[wslot, rows, :] = q
            k_scr[wslot, rows, :] = k
            v_scr[wslot, rows, :] = v
            be_scr[wslot, rows, :] = bee
            ga_scr[wslot, rows, :] = gae
            gate_scr[wslot, pl.ds(s * slab, slab), :] = z[:, GDN_CONV_CH:GDN_CONV_CH + GDN_V]
            yield
        xp_scr[...] = prev
        gar_scr[wslot] = garv
        yield

    gen = input_stage()

    def tick():
        next(gen, None)

    left, incl, strict = _core_masks()
    incl4 = jnp.concatenate([incl] * PAIRS, axis=1)
    psl = [slice(p * LANES, (p + 1) * LANES) for p in range(PAIRS)]

    def phase1(chunks, tick):
        probs = []
        for c in chunks:
            rows = pl.ds(c * CORE, CORE)
            gac = ga_scr[rslot, rows, :]
            glast = gac[CORE - 1:CORE, :]
            egl_scr[pl.ds(c, 1), :] = jnp.exp(glast)
            eg = jnp.exp(gac)
            gam = jnp.where(incl4, jnp.exp(jnp.where(incl4, gac - gar_scr[rslot, pl.ds(c, 1), :], 0.0)),
                            0.0)
            qc = q_scr[rslot, rows, :]
            kc = k_scr[rslot, rows, :]
            bec = be_scr[rslot, rows, :]
            probs.append(dict(rows=rows, gam=gam, be=bec, q=_bf(qc), k=_bf(kc), qe=qc * eg,
                              kd=_bf(kc * jnp.exp(glast - gac)), bv=_bf(bec * v_scr[rslot, rows, :]),
                              bek=_bf(bec * eg * kc)))
        tick()
        items = [(pr, s) for pr in probs for s in psl]
        qk = [_mm(jnp.concatenate([pr["q"][:, s], pr["k"][:, s]], axis=0), _bd(pr["k"][:, s], left), NT)
              for pr, s in items]
        tick()
        pm = [_bf(x[:CORE] * pr["gam"][:, s]) for x, (pr, s) in zip(qk, items)]
        amat = [jnp.where(strict, pr["be"][:, s] * x[CORE:] * pr["gam"][:, s], 0.0)
                for x, (pr, s) in zip(qk, items)]
        tinv = _tri_inv_sbs(amat, left, tick)
        sol = [_bf(_mm(_bf(tm), jnp.concatenate([_bd(pr["bv"][:, s], left), _bd(pr["bek"][:, s], left)],
                                                axis=1)))
               for tm, (pr, s) in zip(tinv, items)]
        tick()
        kts = [_mm(pr["kd"][:, s], x, TN) for x, (pr, s) in zip(sol, items)]
        tick()
        ps = [_mm(p_, jnp.concatenate([_bd(x[:, :LANES], left), _bd(x[:, LANES:], left)], axis=1))
              for p_, x in zip(pm, sol)]
        tick()
        for n, (pr, s) in enumerate(items):
            n_scr[pr["rows"], s] = _diag(kts[n][:, :LANES], left)
            m_scr[pr["rows"], s] = _diag(kts[n][:, LANES:], left)
            o_scr[pr["rows"], s] = ps[n][:, :LANES]
            qp_scr[pr["rows"], s] = pr["qe"][:, s] - ps[n][:, LANES:]

    phase1(range(nc), tick)

    for c in range(nc):
        rows = pl.ds(c * CORE, CORE)
        mc = _bf(m_scr[rows, :])
        nn = n_scr[rows, :]
        qpc = _bf(qp_scr[rows, :])
        eglc = egl_scr[pl.ds(c, 1), :]
        s_old = [s_scr[p] for p in range(PAIRS)]
        res = [_mm(jnp.concatenate([mc[:, psl[p]], qpc[:, psl[p]]], axis=0), _bd(_bf(s_old[p]), left))
               for p in range(PAIRS)]
        for p in range(PAIRS):
            s_scr[p] = eglc[:, psl[p]] * s_old[p] - res[p][:CORE] + nn[:, psl[p]]
        o_scr[rows, :] = o_scr[rows, :] + jnp.concatenate([x[CORE:] for x in res], axis=1)
        tick()
    for _ in gen:
        pass

    o = o_scr[pl.ds(0, tb), :]
    ms = _dot1(o * o, gones) * (1.0 / GDN_DV)
    gate = gate_scr[rslot]
    o_ref[...] = (o * lax.rsqrt(ms + EPS) * nrm_ref[...] * _silu(gate)).astype(BF16)

    @pl.when((jp == nblk - 1) & (t > 0))
    def _():
        _store_state_sbs(s_scr, sn_ref)

    @pl.when(ja == nblk - 1)
    def _():
        cn_ref[...] = xp_scr[pl.ds(HALO - (CONV_W - 1), CONV_W - 1), :]


def _gdn_pipe(h, w, cw, alog, dtb, nrm, s0, c0, tb):
    b, l, _ = h.shape
    nblk = l // tb
    total = b * nblk
    in_blk, out_blk = _pipe_maps(nblk, total)
    st = (GDN_HEADS, GDN_DK, GDN_DV)
    cst = (CONV_W - 1, GDN_CONV_CH)
    tbp = max(tb, CORE)
    dbl = pltpu.VMEM((2, tbp, GDN_V), F32)
    big = pltpu.VMEM((tbp, GDN_V), F32)
    ncp = max(tbp // CORE, SUBLANES)
    return pl.pallas_call(
        functools.partial(_gdn_pipe_body, tb=tb, nblk=nblk),
        grid=(total + 1,),
        in_specs=[
            pl.BlockSpec((None, tb, D_MODEL), lambda t: in_blk(t) + (0,)),
            _const_spec(w.shape),
            _const_spec(cw.shape),
            _const_spec((1, LANES)),
            _const_spec((1, LANES)),
            _const_spec((1, GDN_V)),
            pl.BlockSpec((None,) + st, lambda t: (out_blk(t)[0], 0, 0, 0)),
            pl.BlockSpec((None,) + cst, lambda t: (in_blk(t)[0], 0, 0)),
        ],
        out_specs=[
            pl.BlockSpec((None, tb, GDN_V), lambda t: out_blk(t) + (0,)),
            pl.BlockSpec((None,) + st, lambda t: (out_blk(t)[0], 0, 0, 0)),
            pl.BlockSpec((None,) + cst, lambda t: (in_blk(t)[0], 0, 0)),
        ],
        out_shape=[jax.ShapeDtypeStruct((b, l, GDN_V), BF16),
                   jax.ShapeDtypeStruct((b,) + st, F32),
                   jax.ShapeDtypeStruct((b,) + cst, F32)],
        scratch_shapes=[
            pltpu.VMEM((PAIRS, HEAD, LANES), F32),
            pltpu.VMEM((HALO, GDN_CONV_CH), F32),
            dbl, dbl, dbl, dbl, dbl,
            pltpu.VMEM((2, tb, GDN_V), F32),
            pltpu.VMEM((2, ncp, GDN_V), F32),
            big, big, big, big,
            pltpu.VMEM((ncp, GDN_V), F32),
        ],
        compiler_params=pltpu.CompilerParams(
            dimension_semantics=("arbitrary",), vmem_limit_bytes=VMEM_LIMIT),
        name="gdn",
    )(h, w, cw, alog, dtb, nrm, s0, c0)


def _rwkv_pipe_body(h_ref, w_ref, mu_ref, w0_ref, w2_ref, a0_ref, a2_ref, g2_ref,
                    kkw_ref, ka_ref, rk_ref, gn_ref, s0_ref, sh0_ref,
                    o_ref, sn_ref, shn_ref,
                    s_scr, zc_scr, r_scr, k2_scr, v_scr, kk_scr, kka_scr, lw_scr, cum_scr,
                    bonus_scr, g_scr, o_scr, qp_scr, m_scr, n_scr, wt_scr, *, tb, nblk):
    t = pl.program_id(0)
    total = pl.num_programs(0) - 1
    ja = jnp.minimum(t, total - 1) % nblk
    jp = jnp.maximum(t - 1, 0) % nblk
    wslot = t % 2
    rslot = 1 - wslot
    tbp = max(tb, CORE)
    nc = tbp // CORE
    slab = min(SLAB, tb)
    slabp = max(slab, CORE)
    nslab = tb // slab

    @pl.when(t == 0)
    def _():
        for ref in (r_scr, k2_scr, v_scr, kk_scr, kka_scr, lw_scr, cum_scr, bonus_scr, g_scr):
            ref[...] = jnp.zeros_like(ref)

    @pl.when(ja == 0)
    def _():
        zc_scr[...] = jnp.zeros_like(zc_scr)
        zc_scr[pl.ds(HALO - 1, 1), :] = sh0_ref[...]

    @pl.when(jp == 0)
    def _():
        _load_state_sbs(s0_ref, s_scr)

    gones = _group_ones(RWKV_W, RWKV_N)
    tril = _chunk_tril(slabp, CORE)

    def input_stage():
        prev = zc_scr[...]
        for s in range(nslab):
            z = jnp.dot(h_ref[pl.ds(s * slab, slab), :], w_ref[...], preferred_element_type=F32)
            yield
            cs = z + (_shift_rows(z, prev, 1) - z) * mu_ref[...]
            prev = z[slab - HALO:, :]
            yield
            r = cs[:, :RWKV_W]
            k = cs[:, RWKV_W:2 * RWKV_W]
            v = cs[:, 2 * RWKV_W:3 * RWKV_W]
            zwa = cs[:, 3 * RWKV_W:3 * RWKV_W + LANES]
            zg = cs[:, 3 * RWKV_W + LANES:]
            w_raw = -_softplus(-(w0_ref[...] + _dot1(jnp.tanh(zwa), w2_ref[...]))) - 0.5
            lw = _pad_rows(-jnp.exp(w_raw), slabp)
            a = _sigmoid(a0_ref[...] + _dot1(zwa, a2_ref[...]))
            g = _dot1(_sigmoid(zg), g2_ref[...])
            yield
            kkn = k * kkw_ref[...]
            kkn = kkn * lax.rsqrt(_dot1(kkn * kkn, gones) + 1e-6)
            k2 = k * (1.0 + (a - 1.0) * ka_ref[...])
            bonus = _dot1(r * k2 * rk_ref[...], gones) * v
            yield
            rows = pl.ds(s * slabp, slabp)
            r_scr[wslot, rows, :] = _pad_rows(r, slabp)
            k2_scr[wslot, rows, :] = _pad_rows(k2, slabp)
            v_scr[wslot, rows, :] = _pad_rows(v, slabp)
            kk_scr[wslot, rows, :] = _pad_rows(kkn, slabp)
            kka_scr[wslot, rows, :] = _pad_rows(kkn * a, slabp)
            lw_scr[wslot, rows, :] = lw
            cum_scr[wslot, rows, :] = _dot_exact_lhs(tril, lw)
            bonus_scr[wslot, pl.ds(s * slab, slab), :] = bonus
            g_scr[wslot, pl.ds(s * slab, slab), :] = g
            yield
        zc_scr[...] = prev
        yield

    gen = input_stage()

    def tick():
        next(gen, None)

    left, incl, strict = _core_masks()
    psl = [slice(p * LANES, (p + 1) * LANES) for p in range(PAIRS)]

    probs = []
    for c in range(nc):
        rows = pl.ds(c * CORE, CORE)
        cum = cum_scr[rslot, rows, :]
        tot = cum[CORE - 1:CORE, :]
        wt_scr[pl.ds(c, 1), :] = jnp.exp(tot)
        w_inv = jnp.exp(-cum)
        w_rem = jnp.exp(tot - cum)
        kkac = kka_scr[rslot, rows, :]
        k2c = k2_scr[rslot, rows, :]
        rt = r_scr[rslot, rows, :] * jnp.exp(cum)
        probs.append(dict(rows=rows, rt=rt, rtb=_bf(rt),
                          at=_bf(-kk_scr[rslot, rows, :] * jnp.exp(cum - lw_scr[rslot, rows, :])),
                          bt=_bf(kkac * w_inv), kt=_bf(k2c * w_inv), bh=_bf(kkac * w_rem),
                          kh=_bf(k2c * w_rem), v=_bf(v_scr[rslot, rows, :])))
    tick()
    items = [(pr, s) for pr in probs for s in psl]
    x = [_mm(jnp.concatenate([pr["at"][:, s], pr["rtb"][:, s]], axis=0),
             jnp.concatenate([_bd(pr["bt"][:, s], left), _bd(pr["kt"][:, s], left)], axis=0), NT)
         for pr, s in items]
    tick()
    neg_ab = [jnp.where(strict, -tt[:CORE, :LANES], 0.0) for tt in x]
    a_rb = [_bf(jnp.where(incl, tt[CORE:, :LANES], 0.0)) for tt in x]
    a_ak = [_bf(jnp.where(strict, tt[:CORE, LANES:], 0.0)) for tt in x]
    a_rk = [_bf(jnp.where(incl, tt[CORE:, LANES:], 0.0)) for tt in x]
    tinv = _tri_inv_sbs(neg_ab, left, tick)
    vbd = [_bd(pr["v"][:, s], left) for pr, s in items]
    akv = [_bf(_mm(p_, q_)) for p_, q_ in zip(a_ak, vbd)]
    tick()
    pu = [_bf(_mm(_bf(tm), jnp.concatenate([_bd(pr["at"][:, s], left), _bd(u, left)], axis=1)))
          for tm, u, (pr, s) in zip(tinv, akv, items)]
    tick()
    rbp = [_mm(p_, jnp.concatenate([_bd(u[:, :LANES], left), _bd(u[:, LANES:], left)], axis=1))
           for p_, u in zip(a_rb, pu)]
    tick()
    rkv = [_mm(p_, q_) for p_, q_ in zip(a_rk, vbd)]
    tick()
    bp = [_mm(u, pr["bh"][:, s], TN) for u, (pr, s) in zip(pu, items)]
    tick()
    vk = [_mm(pr["v"][:, s], pr["kh"][:, s], TN) for pr, s in items]
    tick()
    for n, (pr, s) in enumerate(items):
        qp_scr[pr["rows"], s] = pr["rt"][:, s] + rbp[n][:, :LANES]
        o_scr[pr["rows"], s] = rbp[n][:, LANES:] + rkv[n]
        m_scr[pr["rows"], s] = _diag(bp[n][:LANES], left)
        n_scr[pr["rows"], s] = _diag(bp[n][LANES:], left) + _diag(vk[n], left)

    for c in range(nc):
        rows = pl.ds(c * CORE, CORE)
        mc = _bf(m_scr[rows, :])
        nn = n_scr[rows, :]
        qpc = _bf(qp_scr[rows, :])
        wtc = wt_scr[pl.ds(c, 1), :]
        s_old = [s_scr[p] for p in range(PAIRS)]
        sb = [_bf(s) for s in s_old]
        od = [_mm(qpc[:, psl[p]], _bd(sb[p], left), NT) for p in range(PAIRS)]
        sm = [_mm(sb[p], _bd(mc[:, psl[p]], left)) for p in range(PAIRS)]
        for p in range(PAIRS):
            s_scr[p] = s_old[p] * wtc[:, psl[p]] + sm[p] + nn[:, psl[p]]
        o_scr[rows, :] = o_scr[rows, :] + jnp.concatenate(od, axis=1)
        tick()
    for _ in gen:
        pass

    o = o_scr[pl.ds(0, tb), :]
    mean = _dot1(o, gones) * (1.0 / RWKV_N)
    dlt = o - mean
    var = _dot1(dlt * dlt, gones) * (1.0 / RWKV_N)
    y = dlt * lax.rsqrt(var + RWKV_GN_EPS) * gn_ref[...]
    o_ref[...] = ((y + bonus_scr[rslot]) * g_scr[rslot]).astype(BF16)

    @pl.when((jp == nblk - 1) & (t > 0))
    def _():
        _store_state_sbs(s_scr, sn_ref)

    @pl.when(ja == nblk - 1)
    def _():
        shn_ref[...] = zc_scr[pl.ds(HALO - 1, 1), :]


def _rwkv_pipe(h, w, mu, w0, w2, a0, a2, g2, kkw, ka, rk, gn, s0, sh0, tb):
    b, l, _ = h.shape
    nblk = l // tb
    total = b * nblk
    in_blk, out_blk = _pipe_maps(nblk, total)
    st = (RWKV_HEADS, RWKV_N, RWKV_N)
    sst = (1, RWKV_SHIFT_CH)
    tbp = max(tb, CORE)
    row = _const_spec((1, RWKV_W))
    dbl = pltpu.VMEM((2, tbp, RWKV_W), F32)
    dbl_tb = pltpu.VMEM((2, tb, RWKV_W), F32)
    big = pltpu.VMEM((tbp, RWKV_W), F32)
    return pl.pallas_call(
        functools.partial(_rwkv_pipe_body, tb=tb, nblk=nblk),
        grid=(total + 1,),
        in_specs=[
            pl.BlockSpec((None, tb, D_MODEL), lambda t: in_blk(t) + (0,)),
            _const_spec(w.shape),
            _const_spec((1, RWKV_SHIFT_CH)),
            row, _const_spec((LANES, RWKV_W)), row, _const_spec((LANES, RWKV_W)),
            _const_spec((LANES, RWKV_W)),
            row, row, row, row,
            pl.BlockSpec((None,) + st, lambda t: (out_blk(t)[0], 0, 0, 0)),
            pl.BlockSpec((None,) + sst, lambda t: (in_blk(t)[0], 0, 0)),
        ],
        out_specs=[
            pl.BlockSpec((None, tb, RWKV_W), lambda t: out_blk(t) + (0,)),
            pl.BlockSpec((None,) + st, lambda t: (out_blk(t)[0], 0, 0, 0)),
            pl.BlockSpec((None,) + sst, lambda t: (in_blk(t)[0], 0, 0)),
        ],
        out_shape=[jax.ShapeDtypeStruct((b, l, RWKV_W), BF16),
                   jax.ShapeDtypeStruct((b,) + st, F32),
                   jax.ShapeDtypeStruct((b,) + sst, F32)],
        scratch_shapes=[
            pltpu.VMEM((PAIRS, HEAD, LANES), F32),
            pltpu.VMEM((HALO, RWKV_SHIFT_CH), F32),
            dbl, dbl, dbl, dbl, dbl, dbl, dbl,
            dbl_tb, dbl_tb,
            big, big, big, big,
            pltpu.VMEM((max(tbp // CORE, SUBLANES), RWKV_W), F32),
        ],
        compiler_params=pltpu.CompilerParams(
            dimension_semantics=("arbitrary",), vmem_limit_bytes=VMEM_LIMIT),
        name="rwkv7",
    )(h, w, mu, w0, w2, a0, a2, g2, kkw, ka, rk, gn, s0, sh0)


def _merge_body(x_ref, h_ref, wg_ref, oa_ref, ob_ref, oc_ref, wb_ref, wo_ref, o_ref):
    x = x_ref[...]
    gate = jnp.dot(h_ref[...], wg_ref[...], preferred_element_type=F32)
    m = None
    for i, br in enumerate((oa_ref, ob_ref, oc_ref)):
        t = _sigmoid(gate[:, i * D_MODEL:(i + 1) * D_MODEL]) * jnp.dot(
            br[...], wb_ref[i], preferred_element_type=F32)
        m = t if m is None else m + t
    o_ref[...] = x + jnp.dot(m.astype(BF16), wo_ref[...], preferred_element_type=F32)


def _merge(x2d, h2d, wg, oa, ob, oc, wb, wo):
    m = x2d.shape[0]
    tm = min(TOKEN_TILE, m)
    bw = oa.shape[1]
    tok = lambda w: pl.BlockSpec((tm, w), lambda i: (i, 0))
    return pl.pallas_call(
        _merge_body,
        grid=(m // tm,),
        in_specs=[tok(D_MODEL), tok(D_MODEL), _const_spec(wg.shape),
                  tok(bw), tok(bw), tok(bw), _const_spec(wb.shape), _const_spec(wo.shape)],
        out_specs=tok(D_MODEL),
        out_shape=jax.ShapeDtypeStruct((m, D_MODEL), F32),
        compiler_params=pltpu.CompilerParams(
            dimension_semantics=("arbitrary",), vmem_limit_bytes=VMEM_LIMIT),
        name="merge",
    )(x2d, h2d, wg, oa, ob, oc, wb, wo)


def _rope_tables(pos0, l):
    half = RET_DK // 2
    inv = ROPE_BASE ** (-jnp.arange(half, dtype=F32) / half)
    ang = (pos0 + jnp.arange(l)).astype(F32)[:, None] * inv[None, :]
    cos = jnp.cos(ang)
    sin = jnp.sin(ang)
    cos = jnp.tile(jnp.concatenate([cos, cos], axis=1), (1, RET_HEADS))
    sin = jnp.tile(jnp.concatenate([-sin, sin], axis=1), (1, RET_HEADS))
    return cos, sin


def _prep_weights(p):
    w_in = p["w_in"].astype(BF16)
    depth = w_in.shape[0]
    o = 0
    pieces = {}
    for name, size in (("a_qkvg", 2 * RET_QK + 2 * RET_W), ("b_qkv", GDN_CONV_CH), ("b_g", GDN_V),
                       ("b_beta", GDN_HEADS), ("b_alpha", GDN_HEADS), ("c_z", RWKV_SHIFT_CH),
                       ("gate", 3 * D_MODEL)):
        pieces[name] = w_in[:, :, o:o + size]
        o += size
    pad_l = lambda t, n: jnp.pad(t, ((0, 0), (0, 0), (0, n - t.shape[2])))
    row = lambda t: t[:, None, :]
    zeros64 = jnp.zeros((depth, 64, RWKV_W), F32)
    return dict(
        norm_ffn1=row(p["norm_ffn1"]), norm_mix=row(p["norm_mix"]), norm_ffn2=row(p["norm_ffn2"]),
        ffn1_w13=p["ffn1_w13"].astype(BF16), ffn1_w2=p["ffn1_w2"].astype(BF16),
        ffn2_w13=p["ffn2_w13"].astype(BF16), ffn2_w2=p["ffn2_w2"].astype(BF16),
        w_ret=pieces["a_qkvg"],
        w_gdn=jnp.concatenate([pieces["b_qkv"], pieces["b_g"], pad_l(pieces["b_beta"], LANES),
                               pad_l(pieces["b_alpha"], LANES)], axis=2),
        w_rwkv=pieces["c_z"],
        w_gate=pieces["gate"],
        ret_gn=row(p["ret_gn"]),
        gdn_conv=p["gdn_conv"],
        gdn_a_log=row(jnp.pad(p["gdn_a_log"], ((0, 0), (0, LANES - GDN_HEADS)))),
        gdn_dt_bias=row(jnp.pad(p["gdn_dt_bias"], ((0, 0), (0, LANES - GDN_HEADS)))),
        gdn_norm=row(jnp.tile(p["gdn_norm"], (1, GDN_HEADS))),
        rwkv_mu=row(p["rwkv_mu"]),
        rwkv_w0=row(p["rwkv_w0"]),
        rwkv_w2=jnp.concatenate([p["rwkv_w2"], zeros64], axis=1),
        rwkv_a0=row(p["rwkv_a0"]),
        rwkv_a2=jnp.concatenate([zeros64, p["rwkv_a2"]], axis=1),
        rwkv_g2=p["rwkv_g2"],
        rwkv_kk=row(p["rwkv_kk"]), rwkv_ka=row(p["rwkv_ka"]), rwkv_rk=row(p["rwkv_rk"]),
        rwkv_gn=row(p["rwkv_gn"]),
        w_branch=p["w_branch"].astype(BF16), w_out=p["w_out"].astype(BF16),
    )


def _layer(x, st, lp, tables, gf, final_norm, tb):
    b, l, d = x.shape
    s_ret, s_delta, s_conv, s_wkv, s_shift = st
    cos, sin = tables
    x2, h2 = _ffn(x.reshape(b * l, d), lp["norm_ffn1"], lp["ffn1_w13"], lp["ffn1_w2"],
                  lp["norm_mix"], "with_h")
    h3 = h2.reshape(b, l, d)
    oa, n_ret = _retention(h3, lp["w_ret"], cos, sin, lp["ret_gn"], s_ret, tb)
    ob, n_delta, n_conv = _gdn_pipe(h3, lp["w_gdn"], lp["gdn_conv"], lp["gdn_a_log"],
                               lp["gdn_dt_bias"], lp["gdn_norm"], s_delta, s_conv, tb)
    oc, n_wkv, n_shift = _rwkv_pipe(h3, lp["w_rwkv"], lp["rwkv_mu"], lp["rwkv_w0"],
                               lp["rwkv_w2"], lp["rwkv_a0"], lp["rwkv_a2"], lp["rwkv_g2"],
                               lp["rwkv_kk"], lp["rwkv_ka"], lp["rwkv_rk"], lp["rwkv_gn"],
                               s_wkv, s_shift, tb)
    bw = oa.shape[-1]
    x4 = _merge(x2, h2, lp["w_gate"], oa.reshape(b * l, bw), ob.reshape(b * l, bw),
                oc.reshape(b * l, bw), lp["w_branch"], lp["w_out"])
    (x5,) = _ffn(x4, lp["norm_ffn2"], lp["ffn2_w13"], lp["ffn2_w2"], gf,
                 "normed" if final_norm else "plain")
    return x5.reshape(b, l, d), (n_ret, n_delta, n_conv, n_wkv, n_shift)


def _run_stream(x, pos0, states, wts, gf, tb):
    depth = wts["w_ret"].shape[0]
    l = x.shape[1]
    tb = min(tb, l)
    tables = _rope_tables(pos0, l)
    new = [[] for _ in range(5)]
    for li in range(depth):
        lp = {k: v[li] for k, v in wts.items()}
        st = tuple(s[li] for s in states)
        x, nst = _layer(x, st, lp, tables, gf, li == depth - 1, tb)
        for jdx in range(5):
            new[jdx].append(nst[jdx])
    return x, tuple(jnp.stack(t, axis=0) for t in new)


def kernel(x_prompt, x_sample, state_ret, state_delta, state_conv, state_wkv, state_shift,
           norm_ffn1, ffn1_w13, ffn1_w2, norm_mix, w_in, ret_gn, gdn_conv, gdn_a_log, gdn_dt_bias,
           gdn_norm, rwkv_mu, rwkv_w0, rwkv_w2, rwkv_a0, rwkv_a2, rwkv_g2, rwkv_kk, rwkv_ka, rwkv_rk,
           rwkv_gn, w_branch, w_out, norm_ffn2, ffn2_w13, ffn2_w2, norm_final):
    params = dict(
        norm_ffn1=norm_ffn1, ffn1_w13=ffn1_w13, ffn1_w2=ffn1_w2, norm_mix=norm_mix, w_in=w_in,
        ret_gn=ret_gn, gdn_conv=gdn_conv, gdn_a_log=gdn_a_log, gdn_dt_bias=gdn_dt_bias,
        gdn_norm=gdn_norm, rwkv_mu=rwkv_mu, rwkv_w0=rwkv_w0, rwkv_w2=rwkv_w2, rwkv_a0=rwkv_a0,
        rwkv_a2=rwkv_a2, rwkv_g2=rwkv_g2, rwkv_kk=rwkv_kk, rwkv_ka=rwkv_ka, rwkv_rk=rwkv_rk,
        rwkv_gn=rwkv_gn, w_branch=w_branch, w_out=w_out, norm_ffn2=norm_ffn2, ffn2_w13=ffn2_w13,
        ffn2_w2=ffn2_w2)
    wts = _prep_weights(params)
    gf = norm_final[None, :]
    depth = w_in.shape[0]
    bp = x_prompt.shape[0]
    zero_states = (
        jnp.zeros((depth, bp, RET_HEADS, RET_DK, RET_DV), F32),
        jnp.zeros((depth, bp, GDN_HEADS, GDN_DK, GDN_DV), F32),
        jnp.zeros((depth, bp, CONV_W - 1, GDN_CONV_CH), F32),
        jnp.zeros((depth, bp, RWKV_HEADS, RWKV_N, RWKV_N), F32),
        jnp.zeros((depth, bp, 1, RWKV_SHIFT_CH), F32),
    )
    y_p, st_p = _run_stream(x_prompt, 0, zero_states, wts, gf, SEQ_BLOCK)
    y_s, st_s = _run_stream(x_sample, PAST_LEN, (state_ret, state_delta, state_conv, state_wkv,
                                                 state_shift), wts, gf, SEQ_BLOCK)
    return (y_p, y_s) + st_p + st_s
```

```python
import functools
import math

import jax
import jax.numpy as jnp
from jax import lax
from jax.experimental import pallas as pl
from jax.experimental.pallas import tpu as pltpu

F32 = jnp.float32
BF16 = jnp.bfloat16

D_MODEL = 1024
D_FF = 2816
EPS = 1e-6
PAST_LEN = 2048
RET_HEADS, RET_DK, RET_DV = 4, 64, 128
ROPE_BASE = 10000.0
GDN_HEADS, GDN_DK, GDN_DV, CONV_W = 8, 64, 64, 4
RWKV_HEADS, RWKV_N = 8, 64
RWKV_GN_EPS = 64e-5
RET_QK = RET_HEADS * RET_DK
RET_W = RET_HEADS * RET_DV
GDN_QK = GDN_HEADS * GDN_DK
GDN_V = GDN_HEADS * GDN_DV
GDN_CONV_CH = 2 * GDN_QK + GDN_V
RWKV_W = RWKV_HEADS * RWKV_N
RWKV_SHIFT_CH = 3 * RWKV_W + 256
LANES = 128
SUBLANES = 8
VMEM_LIMIT = 56 * 1024 * 1024
HEAD = 64
PAIRS = 4
CORE = 64
SEQ_BLOCK = 512
TOKEN_TILE = 512
FFN_TILE = 1024
FF_CHUNK = 256
assert D_FF % FF_CHUNK == 0
assert GDN_DK == GDN_DV == RWKV_N == HEAD and GDN_HEADS == RWKV_HEADS == 2 * PAIRS


NT = (((1,), (1,)), ((), ()))
TN = (((0,), (0,)), ((), ()))
NN = (((1,), (0,)), ((), ()))


def _bf(x):
    return x.astype(BF16)


def _mm(a, b, dims=NN):
    return lax.dot_general(a, b, dims, preferred_element_type=F32)


def _dot1(a, b, dims=NN):
    return _mm(_bf(a), _bf(b), dims)


def _split3(x):
    x0 = _bf(x)
    r1 = x - x0.astype(F32)
    x1 = _bf(r1)
    x2 = _bf(r1 - x1.astype(F32))
    return x0, x1, x2


def _dot_exact_rhs(a, b_bf16):
    a0, a1, a2 = _split3(a)
    return _mm(a0, b_bf16) + (_mm(a1, b_bf16) + _mm(a2, b_bf16))


def _dot_exact_lhs(a_bf16, b):
    b0, b1, b2 = _split3(b)
    return _mm(a_bf16, b0) + (_mm(a_bf16, b1) + _mm(a_bf16, b2))


def _rms(x, g):
    return x * lax.rsqrt(jnp.mean(x * x, axis=-1, keepdims=True) + EPS) * g


def _sigmoid(x):
    return 0.5 + 0.5 * jnp.tanh(0.5 * x)


def _silu(x):
    return x * _sigmoid(x)


def _softplus(x):
    return jnp.maximum(x, 0.0) + jnp.log(1.0 + jnp.exp(-jnp.abs(x)))


def _iota2(shape, dim):
    return lax.broadcasted_iota(jnp.int32, shape, dim)


def _group_ones(width, group):
    r = _iota2((width, width), 0) // group
    c = _iota2((width, width), 1) // group
    return jnp.where(r == c, 1.0, 0.0).astype(BF16)


def _head_expand(width):
    r = _iota2((LANES, width), 0)
    c = _iota2((LANES, width), 1) // HEAD
    return jnp.where(r == c, 1.0, 0.0).astype(BF16)


def _chunk_tril(tb, chunk):
    r = _iota2((tb, tb), 0)
    c = _iota2((tb, tb), 1)
    return jnp.where(((r // chunk) == (c // chunk)) & (c <= r), 1.0, 0.0).astype(BF16)


def _pad_rows(x, rows):
    if x.shape[0] == rows:
        return x
    return jnp.concatenate([x, jnp.zeros((rows - x.shape[0], x.shape[1]), x.dtype)], axis=0)


def _shift_rows(x, prev, i):
    rolled = pltpu.roll(x, i, 0)
    first = jnp.where(_iota2((SUBLANES, x.shape[1]), 0) < i, pltpu.roll(prev, i, 0),
                      rolled[:SUBLANES])
    return jnp.concatenate([first, rolled[SUBLANES:]], axis=0)


def _bd(y, left):
    z = jnp.zeros_like(y)
    return jnp.concatenate([jnp.where(left, y, z), jnp.where(left, z, y)], axis=0)


def _diag(r, left):
    return jnp.where(left, r[:HEAD], r[HEAD:])


BASE = 16


def _bd8(y):
    g = _iota2((BASE, LANES), 1) // BASE
    z = jnp.zeros_like(y)
    return jnp.concatenate([jnp.where(g == i, y, z) for i in range(LANES // BASE)], axis=0)


def _tri_inv_sbs(a_list, left, tick=lambda: None):
    r = _iota2((CORE, LANES), 0)
    c = _iota2((CORE, LANES), 1) % HEAD
    nb = CORE // BASE
    lane16 = _iota2((BASE, LANES), 1)
    blk_of_lane = (lane16 % HEAD) // BASE
    eye16 = jnp.where(_iota2((BASE, LANES), 0) == lane16 % BASE, 1.0, 0.0).astype(F32)

    def diag_blocks(a):
        d = a[(nb - 1) * BASE:]
        for b in range(nb - 2, -1, -1):
            d = jnp.where(blk_of_lane == b, a[b * BASE:(b + 1) * BASE], d)
        return d

    d0f = [diag_blocks(a) for a in a_list]
    d0 = [_bf(x) for x in d0f]
    pw = [_bf(_mm(x, _bd8(x))) for x in d0]
    tick()
    m = [eye16 - x for x in d0f]
    for _ in range(2):
        bdp = [_bd8(x) for x in pw]
        both = [_mm(jnp.concatenate([x, _bf(mi)], axis=0), y) for x, mi, y in zip(pw, m, bdp)]
        tick()
        pw = [_bf(t[:BASE]) for t in both]
        m = [mi + t[BASE:] for mi, t in zip(m, both)]
    m = [mi + _mm(_bf(mi), _bd8(x)) for mi, x in zip(m, pw)]
    tick()
    m = [jnp.concatenate([jnp.where(blk_of_lane == b, mi, 0.0) for b in range(nb)], axis=0) for mi in m]
    size = 2 * BASE
    while size <= CORE:
        half = size // 2
        nh = CORE // half
        sel = ((r // size) == (c // size)) & ((r // half) != (c // half))
        bdb = [_bd(_bf(jnp.where(sel, a, 0.0)), left) for a in a_list]
        mb = [_bf(mi) for mi in m]
        low = [jnp.concatenate([x[b * half:(b + 1) * half] for b in range(1, nh, 2)], axis=0) for x in mb]
        t = [_bf(_mm(x, y)) for x, y in zip(low, bdb)]
        tick()
        upd = [_mm(x, _bd(y, left)) for x, y in zip(t, mb)]
        tick()
        m = [jnp.concatenate([mi[b * half:(b + 1) * half] - u[(b // 2) * half:(b // 2 + 1) * half]
                              if b % 2 else mi[b * half:(b + 1) * half] for b in range(nh)], axis=0)
             for mi, u in zip(m, upd)]
        size *= 2
    return m


def _const_spec(shape):
    nd = len(shape)
    return pl.BlockSpec(shape, lambda *_: (0,) * nd, pipeline_mode=pl.Buffered(1))


def _ffn_body(x_ref, g_ref, w13_ref, w2_ref, gn_ref, *o_refs, tail):
    x = x_ref[...]
    h = _rms(x, g_ref[...]).astype(BF16)
    acc = None
    for c in range(D_FF // FF_CHUNK):
        lo = c * FF_CHUNK
        a = jnp.dot(h, w13_ref[:, lo:lo + FF_CHUNK], preferred_element_type=F32)
        b = jnp.dot(h, w13_ref[:, D_FF + lo:D_FF + lo + FF_CHUNK], preferred_element_type=F32)
        part = jnp.dot((_silu(a) * b).astype(BF16), w2_ref[lo:lo + FF_CHUNK, :],
                       preferred_element_type=F32)
        acc = part if acc is None else acc + part
    y = x + 0.5 * acc
    if tail == "normed":
        o_refs[0][...] = _rms(y, gn_ref[...])
    else:
        o_refs[0][...] = y
    if tail == "with_h":
        o_refs[1][...] = _rms(y, gn_ref[...]).astype(BF16)


def _ffn(x2d, g, w13, w2, gn, tail):
    m = x2d.shape[0]
    tm = min(FFN_TILE, m)
    tok = pl.BlockSpec((tm, D_MODEL), lambda i: (i, 0))
    out_specs = [tok]
    out_shape = [jax.ShapeDtypeStruct((m, D_MODEL), F32)]
    if tail == "with_h":
        out_specs.append(tok)
        out_shape.append(jax.ShapeDtypeStruct((m, D_MODEL), BF16))
    return pl.pallas_call(
        functools.partial(_ffn_body, tail=tail),
        grid=(m // tm,),
        in_specs=[
            tok,
            _const_spec((1, D_MODEL)),
            _const_spec((D_MODEL, 2 * D_FF)),
            _const_spec((D_FF, D_MODEL)),
            _const_spec((1, D_MODEL)),
        ],
        out_specs=out_specs,
        out_shape=out_shape,
        compiler_params=pltpu.CompilerParams(
            dimension_semantics=("arbitrary",), vmem_limit_bytes=VMEM_LIMIT),
        name="ffn",
    )(x2d, g, w13, w2, gn)


RET_PAIRS = RET_HEADS // 2
RET_LOG_G = [math.log1p(-(2.0 ** (-5.0 - hh))) for hh in range(RET_HEADS)]


def _ret_body(h_ref, w_ref, cos_ref, sin_ref, gn_ref, s0_ref,
              o_ref, sn_ref, s_scr, q_scr, k_scr, v_scr, o_scr, *, tb):
    j = pl.program_id(1)
    tbp = max(tb, CORE)
    nc = tbp // CORE
    grp = _chunk_groups(nc)
    valid = min(tb, CORE)
    zero_blk = jnp.zeros((RET_DK, RET_DV), F32)

    @pl.when(j == 0)
    def _():
        for p in range(RET_PAIRS):
            s_scr[p] = jnp.concatenate(
                [jnp.concatenate([s0_ref[2 * p], zero_blk], axis=1),
                 jnp.concatenate([zero_blk, s0_ref[2 * p + 1]], axis=1)], axis=0)

    z = jnp.dot(h_ref[...], w_ref[...], preferred_element_type=F32)
    q = z[:, :RET_QK]
    k = z[:, RET_QK:2 * RET_QK]
    cos = cos_ref[...]
    sin = sin_ref[...]
    first = (_iota2((tb, RET_QK), 1) % RET_DK) < (RET_DK // 2)

    def rot(t):
        swapped = jnp.where(first, pltpu.roll(t, RET_QK - RET_DK // 2, 1),
                            pltpu.roll(t, RET_DK // 2, 1))
        return t * cos + swapped * sin

    q_scr[...] = _pad_rows(rot(q), tbp)
    k_scr[...] = _pad_rows(rot(k) * (RET_DK ** -0.5), tbp)
    v_scr[...] = _pad_rows(z[:, 2 * RET_QK:2 * RET_QK + RET_W], tbp)

    left, incl, _ = _core_masks()
    ri = _iota2((CORE, LANES), 0)
    ci = _iota2((CORE, LANES), 1) % HEAD
    rf = ri.astype(F32)
    dfi = jnp.where(incl, (ri - ci).astype(F32), 0.0)
    row2 = _iota2((2 * RET_DK, 2 * RET_DV), 0) < RET_DK
    col2 = _iota2((2 * RET_DK, 2 * RET_DV), 1) < RET_DV
    blockmask = row2 == col2
    vtop = _iota2((CORE, 2 * RET_DV), 1) < RET_DV
    dmask, qdec, kdec, cdec = [], [], [], []
    for p in range(RET_PAIRS):
        lg = jnp.where(left, RET_LOG_G[2 * p], RET_LOG_G[2 * p + 1])
        dmask.append(jnp.where(incl, jnp.exp(dfi * lg), 0.0))
        qdec.append(jnp.exp((rf + 1.0) * lg))
        kdec.append(jnp.exp((valid - 1.0 - rf) * lg))
        cdec.append(jnp.where(row2, math.exp(valid * RET_LOG_G[2 * p]),
                              math.exp(valid * RET_LOG_G[2 * p + 1])))
    psl = [slice(p * LANES, (p + 1) * LANES) for p in range(RET_PAIRS)]
    vsl = [slice(p * 2 * RET_DV, (p + 1) * 2 * RET_DV) for p in range(RET_PAIRS)]

    def body(i, carry):
        chunks = []
        for gi in range(grp):
            rows = pl.ds(pl.multiple_of((i * grp + gi) * CORE, CORE), CORE)
            chunks.append((rows, q_scr[rows, :], k_scr[rows, :], v_scr[rows, :]))
        items = [(ch, p) for ch in chunks for p in range(RET_PAIRS)]
        scores = [_mm(_bf(qc[:, psl[p]]), _bd(_bf(kc[:, psl[p]]), left), NT) * dmask[p]
                  for (_, qc, kc, _), p in items]
        kv = [_mm(_bf(kc[:, psl[p]] * kdec[p]), _bf(vc[:, vsl[p]]), TN)
              for (_, _, kc, vc), p in items]
        state = [s_scr[p] for p in range(RET_PAIRS)]
        for n, ((rows, qc, _, vc), p) in enumerate(items):
            vp = _bf(vc[:, vsl[p]])
            zv = jnp.zeros_like(vp)
            rhs = jnp.concatenate([jnp.where(vtop, vp, zv), jnp.where(vtop, zv, vp), _bf(state[p])],
                                  axis=0)
            lhs = _bf(jnp.concatenate([scores[n], qc[:, psl[p]] * qdec[p]], axis=1))
            o_scr[rows, vsl[p]] = _mm(lhs, rhs)
            state[p] = cdec[p] * state[p] + jnp.where(blockmask, kv[n], 0.0)
        for p in range(RET_PAIRS):
            s_scr[p] = state[p]
        return carry

    lax.fori_loop(0, nc // grp, body, 0)

    gate = z[:, 2 * RET_QK + RET_W:]
    gn = gn_ref[...]
    for hh in range(RET_HEADS):
        sl = slice(hh * RET_DV, (hh + 1) * RET_DV)
        oh = o_scr[pl.ds(0, tb), sl]
        mu = jnp.mean(oh, axis=-1, keepdims=True)
        dlt = oh - mu
        var = jnp.mean(dlt * dlt, axis=-1, keepdims=True)
        y = dlt * lax.rsqrt(var + 1e-5) * gn[:, sl]
        o_ref[:, sl] = (y * _silu(gate[:, sl])).astype(BF16)

    @pl.when(j == pl.num_programs(1) - 1)
    def _():
        for p in range(RET_PAIRS):
            s = s_scr[p]
            sn_ref[2 * p] = s[:RET_DK, :RET_DV]
            sn_ref[2 * p + 1] = s[RET_DK:, RET_DV:]


def _retention(h, w, cos, sin, gn, s0, tb):
    b, l, _ = h.shape
    st = (RET_HEADS, RET_DK, RET_DV)
    tbp = max(tb, CORE)
    return pl.pallas_call(
        functools.partial(_ret_body, tb=tb),
        grid=(b, l // tb),
        in_specs=[
            pl.BlockSpec((None, tb, D_MODEL), lambda i, j: (i, j, 0)),
            _const_spec(w.shape),
            pl.BlockSpec((tb, RET_QK), lambda i, j: (j, 0)),
            pl.BlockSpec((tb, RET_QK), lambda i, j: (j, 0)),
            _const_spec((1, RET_W)),
            pl.BlockSpec((None,) + st, lambda i, j: (i, 0, 0, 0)),
        ],
        out_specs=[
            pl.BlockSpec((None, tb, RET_W), lambda i, j: (i, j, 0)),
            pl.BlockSpec((None,) + st, lambda i, j: (i, 0, 0, 0)),
        ],
        out_shape=[jax.ShapeDtypeStruct((b, l, RET_W), BF16),
                   jax.ShapeDtypeStruct((b,) + st, F32)],
        scratch_shapes=[
            pltpu.VMEM((RET_PAIRS, 2 * RET_DK, 2 * RET_DV), F32),
            pltpu.VMEM((tbp, RET_QK), F32),
            pltpu.VMEM((tbp, RET_QK), F32),
            pltpu.VMEM((tbp, RET_W), F32),
            pltpu.VMEM((tbp, RET_W), F32),
        ],
        compiler_params=pltpu.CompilerParams(
            dimension_semantics=("arbitrary", "arbitrary"), vmem_limit_bytes=VMEM_LIMIT),
        name="retention",
    )(h, w, cos, sin, gn, s0)


HALO = SUBLANES


def _core_masks():
    r = _iota2((CORE, LANES), 0)
    c = _iota2((CORE, LANES), 1) % HEAD
    left = _iota2((CORE, LANES), 1) < HEAD
    return left, r >= c, r > c


def _load_state_sbs(s0_ref, s_scr):
    for p in range(PAIRS):
        s_scr[p] = jnp.concatenate([s0_ref[2 * p], s0_ref[2 * p + 1]], axis=1)


def _store_state_sbs(s_scr, sn_ref):
    for p in range(PAIRS):
        s = s_scr[p]
        sn_ref[2 * p] = s[:, :HEAD]
        sn_ref[2 * p + 1] = s[:, HEAD:]


def _chunk_groups(nc):
    return 8 if nc % 8 == 0 else (4 if nc % 4 == 0 else (2 if nc % 2 == 0 else 1))


SLAB = 128


def _pipe_maps(nblk, total):
    def in_blk(t):
        ta = jnp.minimum(t, total - 1)
        return ta // nblk, ta % nblk

    def out_blk(t):
        tp = jnp.maximum(t - 1, 0)
        return tp // nblk, tp % nblk

    return in_blk, out_blk


def _gdn_pipe_body(h_ref, w_ref, cw_ref, alog_ref, dtb_ref, nrm_ref, s0_ref, c0_ref,
                   o_ref, sn_ref, cn_ref,
                   s_scr, xp_scr, q_scr, k_scr, v_scr, be_scr, ga_scr, gate_scr, gar_scr,
                   o_scr, m_scr, n_scr, qp_scr, egl_scr, *, tb, nblk):
    t = pl.program_id(0)
    total = pl.num_programs(0) - 1
    ja = jnp.minimum(t, total - 1) % nblk
    jp = jnp.maximum(t - 1, 0) % nblk
    wslot = t % 2
    rslot = 1 - wslot
    tbp = max(tb, CORE)
    nc = tbp // CORE
    slab = min(SLAB, tb)
    slabp = max(slab, CORE)
    nslab = tb // slab

    @pl.when(t == 0)
    def _():
        for ref in (q_scr, k_scr, v_scr, be_scr, ga_scr, gate_scr, gar_scr):
            ref[...] = jnp.zeros_like(ref)

    @pl.when(ja == 0)
    def _():
        xp_scr[...] = jnp.zeros_like(xp_scr)
        xp_scr[pl.ds(HALO - (CONV_W - 1), CONV_W - 1), :] = c0_ref[...]

    @pl.when(jp == 0)
    def _():
        _load_state_sbs(s0_ref, s_scr)

    gones = _group_ones(GDN_QK, GDN_DK)
    expand = _head_expand(GDN_V)
    tril = _chunk_tril(slabp, CORE)
    ncp = gar_scr.shape[1]
    rr = _iota2((slabp, GDN_V), 0)
    cc = _iota2((slabp, GDN_V), 1)
    on_diag = (rr % CORE) == (cc % HEAD)

    def input_stage():
        prev = xp_scr[...]
        garv = jnp.zeros((ncp, GDN_V), F32)
        for s in range(nslab):
            z = jnp.dot(h_ref[pl.ds(s * slab, slab), :], w_ref[...], preferred_element_type=F32)
            yield
            zq = z[:, :GDN_CONV_CH]
            cw = cw_ref[...]
            y = zq * cw[CONV_W - 1:CONV_W, :]
            for i in range(1, CONV_W):
                y = y + _shift_rows(zq, prev, i) * cw[CONV_W - 1 - i:CONV_W - i, :]
            prev = zq[slab - HALO:, :]
            yield
            qkv = _silu(y)
            q = qkv[:, :GDN_QK]
            k = qkv[:, GDN_QK:2 * GDN_QK]
            q = _pad_rows(q * lax.rsqrt(_dot1(q * q, gones) + 1e-6) * (GDN_DK ** -0.5), slabp)
            k = _pad_rows(k * lax.rsqrt(_dot1(k * k, gones) + 1e-6), slabp)
            v = _pad_rows(qkv[:, 2 * GDN_QK:], slabp)
            yield
            zb = z[:, GDN_CONV_CH + GDN_V:GDN_CONV_CH + GDN_V + LANES]
            za = z[:, GDN_CONV_CH + GDN_V + LANES:]
            beta = _pad_rows(_sigmoid(zb), slabp)
            log_a = _pad_rows(-jnp.exp(alog_ref[...]) * _softplus(za + dtb_ref[...]), slabp)
            ga = _dot_exact_lhs(tril, log_a)
            gae = _dot_exact_rhs(ga, expand)
            bee = _dot_exact_rhs(beta, expand)
            pick = jnp.where(on_diag, gae, 0.0)
            chunk_of = jnp.where(_iota2((ncp, slabp), 0)
                                 == _iota2((ncp, slabp), 1) // CORE + s * (slabp // CORE), 1.0, 0.0)
            garv = garv + _dot_exact_lhs(_bf(chunk_of), pick)
            yield
            rows = pl.ds(s * slabp, slabp)
            q_scr[wslot, rows, :] = q
            k_scr[wslot, rows, :] = k
            v_scr[wslot, rows, :] = v
            be_scr[wslot, rows, :] = bee
            ga_scr[wslot, rows, :] = gae
            gate_scr[wslot, pl.ds(s * slab, slab), :] = z[:, GDN_CONV_CH:GDN_CONV_CH + GDN_V]
            yield
        xp_scr[...] = prev
        gar_scr[wslot] = garv
        yield

    gen = input_stage()

    def tick():
        next(gen, None)

    left, incl, strict = _core_masks()
    incl4 = jnp.concatenate([incl] * PAIRS, axis=1)
    psl = [slice(p * LANES, (p + 1) * LANES) for p in range(PAIRS)]

    def phase1(chunks, tick):
        probs = []
        for c in chunks:
            rows = pl.ds(c * CORE, CORE)
            gac = ga_scr[rslot, rows, :]
            glast = gac[CORE - 1:CORE, :]
            egl_scr[pl.ds(c, 1), :] = jnp.exp(glast)
            eg = jnp.exp(gac)
            gam = jnp.where(incl4, jnp.exp(jnp.where(incl4, gac - gar_scr[rslot, pl.ds(c, 1), :], 0.0)),
                            0.0)
            qc = q_scr[rslot, rows, :]
            kc = k_scr[rslot, rows, :]
            bec = be_scr[rslot, rows, :]
            probs.append(dict(rows=rows, gam=gam, be=bec, q=_bf(qc), k=_bf(kc), qe=qc * eg,
                              kd=_bf(kc * jnp.exp(glast - gac)), bv=_bf(bec * v_scr[rslot, rows, :]),
                              bek=_bf(bec * eg * kc)))
        tick()
        items = [(pr, s) for pr in probs for s in psl]
        qk = [_mm(jnp.concatenate([pr["q"][:, s], pr["k"][:, s]], axis=0), _bd(pr["k"][:, s], left), NT)
              for pr, s in items]
        tick()
        pm = [_bf(x[:CORE] * pr["gam"][:, s]) for x, (pr, s) in zip(qk, items)]
        amat = [jnp.where(strict, pr["be"][:, s] * x[CORE:] * pr["gam"][:, s], 0.0)
                for x, (pr, s) in zip(qk, items)]
        tinv = _tri_inv_sbs(amat, left, tick)
        sol = [_bf(_mm(_bf(tm), jnp.concatenate([_bd(pr["bv"][:, s], left), _bd(pr["bek"][:, s], left)],
                                                axis=1)))
               for tm, (pr, s) in zip(tinv, items)]
        tick()
        kts = [_mm(pr["kd"][:, s], x, TN) for x, (pr, s) in zip(sol, items)]
        tick()
        ps = [_mm(p_, jnp.concatenate([_bd(x[:, :LANES], left), _bd(x[:, LANES:], left)], axis=1))
              for p_, x in zip(pm, sol)]
        tick()
        for n, (pr, s) in enumerate(items):
            n_scr[pr["rows"], s] = _diag(kts[n][:, :LANES], left)
            m_scr[pr["rows"], s] = _diag(kts[n][:, LANES:], left)
            o_scr[pr["rows"], s] = ps[n][:, :LANES]
            qp_scr[pr["rows"], s] = pr["qe"][:, s] - ps[n][:, LANES:]

    phase1(range(nc), tick)

    for c in range(nc):
        rows = pl.ds(c * CORE, CORE)
        mc = _bf(m_scr[rows, :])
        nn = n_scr[rows, :]
        qpc = _bf(qp_scr[rows, :])
        eglc = egl_scr[pl.ds(c, 1), :]
        s_old = [s_scr[p] for p in range(PAIRS)]
        res = [_mm(jnp.concatenate([mc[:, psl[p]], qpc[:, psl[p]]], axis=0), _bd(_bf(s_old[p]), left))
               for p in range(PAIRS)]
        for p in range(PAIRS):
            s_scr[p] = eglc[:, psl[p]] * s_old[p] - res[p][:CORE] + nn[:, psl[p]]
        o_scr[rows, :] = o_scr[rows, :] + jnp.concatenate([x[CORE:] for x in res], axis=1)
        tick()
    for _ in gen:
        pass

    o = o_scr[pl.ds(0, tb), :]
    ms = _dot1(o * o, gones) * (1.0 / GDN_DV)
    gate = gate_scr[rslot]
    o_ref[...] = (o * lax.rsqrt(ms + EPS) * nrm_ref[...] * _silu(gate)).astype(BF16)

    @pl.when((jp == nblk - 1) & (t > 0))
    def _():
        _store_state_sbs(s_scr, sn_ref)

    @pl.when(ja == nblk - 1)
    def _():
        cn_ref[...] = xp_scr[pl.ds(HALO - (CONV_W - 1), CONV_W - 1), :]


def _gdn_pipe(h, w, cw, alog, dtb, nrm, s0, c0, tb):
    b, l, _ = h.shape
    nblk = l // tb
    total = b * nblk
    in_blk, out_blk = _pipe_maps(nblk, total)
    st = (GDN_HEADS, GDN_DK, GDN_DV)
    cst = (CONV_W - 1, GDN_CONV_CH)
    tbp = max(tb, CORE)
    dbl = pltpu.VMEM((2, tbp, GDN_V), F32)
    big = pltpu.VMEM((tbp, GDN_V), F32)
    ncp = max(tbp // CORE, SUBLANES)
    return pl.pallas_call(
        functools.partial(_gdn_pipe_body, tb=tb, nblk=nblk),
        grid=(total + 1,),
        in_specs=[
            pl.BlockSpec((None, tb, D_MODEL), lambda t: in_blk(t) + (0,)),
            _const_spec(w.shape),
            _const_spec(cw.shape),
            _const_spec((1, LANES)),
            _const_spec((1, LANES)),
            _const_spec((1, GDN_V)),
            pl.BlockSpec((None,) + st, lambda t: (out_blk(t)[0], 0, 0, 0)),
            pl.BlockSpec((None,) + cst, lambda t: (in_blk(t)[0], 0, 0)),
        ],
        out_specs=[
            pl.BlockSpec((None, tb, GDN_V), lambda t: out_blk(t) + (0,)),
            pl.BlockSpec((None,) + st, lambda t: (out_blk(t)[0], 0, 0, 0)),
            pl.BlockSpec((None,) + cst, lambda t: (in_blk(t)[0], 0, 0)),
        ],
        out_shape=[jax.ShapeDtypeStruct((b, l, GDN_V), BF16),
                   jax.ShapeDtypeStruct((b,) + st, F32),
                   jax.ShapeDtypeStruct((b,) + cst, F32)],
        scratch_shapes=[
            pltpu.VMEM((PAIRS, HEAD, LANES), F32),
            pltpu.VMEM((HALO, GDN_CONV_CH), F32),
            dbl, dbl, dbl, dbl, dbl,
            pltpu.VMEM((2, tb, GDN_V), F32),
            pltpu.VMEM((2, ncp, GDN_V), F32),
            big, big, big, big,
            pltpu.VMEM((ncp, GDN_V), F32),
        ],
        compiler_params=pltpu.CompilerParams(
            dimension_semantics=("arbitrary",), vmem_limit_bytes=VMEM_LIMIT),
        name="gdn",
    )(h, w, cw, alog, dtb, nrm, s0, c0)


def _rwkv_pipe_body(h_ref, w_ref, mu_ref, w0_ref, w2_ref, a0_ref, a2_ref, g2_ref,
                    kkw_ref, ka_ref, rk_ref, gn_ref, s0_ref, sh0_ref,
                    o_ref, sn_ref, shn_ref,
                    s_scr, zc_scr, r_scr, k2_scr, v_scr, kk_scr, kka_scr, lw_scr, cum_scr,
                    bonus_scr, g_scr, o_scr, qp_scr, m_scr, n_scr, wt_scr, *, tb, nblk):
    t = pl.program_id(0)
    total = pl.num_programs(0) - 1
    ja = jnp.minimum(t, total - 1) % nblk
    jp = jnp.maximum(t - 1, 0) % nblk
    wslot = t % 2
    rslot = 1 - wslot
    tbp = max(tb, CORE)
    nc = tbp // CORE
    slab = min(SLAB, tb)
    slabp = max(slab, CORE)
    nslab = tb // slab

    @pl.when(t == 0)
    def _():
        for ref in (r_scr, k2_scr, v_scr, kk_scr, kka_scr, lw_scr, cum_scr, bonus_scr, g_scr):
            ref[...] = jnp.zeros_like(ref)

    @pl.when(ja == 0)
    def _():
        zc_scr[...] = jnp.zeros_like(zc_scr)
        zc_scr[pl.ds(HALO - 1, 1), :] = sh0_ref[...]

    @pl.when(jp == 0)
    def _():
        _load_state_sbs(s0_ref, s_scr)

    gones = _group_ones(RWKV_W, RWKV_N)
    tril = _chunk_tril(slabp, CORE)

    def input_stage():
        prev = zc_scr[...]
        for s in range(nslab):
            z = jnp.dot(h_ref[pl.ds(s * slab, slab), :], w_ref[...], preferred_element_type=F32)
            yield
            cs = z + (_shift_rows(z, prev, 1) - z) * mu_ref[...]
            prev = z[slab - HALO:, :]
            yield
            r = cs[:, :RWKV_W]
            k = cs[:, RWKV_W:2 * RWKV_W]
            v = cs[:, 2 * RWKV_W:3 * RWKV_W]
            zwa = cs[:, 3 * RWKV_W:3 * RWKV_W + LANES]
            zg = cs[:, 3 * RWKV_W + LANES:]
            w_raw = -_softplus(-(w0_ref[...] + _dot1(jnp.tanh(zwa), w2_ref[...]))) - 0.5
            lw = _pad_rows(-jnp.exp(w_raw), slabp)
            a = _sigmoid(a0_ref[...] + _dot1(zwa, a2_ref[...]))
            g = _dot1(_sigmoid(zg), g2_ref[...])
            yield
            kkn = k * kkw_ref[...]
            kkn = kkn * lax.rsqrt(_dot1(kkn * kkn, gones) + 1e-6)
            k2 = k * (1.0 + (a - 1.0) * ka_ref[...])
            bonus = _dot1(r * k2 * rk_ref[...], gones) * v
            yield
            rows = pl.ds(s * slabp, slabp)
            r_scr[wslot, rows, :] = _pad_rows(r, slabp)
            k2_scr[wslot, rows, :] = _pad_rows(k2, slabp)
            v_scr[wslot, rows, :] = _pad_rows(v, slabp)
            kk_scr[wslot, rows, :] = _pad_rows(kkn, slabp)
            kka_scr[wslot, rows, :] = _pad_rows(kkn * a, slabp)
            lw_scr[wslot, rows, :] = lw
            cum_scr[wslot, rows, :] = _dot_exact_lhs(tril, lw)
            bonus_scr[wslot, pl.ds(s * slab, slab), :] = bonus
            g_scr[wslot, pl.ds(s * slab, slab), :] = g
            yield
        zc_scr[...] = prev
        yield

    gen = input_stage()

    def tick():
        next(gen, None)

    left, incl, strict = _core_masks()
    psl = [slice(p * LANES, (p + 1) * LANES) for p in range(PAIRS)]

    probs = []
    for c in range(nc):
        rows = pl.ds(c * CORE, CORE)
        cum = cum_scr[rslot, rows, :]
        tot = cum[CORE - 1:CORE, :]
        wt_scr[pl.ds(c, 1), :] = jnp.exp(tot)
        w_inv = jnp.exp(-cum)
        w_rem = jnp.exp(tot - cum)
        kkac = kka_scr[rslot, rows, :]
        k2c = k2_scr[rslot, rows, :]
        rt = r_scr[rslot, rows, :] * jnp.exp(cum)
        probs.append(dict(rows=rows, rt=rt, rtb=_bf(rt),
                          at=_bf(-kk_scr[rslot, rows, :] * jnp.exp(cum - lw_scr[rslot, rows, :])),
                          bt=_bf(kkac * w_inv), kt=_bf(k2c * w_inv), bh=_bf(kkac * w_rem),
                          kh=_bf(k2c * w_rem), v=_bf(v_scr[rslot, rows, :])))
    tick()
    items = [(pr, s) for pr in probs for s in psl]
    x = [_mm(jnp.concatenate([pr["at"][:, s], pr["rtb"][:, s]], axis=0),
             jnp.concatenate([_bd(pr["bt"][:, s], left), _bd(pr["kt"][:, s], left)], axis=0), NT)
         for pr, s in items]
    tick()
    neg_ab = [jnp.where(strict, -tt[:CORE, :LANES], 0.0) for tt in x]
    a_rb = [_bf(jnp.where(incl, tt[CORE:, :LANES], 0.0)) for tt in x]
    a_ak = [_bf(jnp.where(strict, tt[:CORE, LANES:], 0.0)) for tt in x]
    a_rk = [_bf(jnp.where(incl, tt[CORE:, LANES:], 0.0)) for tt in x]
    tinv = _tri_inv_sbs(neg_ab, left, tick)
    vbd = [_bd(pr["v"][:, s], left) for pr, s in items]
    akv = [_bf(_mm(p_, q_)) for p_, q_ in zip(a_ak, vbd)]
    tick()
    pu = [_bf(_mm(_bf(tm), jnp.concatenate([_bd(pr["at"][:, s], left), _bd(u, left)], axis=1)))
          for tm, u, (pr, s) in zip(tinv, akv, items)]
    tick()
    rbp = [_mm(p_, jnp.concatenate([_bd(u[:, :LANES], left), _bd(u[:, LANES:], left)], axis=1))
           for p_, u in zip(a_rb, pu)]
    tick()
    rkv = [_mm(p_, q_) for p_, q_ in zip(a_rk, vbd)]
    tick()
    bp = [_mm(u, pr["bh"][:, s], TN) for u, (pr, s) in zip(pu, items)]
    tick()
    vk = [_mm(pr["v"][:, s], pr["kh"][:, s], TN) for pr, s in items]
    tick()
    for n, (pr, s) in enumerate(items):
        qp_scr[pr["rows"], s] = pr["rt"][:, s] + rbp[n][:, :LANES]
        o_scr[pr["rows"], s] = rbp[n][:, LANES:] + rkv[n]
        m_scr[pr["rows"], s] = _diag(bp[n][:LANES], left)
        n_scr[pr["rows"], s] = _diag(bp[n][LANES:], left) + _diag(vk[n], left)

    for c in range(nc):
        rows = pl.ds(c * CORE, CORE)
        mc = _bf(m_scr[rows, :])
        nn = n_scr[rows, :]
        qpc = _bf(qp_scr[rows, :])
        wtc = wt_scr[pl.ds(c, 1), :]
        s_old = [s_scr[p] for p in range(PAIRS)]
        sb = [_bf(s) for s in s_old]
        od = [_mm(qpc[:, psl[p]], _bd(sb[p], left), NT) for p in range(PAIRS)]
        sm = [_mm(sb[p], _bd(mc[:, psl[p]], left)) for p in range(PAIRS)]
        for p in range(PAIRS):
            s_scr[p] = s_old[p] * wtc[:, psl[p]] + sm[p] + nn[:, psl[p]]
        o_scr[rows, :] = o_scr[rows, :] + jnp.concatenate(od, axis=1)
        tick()
    for _ in gen:
        pass

    o = o_scr[pl.ds(0, tb), :]
    mean = _dot1(o, gones) * (1.0 / RWKV_N)
    dlt = o - mean
    var = _dot1(dlt * dlt, gones) * (1.0 / RWKV_N)
    y = dlt * lax.rsqrt(var + RWKV_GN_EPS) * gn_ref[...]
    o_ref[...] = ((y + bonus_scr[rslot]) * g_scr[rslot]).astype(BF16)

    @pl.when((jp == nblk - 1) & (t > 0))
    def _():
        _store_state_sbs(s_scr, sn_ref)

    @pl.when(ja == nblk - 1)
    def _():
        shn_ref[...] = zc_scr[pl.ds(HALO - 1, 1), :]


def _rwkv_pipe(h, w, mu, w0, w2, a0, a2, g2, kkw, ka, rk, gn, s0, sh0, tb):
    b, l, _ = h.shape
    nblk = l // tb
    total = b * nblk
    in_blk, out_blk = _pipe_maps(nblk, total)
    st = (RWKV_HEADS, RWKV_N, RWKV_N)
    sst = (1, RWKV_SHIFT_CH)
    tbp = max(tb, CORE)
    row = _const_spec((1, RWKV_W))
    dbl = pltpu.VMEM((2, tbp, RWKV_W), F32)
    dbl_tb = pltpu.VMEM((2, tb, RWKV_W), F32)
    big = pltpu.VMEM((tbp, RWKV_W), F32)
    return pl.pallas_call(
        functools.partial(_rwkv_pipe_body, tb=tb, nblk=nblk),
        grid=(total + 1,),
        in_specs=[
            pl.BlockSpec((None, tb, D_MODEL), lambda t: in_blk(t) + (0,)),
            _const_spec(w.shape),
            _const_spec((1, RWKV_SHIFT_CH)),
            row, _const_spec((LANES, RWKV_W)), row, _const_spec((LANES, RWKV_W)),
            _const_spec((LANES, RWKV_W)),
            row, row, row, row,
            pl.BlockSpec((None,) + st, lambda t: (out_blk(t)[0], 0, 0, 0)),
            pl.BlockSpec((None,) + sst, lambda t: (in_blk(t)[0], 0, 0)),
        ],
        out_specs=[
            pl.BlockSpec((None, tb, RWKV_W), lambda t: out_blk(t) + (0,)),
            pl.BlockSpec((None,) + st, lambda t: (out_blk(t)[0], 0, 0, 0)),
            pl.BlockSpec((None,) + sst, lambda t: (in_blk(t)[0], 0, 0)),
        ],
        out_shape=[jax.ShapeDtypeStruct((b, l, RWKV_W), BF16),
                   jax.ShapeDtypeStruct((b,) + st, F32),
                   jax.ShapeDtypeStruct((b,) + sst, F32)],
        scratch_shapes=[
            pltpu.VMEM((PAIRS, HEAD, LANES), F32),
            pltpu.VMEM((HALO, RWKV_SHIFT_CH), F32),
            dbl, dbl, dbl, dbl, dbl, dbl, dbl,
            dbl_tb, dbl_tb,
            big, big, big, big,
            pltpu.VMEM((max(tbp // CORE, SUBLANES), RWKV_W), F32),
        ],
        compiler_params=pltpu.CompilerParams(
            dimension_semantics=("arbitrary",), vmem_limit_bytes=VMEM_LIMIT),
        name="rwkv7",
    )(h, w, mu, w0, w2, a0, a2, g2, kkw, ka, rk, gn, s0, sh0)


def _merge_body(x_ref, h_ref, wg_ref, oa_ref, ob_ref, oc_ref, wb_ref, wo_ref, o_ref):
    x = x_ref[...]
    gate = jnp.dot(h_ref[...], wg_ref[...], preferred_element_type=F32)
    m = None
    for i, br in enumerate((oa_ref, ob_ref, oc_ref)):
        t = _sigmoid(gate[:, i * D_MODEL:(i + 1) * D_MODEL]) * jnp.dot(
            br[...], wb_ref[i], preferred_element_type=F32)
        m = t if m is None else m + t
    o_ref[...] = x + jnp.dot(m.astype(BF16), wo_ref[...], preferred_element_type=F32)


def _merge(x2d, h2d, wg, oa, ob, oc, wb, wo):
    m = x2d.shape[0]
    tm = min(TOKEN_TILE, m)
    bw = oa.shape[1]
    tok = lambda w: pl.BlockSpec((tm, w), lambda i: (i, 0))
    return pl.pallas_call(
        _merge_body,
        grid=(m // tm,),
        in_specs=[tok(D_MODEL), tok(D_MODEL), _const_spec(wg.shape),
                  tok(bw), tok(bw), tok(bw), _const_spec(wb.shape), _const_spec(wo.shape)],
        out_specs=tok(D_MODEL),
        out_shape=jax.ShapeDtypeStruct((m, D_MODEL), F32),
        compiler_params=pltpu.CompilerParams(
            dimension_semantics=("arbitrary",), vmem_limit_bytes=VMEM_LIMIT),
        name="merge",
    )(x2d, h2d, wg, oa, ob, oc, wb, wo)


def _rope_tables(pos0, l):
    half = RET_DK // 2
    inv = ROPE_BASE ** (-jnp.arange(half, dtype=F32) / half)
    ang = (pos0 + jnp.arange(l)).astype(F32)[:, None] * inv[None, :]
    cos = jnp.cos(ang)
    sin = jnp.sin(ang)
    cos = jnp.tile(jnp.concatenate([cos, cos], axis=1), (1, RET_HEADS))
    sin = jnp.tile(jnp.concatenate([-sin, sin], axis=1), (1, RET_HEADS))
    return cos, sin


def _prep_weights(p):
    w_in = p["w_in"].astype(BF16)
    depth = w_in.shape[0]
    o = 0
    pieces = {}
    for name, size in (("a_qkvg", 2 * RET_QK + 2 * RET_W), ("b_qkv", GDN_CONV_CH), ("b_g", GDN_V),
                       ("b_beta", GDN_HEADS), ("b_alpha", GDN_HEADS), ("c_z", RWKV_SHIFT_CH),
                       ("gate", 3 * D_MODEL)):
        pieces[name] = w_in[:, :, o:o + size]
        o += size
    pad_l = lambda t, n: jnp.pad(t, ((0, 0), (0, 0), (0, n - t.shape[2])))
    row = lambda t: t[:, None, :]
    zeros64 = jnp.zeros((depth, 64, RWKV_W), F32)
    return dict(
        norm_ffn1=row(p["norm_ffn1"]), norm_mix=row(p["norm_mix"]), norm_ffn2=row(p["norm_ffn2"]),
        ffn1_w13=p["ffn1_w13"].astype(BF16), ffn1_w2=p["ffn1_w2"].astype(BF16),
        ffn2_w13=p["ffn2_w13"].astype(BF16), ffn2_w2=p["ffn2_w2"].astype(BF16),
        w_ret=pieces["a_qkvg"],
        w_gdn=jnp.concatenate([pieces["b_qkv"], pieces["b_g"], pad_l(pieces["b_beta"], LANES),
                               pad_l(pieces["b_alpha"], LANES)], axis=2),
        w_rwkv=pieces["c_z"],
        w_gate=pieces["gate"],
        ret_gn=row(p["ret_gn"]),
        gdn_conv=p["gdn_conv"],
        gdn_a_log=row(jnp.pad(p["gdn_a_log"], ((0, 0), (0, LANES - GDN_HEADS)))),
        gdn_dt_bias=row(jnp.pad(p["gdn_dt_bias"], ((0, 0), (0, LANES - GDN_HEADS)))),
        gdn_norm=row(jnp.tile(p["gdn_norm"], (1, GDN_HEADS))),
        rwkv_mu=row(p["rwkv_mu"]),
        rwkv_w0=row(p["rwkv_w0"]),
        rwkv_w2=jnp.concatenate([p["rwkv_w2"], zeros64], axis=1),
        rwkv_a0=row(p["rwkv_a0"]),
        rwkv_a2=jnp.concatenate([zeros64, p["rwkv_a2"]], axis=1),
        rwkv_g2=p["rwkv_g2"],
        rwkv_kk=row(p["rwkv_kk"]), rwkv_ka=row(p["rwkv_ka"]), rwkv_rk=row(p["rwkv_rk"]),
        rwkv_gn=row(p["rwkv_gn"]),
        w_branch=p["w_branch"].astype(BF16), w_out=p["w_out"].astype(BF16),
    )


def _layer(x, st, lp, tables, gf, final_norm, tb):
    b, l, d = x.shape
    s_ret, s_delta, s_conv, s_wkv, s_shift = st
    cos, sin = tables
    x2, h2 = _ffn(x.reshape(b * l, d), lp["norm_ffn1"], lp["ffn1_w13"], lp["ffn1_w2"],
                  lp["norm_mix"], "with_h")
    h3 = h2.reshape(b, l, d)
    oa, n_ret = _retention(h3, lp["w_ret"], cos, sin, lp["ret_gn"], s_ret, tb)
    ob, n_delta, n_conv = _gdn_pipe(h3, lp["w_gdn"], lp["gdn_conv"], lp["gdn_a_log"],
                               lp["gdn_dt_bias"], lp["gdn_norm"], s_delta, s_conv, tb)
    oc, n_wkv, n_shift = _rwkv_pipe(h3, lp["w_rwkv"], lp["rwkv_mu"], lp["rwkv_w0"],
                                    lp["rwkv_w2"], lp["rwkv_a0"], lp["rwkv_a2"], lp["rwkv_g2"],
                                    lp["rwkv_kk"], lp["rwkv_ka"], lp["rwkv_rk"], lp["rwkv_gn"],
                                    s_wkv, s_shift, tb)
    bw = oa.shape[-1]
    x4 = _merge(x2, h2, lp["w_gate"], oa.reshape(b * l, bw), ob.reshape(b * l, bw),
                oc.reshape(b * l, bw), lp["w_branch"], lp["w_out"])
    (x5,) = _ffn(x4, lp["norm_ffn2"], lp["ffn2_w13"], lp["ffn2_w2"], gf,
                 "normed" if final_norm else "plain")
    return x5.reshape(b, l, d), (n_ret, n_delta, n_conv, n_wkv, n_shift)


def _run_stream(x, pos0, states, wts, gf, tb):
    depth = wts["w_ret"].shape[0]
    l = x.shape[1]
    tb = min(tb, l)
    tables = _rope_tables(pos0, l)
    new = [[] for _ in range(5)]
    for li in range(depth):
        lp = {k: v[li] for k, v in wts.items()}
        st = tuple(s[li] for s in states)
        x, nst = _layer(x, st, lp, tables, gf, li == depth - 1, tb)
        for jdx in range(5):
            new[jdx].append(nst[jdx])
    return x, tuple(jnp.stack(t, axis=0) for t in new)


def kernel(x_prompt, x_sample, state_ret, state_delta, state_conv, state_wkv, state_shift,
           norm_ffn1, ffn1_w13, ffn1_w2, norm_mix, w_in, ret_gn, gdn_conv, gdn_a_log, gdn_dt_bias,
           gdn_norm, rwkv_mu, rwkv_w0, rwkv_w2, rwkv_a0, rwkv_a2, rwkv_g2, rwkv_kk, rwkv_ka, rwkv_rk,
           rwkv_gn, w_branch, w_out, norm_ffn2, ffn2_w13, ffn2_w2, norm_final):
    params = dict(
        norm_ffn1=norm_ffn1, ffn1_w13=ffn1_w13, ffn1_w2=ffn1_w2, norm_mix=norm_mix, w_in=w_in,
        ret_gn=ret_gn, gdn_conv=gdn_conv, gdn_a_log=gdn_a_log, gdn_dt_bias=gdn_dt_bias,
        gdn_norm=gdn_norm, rwkv_mu=rwkv_mu, rwkv_w0=rwkv_w0, rwkv_w2=rwkv_w2, rwkv_a0=rwkv_a0,
        rwkv_a2=rwkv_a2, rwkv_g2=rwkv_g2, rwkv_kk=rwkv_kk, rwkv_ka=rwkv_ka, rwkv_rk=rwkv_rk,
        rwkv_gn=rwkv_gn, w_branch=w_branch, w_out=w_out, norm_ffn2=norm_ffn2, ffn2_w13=ffn2_w13,
        ffn2_w2=ffn2_w2)
    wts = _prep_weights(params)
    gf = norm_final[None, :]
    depth = w_in.shape[0]
    bp = x_prompt.shape[0]
    zero_states = (
        jnp.zeros((depth, bp, RET_HEADS, RET_DK, RET_DV), F32),
        jnp.zeros((depth, bp, GDN_HEADS, GDN_DK, GDN_DV), F32),
        jnp.zeros((depth, bp, CONV_W - 1, GDN_CONV_CH), F32),
        jnp.zeros((depth, bp, RWKV_HEADS, RWKV_N, RWKV_N), F32),
        jnp.zeros((depth, bp, 1, RWKV_SHIFT_CH), F32),
    )
    y_p, st_p = _run_stream(x_prompt, 0, zero_states, wts, gf, SEQ_BLOCK)
    y_s, st_s = _run_stream(x_sample, PAST_LEN, (state_ret, state_delta, state_conv, state_wkv,
                                                 state_shift), wts, gf, SEQ_BLOCK)
    return (y_p, y_s) + st_p + st_s
```

```python
import functools
import math

import jax
import jax.numpy as jnp
from jax import lax
from jax.experimental import pallas as pl
from jax.experimental.pallas import tpu as pltpu

F32 = jnp.float32
BF16 = jnp.bfloat16

D_MODEL = 1024
D_FF = 2816
EPS = 1e-6
PAST_LEN = 2048
RET_HEADS, RET_DK, RET_DV = 4, 64, 128
ROPE_BASE = 10000.0
GDN_HEADS, GDN_DK, GDN_DV, CONV_W = 8, 64, 64, 4
RWKV_HEADS, RWKV_N = 8, 64
RWKV_GN_EPS = 64e-5
RET_QK = RET_HEADS * RET_DK
RET_W = RET_HEADS * RET_DV
GDN_QK = GDN_HEADS * GDN_DK
GDN_V = GDN_HEADS * GDN_DV
GDN_CONV_CH = 2 * GDN_QK + GDN_V
RWKV_W = RWKV_HEADS * RWKV_N
RWKV_SHIFT_CH = 3 * RWKV_W + 256
LANES = 128
SUBLANES = 8
VMEM_LIMIT = 56 * 1024 * 1024
HEAD = 64
PAIRS = 4
CORE = 64
SEQ_BLOCK = 512
TOKEN_TILE = 512
FFN_TILE = 1024
FF_CHUNK = 256
assert D_FF % FF_CHUNK == 0
assert GDN_DK == GDN_DV == RWKV_N == HEAD and GDN_HEADS == RWKV_HEADS == 2 * PAIRS


NT = (((1,), (1,)), ((), ()))
TN = (((0,), (0,)), ((), ()))
NN = (((1,), (0,)), ((), ()))


def _bf(x):
    return x.astype(BF16)


def _mm(a, b, dims=NN):
    return lax.dot_general(a, b, dims, preferred_element_type=F32)


def _dot1(a, b, dims=NN):
    return _mm(_bf(a), _bf(b), dims)


def _split3(x):
    x0 = _bf(x)
    r1 = x - x0.astype(F32)
    x1 = _bf(r1)
    x2 = _bf(r1 - x1.astype(F32))
    return x0, x1, x2


def _dot_exact_rhs(a, b_bf16):
    a0, a1, a2 = _split3(a)
    return _mm(a0, b_bf16) + (_mm(a1, b_bf16) + _mm(a2, b_bf16))


def _dot_exact_lhs(a_bf16, b):
    b0, b1, b2 = _split3(b)
    return _mm(a_bf16, b0) + (_mm(a_bf16, b1) + _mm(a_bf16, b2))


def _rms(x, g):
    return x * lax.rsqrt(jnp.mean(x * x, axis=-1, keepdims=True) + EPS) * g


def _sigmoid(x):
    return 0.5 + 0.5 * jnp.tanh(0.5 * x)


def _silu(x):
    return x * _sigmoid(x)


def _softplus(x):
    return jnp.maximum(x, 0.0) + jnp.log(1.0 + jnp.exp(-jnp.abs(x)))


def _iota2(shape, dim):
    return lax.broadcasted_iota(jnp.int32, shape, dim)


def _group_ones(width, group):
    r = _iota2((width, width), 0) // group
    c = _iota2((width, width), 1) // group
    return jnp.where(r == c, 1.0, 0.0).astype(BF16)


def _head_expand(width):
    r = _iota2((LANES, width), 0)
    c = _iota2((LANES, width), 1) // HEAD
    return jnp.where(r == c, 1.0, 0.0).astype(BF16)


def _chunk_tril(tb, chunk):
    r = _iota2((tb, tb), 0)
    c = _iota2((tb, tb), 1)
    return jnp.where(((r // chunk) == (c // chunk)) & (c <= r), 1.0, 0.0).astype(BF16)


def _pad_rows(x, rows):
    if x.shape[0] == rows:
        return x
    return jnp.concatenate([x, jnp.zeros((rows - x.shape[0], x.shape[1]), x.dtype)], axis=0)


def _shift_rows(x, prev, i):
    rolled = pltpu.roll(x, i, 0)
    first = jnp.where(_iota2((SUBLANES, x.shape[1]), 0) < i, pltpu.roll(prev, i, 0),
                      rolled[:SUBLANES])
    return jnp.concatenate([first, rolled[SUBLANES:]], axis=0)


def _bd(y, left):
    z = jnp.zeros_like(y)
    return jnp.concatenate([jnp.where(left, y, z), jnp.where(left, z, y)], axis=0)


def _diag(r, left):
    return jnp.where(left, r[:HEAD], r[HEAD:])


BASE = 16


def _bd8(y):
    g = _iota2((BASE, LANES), 1) // BASE
    z = jnp.zeros_like(y)
    return jnp.concatenate([jnp.where(g == i, y, z) for i in range(LANES // BASE)], axis=0)


def _tri_inv_sbs(a_list, left, tick=lambda: None):
    r = _iota2((CORE, LANES), 0)
    c = _iota2((CORE, LANES), 1) % HEAD
    nb = CORE // BASE
    lane16 = _iota2((BASE, LANES), 1)
    blk_of_lane = (lane16 % HEAD) // BASE
    eye16 = jnp.where(_iota2((BASE, LANES), 0) == lane16 % BASE, 1.0, 0.0).astype(F32)

    def diag_blocks(a):
        d = a[(nb - 1) * BASE:]
        for b in range(nb - 2, -1, -1):
            d = jnp.where(blk_of_lane == b, a[b * BASE:(b + 1) * BASE], d)
        return d

    d0f = [diag_blocks(a) for a in a_list]
    d0 = [_bf(x) for x in d0f]
    pw = [_bf(_mm(x, _bd8(x))) for x in d0]
    tick()
    m = [eye16 - x for x in d0f]
    for _ in range(2):
        bdp = [_bd8(x) for x in pw]
        both = [_mm(jnp.concatenate([x, _bf(mi)], axis=0), y) for x, mi, y in zip(pw, m, bdp)]
        tick()
        pw = [_bf(t[:BASE]) for t in both]
        m = [mi + t[BASE:] for mi, t in zip(m, both)]
    m = [mi + _mm(_bf(mi), _bd8(x)) for mi, x in zip(m, pw)]
    tick()
    m = [jnp.concatenate([jnp.where(blk_of_lane == b, mi, 0.0) for b in range(nb)], axis=0) for mi in m]
    size = 2 * BASE
    while size <= CORE:
        half = size // 2
        nh = CORE // half
        sel = ((r // size) == (c // size)) & ((r // half) != (c // half))
        bdb = [_bd(_bf(jnp.where(sel, a, 0.0)), left) for a in a_list]
        mb = [_bf(mi) for mi in m]
        low = [jnp.concatenate([x[b * half:(b + 1) * half] for b in range(1, nh, 2)], axis=0) for x in mb]
        t = [_bf(_mm(x, y)) for x, y in zip(low, bdb)]
        tick()
        upd = [_mm(x, _bd(y, left)) for x, y in zip(t, mb)]
        tick()
        m = [jnp.concatenate([mi[b * half:(b + 1) * half] - u[(b // 2) * half:(b // 2 + 1) * half]
                              if b % 2 else mi[b * half:(b + 1) * half] for b in range(nh)], axis=0)
             for mi, u in zip(m, upd)]
        size *= 2
    return m


def _const_spec(shape):
    nd = len(shape)
    return pl.BlockSpec(shape, lambda *_: (0,) * nd, pipeline_mode=pl.Buffered(1))


def _ffn_body(x_ref, g_ref, w13_ref, w2_ref, gn_ref, *o_refs, tail):
    x = x_ref[...]
    h = _rms(x, g_ref[...]).astype(BF16)
    acc = None
    for c in range(D_FF // FF_CHUNK):
        lo = c * FF_CHUNK
        a = jnp.dot(h, w13_ref[:, lo:lo + FF_CHUNK], preferred_element_type=F32)
        b = jnp.dot(h, w13_ref[:, D_FF + lo:D_FF + lo + FF_CHUNK], preferred_element_type=F32)
        part = jnp.dot((_silu(a) * b).astype(BF16), w2_ref[lo:lo + FF_CHUNK, :],
                       preferred_element_type=F32)
        acc = part if acc is None else acc + part
    y = x + 0.5 * acc
    if tail == "normed":
        o_refs[0][...] = _rms(y, gn_ref[...])
    else:
        o_refs[0][...] = y
    if tail == "with_h":
        o_refs[1][...] = _rms(y, gn_ref[...]).astype(BF16)


def _ffn(x2d, g, w13, w2, gn, tail):
    m = x2d.shape[0]
    tm = min(FFN_TILE, m)
    tok = pl.BlockSpec((tm, D_MODEL), lambda i: (i, 0))
    out_specs = [tok]
    out_shape = [jax.ShapeDtypeStruct((m, D_MODEL), F32)]
    if tail == "with_h":
        out_specs.append(tok)
        out_shape.append(jax.ShapeDtypeStruct((m, D_MODEL), BF16))
    return pl.pallas_call(
        functools.partial(_ffn_body, tail=tail),
        grid=(m // tm,),
        in_specs=[
            tok,
            _const_spec((1, D_MODEL)),
            _const_spec((D_MODEL, 2 * D_FF)),
            _const_spec((D_FF, D_MODEL)),
            _const_spec((1, D_MODEL)),
        ],
        out_specs=out_specs,
        out_shape=out_shape,
        compiler_params=pltpu.CompilerParams(
            dimension_semantics=("arbitrary",), vmem_limit_bytes=VMEM_LIMIT),
        name="ffn",
    )(x2d, g, w13, w2, gn)


RET_PAIRS = RET_HEADS // 2
RET_LOG_G = [math.log1p(-(2.0 ** (-5.0 - hh))) for hh in range(RET_HEADS)]


def _ret_body(h_ref, w_ref, cos_ref, sin_ref, gn_ref, s0_ref,
              o_ref, sn_ref, s_scr, q_scr, k_scr, v_scr, o_scr, *, tb):
    j = pl.program_id(1)
    tbp = max(tb, CORE)
    nc = tbp // CORE
    grp = _chunk_groups(nc)
    valid = min(tb, CORE)
    zero_blk = jnp.zeros((RET_DK, RET_DV), F32)

    @pl.when(j == 0)
    def _():
        for p in range(RET_PAIRS):
            s_scr[p] = jnp.concatenate(
                [jnp.concatenate([s0_ref[2 * p], zero_blk], axis=1),
                 jnp.concatenate([zero_blk, s0_ref[2 * p + 1]], axis=1)], axis=0)

    z = jnp.dot(h_ref[...], w_ref[...], preferred_element_type=F32)
    q = z[:, :RET_QK]
    k = z[:, RET_QK:2 * RET_QK]
    cos = cos_ref[...]
    sin = sin_ref[...]
    first = (_iota2((tb, RET_QK), 1) % RET_DK) < (RET_DK // 2)

    def rot(t):
        swapped = jnp.where(first, pltpu.roll(t, RET_QK - RET_DK // 2, 1),
                            pltpu.roll(t, RET_DK // 2, 1))
        return t * cos + swapped * sin

    q_scr[...] = _pad_rows(rot(q), tbp)
    k_scr[...] = _pad_rows(rot(k) * (RET_DK ** -0.5), tbp)
    v_scr[...] = _pad_rows(z[:, 2 * RET_QK:2 * RET_QK + RET_W], tbp)

    left, incl, _ = _core_masks()
    ri = _iota2((CORE, LANES), 0)
    ci = _iota2((CORE, LANES), 1) % HEAD
    rf = ri.astype(F32)
    dfi = jnp.where(incl, (ri - ci).astype(F32), 0.0)
    row2 = _iota2((2 * RET_DK, 2 * RET_DV), 0) < RET_DK
    col2 = _iota2((2 * RET_DK, 2 * RET_DV), 1) < RET_DV
    blockmask = row2 == col2
    vtop = _iota2((CORE, 2 * RET_DV), 1) < RET_DV
    dmask, qdec, kdec, cdec = [], [], [], []
    for p in range(RET_PAIRS):
        lg = jnp.where(left, RET_LOG_G[2 * p], RET_LOG_G[2 * p + 1])
        dmask.append(jnp.where(incl, jnp.exp(dfi * lg), 0.0))
        qdec.append(jnp.exp((rf + 1.0) * lg))
        kdec.append(jnp.exp((valid - 1.0 - rf) * lg))
        cdec.append(jnp.where(row2, math.exp(valid * RET_LOG_G[2 * p]),
                              math.exp(valid * RET_LOG_G[2 * p + 1])))
    psl = [slice(p * LANES, (p + 1) * LANES) for p in range(RET_PAIRS)]
    vsl = [slice(p * 2 * RET_DV, (p + 1) * 2 * RET_DV) for p in range(RET_PAIRS)]

    def body(i, carry):
        chunks = []
        for gi in range(grp):
            rows = pl.ds(pl.multiple_of((i * grp + gi) * CORE, CORE), CORE)
            chunks.append((rows, q_scr[rows, :], k_scr[rows, :], v_scr[rows, :]))
        items = [(ch, p) for ch in chunks for p in range(RET_PAIRS)]
        scores = [_mm(_bf(qc[:, psl[p]]), _bd(_bf(kc[:, psl[p]]), left), NT) * dmask[p]
                  for (_, qc, kc, _), p in items]
        kv = [_mm(_bf(kc[:, psl[p]] * kdec[p]), _bf(vc[:, vsl[p]]), TN)
              for (_, _, kc, vc), p in items]
        state = [s_scr[p] for p in range(RET_PAIRS)]
        for n, ((rows, qc, _, vc), p) in enumerate(items):
            vp = _bf(vc[:, vsl[p]])
            zv = jnp.zeros_like(vp)
            rhs = jnp.concatenate([jnp.where(vtop, vp, zv), jnp.where(vtop, zv, vp), _bf(state[p])],
                                  axis=0)
            lhs = _bf(jnp.concatenate([scores[n], qc[:, psl[p]] * qdec[p]], axis=1))
            o_scr[rows, vsl[p]] = _mm(lhs, rhs)
            state[p] = cdec[p] * state[p] + jnp.where(blockmask, kv[n], 0.0)
        for p in range(RET_PAIRS):
            s_scr[p] = state[p]
        return carry

    lax.fori_loop(0, nc // grp, body, 0)

    gate = z[:, 2 * RET_QK + RET_W:]
    gn = gn_ref[...]
    for hh in range(RET_HEADS):
        sl = slice(hh * RET_DV, (hh + 1) * RET_DV)
        oh = o_scr[pl.ds(0, tb), sl]
        mu = jnp.mean(oh, axis=-1, keepdims=True)
        dlt = oh - mu
        var = jnp.mean(dlt * dlt, axis=-1, keepdims=True)
        y = dlt * lax.rsqrt(var + 1e-5) * gn[:, sl]
        o_ref[:, sl] = (y * _silu(gate[:, sl])).astype(BF16)

    @pl.when(j == pl.num_programs(1) - 1)
    def _():
        for p in range(RET_PAIRS):
            s = s_scr[p]
            sn_ref[2 * p] = s[:RET_DK, :RET_DV]
            sn_ref[2 * p + 1] = s[RET_DK:, RET_DV:]


def _retention(h, w, cos, sin, gn, s0, tb):
    b, l, _ = h.shape
    st = (RET_HEADS, RET_DK, RET_DV)
    tbp = max(tb, CORE)
    return pl.pallas_call(
        functools.partial(_ret_body, tb=tb),
        grid=(b, l // tb),
        in_specs=[
            pl.BlockSpec((None, tb, D_MODEL), lambda i, j: (i, j, 0)),
            _const_spec(w.shape),
            pl.BlockSpec((tb, RET_QK), lambda i, j: (j, 0)),
            pl.BlockSpec((tb, RET_QK), lambda i, j: (j, 0)),
            _const_spec((1, RET_W)),
            pl.BlockSpec((None,) + st, lambda i, j: (i, 0, 0, 0)),
        ],
        out_specs=[
            pl.BlockSpec((None, tb, RET_W), lambda i, j: (i, j, 0)),
            pl.BlockSpec((None,) + st, lambda i, j: (i, 0, 0, 0)),
        ],
        out_shape=[jax.ShapeDtypeStruct((b, l, RET_W), BF16),
                   jax.ShapeDtypeStruct((b,) + st, F32)],
        scratch_shapes=[
            pltpu.VMEM((RET_PAIRS, 2 * RET_DK, 2 * RET_DV), F32),
            pltpu.VMEM((tbp, RET_QK), F32),
            pltpu.VMEM((tbp, RET_QK), F32),
            pltpu.VMEM((tbp, RET_W), F32),
            pltpu.VMEM((tbp, RET_W), F32),
        ],
        compiler_params=pltpu.CompilerParams(
            dimension_semantics=("arbitrary", "arbitrary"), vmem_limit_bytes=VMEM_LIMIT),
        name="retention",
    )(h, w, cos, sin, gn, s0)


HALO = SUBLANES


def _core_masks():
    r = _iota2((CORE, LANES), 0)
    c = _iota2((CORE, LANES), 1) % HEAD
    left = _iota2((CORE, LANES), 1) < HEAD
    return left, r >= c, r > c


def _load_state_sbs(s0_ref, s_scr):
    for p in range(PAIRS):
        s_scr[p] = jnp.concatenate([s0_ref[2 * p], s0_ref[2 * p + 1]], axis=1)


def _store_state_sbs(s_scr, sn_ref):
    for p in range(PAIRS):
        s = s_scr[p]
        sn_ref[2 * p] = s[:, :HEAD]
        sn_ref[2 * p + 1] = s[:, HEAD:]


def _chunk_groups(nc):
    return 8 if nc % 8 == 0 else (4 if nc % 4 == 0 else (2 if nc % 2 == 0 else 1))


SLAB = 128


def _pipe_maps(nblk, total):
    def in_blk(t):
        ta = jnp.minimum(t, total - 1)
        return ta // nblk, ta % nblk

    def out_blk(t):
        tp = jnp.maximum(t - 1, 0)
        return tp // nblk, tp % nblk

    return in_blk, out_blk


def _gdn_pipe_body(h_ref, w_ref, cw_ref, alog_ref, dtb_ref, nrm_ref, s0_ref, c0_ref,
                   o_ref, sn_ref, cn_ref,
                   s_scr, xp_scr, q_scr, k_scr, v_scr, be_scr, ga_scr, gate_scr, gar_scr,
                   o_scr, m_scr, n_scr, qp_scr, egl_scr, *, tb, nblk):
    t = pl.program_id(0)
    total = pl.num_programs(0) - 1
    ja = jnp.minimum(t, total - 1) % nblk
    jp = jnp.maximum(t - 1, 0) % nblk
    wslot = t % 2
    rslot = 1 - wslot
    tbp = max(tb, CORE)
    nc = tbp // CORE
    slab = min(SLAB, tb)
    slabp = max(slab, CORE)
    nslab = tb // slab

    @pl.when(t == 0)
    def _():
        for ref in (q_scr, k_scr, v_scr, be_scr, ga_scr, gate_scr, gar_scr):
            ref[...] = jnp.zeros_like(ref)

    @pl.when(ja == 0)
    def _():
        xp_scr[...] = jnp.zeros_like(xp_scr)
        xp_scr[pl.ds(HALO - (CONV_W - 1), CONV_W - 1), :] = c0_ref[...]

    @pl.when(jp == 0)
    def _():
        _load_state_sbs(s0_ref, s_scr)

    gones = _group_ones(GDN_QK, GDN_DK)
    expand = _head_expand(GDN_V)
    tril = _chunk_tril(slabp, CORE)
    ncp = gar_scr.shape[1]
    rr = _iota2((slabp, GDN_V), 0)
    cc = _iota2((slabp, GDN_V), 1)
    on_diag = (rr % CORE) == (cc % HEAD)

    def input_stage():
        prev = xp_scr[...]
        garv = jnp.zeros((ncp, GDN_V), F32)
        for s in range(nslab):
            z = jnp.dot(h_ref[pl.ds(s * slab, slab), :], w_ref[...], preferred_element_type=F32)
            yield
            zq = z[:, :GDN_CONV_CH]
            cw = cw_ref[...]
            y = zq * cw[CONV_W - 1:CONV_W, :]
            for i in range(1, CONV_W):
                y = y + _shift_rows(zq, prev, i) * cw[CONV_W - 1 - i:CONV_W - i, :]
            prev = zq[slab - HALO:, :]
            yield
            qkv = _silu(y)
            q = qkv[:, :GDN_QK]
            k = qkv[:, GDN_QK:2 * GDN_QK]
            q = _pad_rows(q * lax.rsqrt(_dot1(q * q, gones) + 1e-6) * (GDN_DK ** -0.5), slabp)
            k = _pad_rows(k * lax.rsqrt(_dot1(k * k, gones) + 1e-6), slabp)
            v = _pad_rows(qkv[:, 2 * GDN_QK:], slabp)
            yield
            zb = z[:, GDN_CONV_CH + GDN_V:GDN_CONV_CH + GDN_V + LANES]
            za = z[:, GDN_CONV_CH + GDN_V + LANES:]
            beta = _pad_rows(_sigmoid(zb), slabp)
            log_a = _pad_rows(-jnp.exp(alog_ref[...]) * _softplus(za + dtb_ref[...]), slabp)
            ga = _dot_exact_lhs(tril, log_a)
            gae = _dot_exact_rhs(ga, expand)
            bee = _dot_exact_rhs(beta, expand)
            pick = jnp.where(on_diag, gae, 0.0)
            chunk_of = jnp.where(_iota2((ncp, slabp), 0)
                                 == _iota2((ncp, slabp), 1) // CORE + s * (slabp // CORE), 1.0, 0.0)
            garv = garv + _dot_exact_lhs(_bf(chunk_of), pick)
            yield
            rows = pl.ds(s * slabp, slabp)
            q_scr[wslot, rows, :] = q
            k_scr[wslot, rows, :] = k
            v_scr[wslot, rows, :] = v
            be_scr[wslot, rows, :] = bee
            ga_scr[wslot, rows, :] = gae
            gate_scr[wslot, pl.ds(s * slab, slab), :] = z[:, GDN_CONV_CH:GDN_CONV_CH + GDN_V]
            yield
        xp_scr[...] = prev
        gar_scr[wslot] = garv
        yield

    gen = input_stage()

    def tick():
        next(gen, None)

    left, incl, strict = _core_masks()
    incl4 = jnp.concatenate([incl] * PAIRS, axis=1)
    psl = [slice(p * LANES, (p + 1) * LANES) for p in range(PAIRS)]

    def phase1(chunks, tick):
        probs = []
        for c in chunks:
            rows = pl.ds(c * CORE, CORE)
            gac = ga_scr[rslot, rows, :]
            glast = gac[CORE - 1:CORE, :]
            egl_scr[pl.ds(c, 1), :] = jnp.exp(glast)
            eg = jnp.exp(gac)
            gam = jnp.where(incl4, jnp.exp(jnp.where(incl4, gac - gar_scr[rslot, pl.ds(c, 1), :], 0.0)),
                            0.0)
            qc = q_scr[rslot, rows, :]
            kc = k_scr[rslot, rows, :]
            bec = be_scr[rslot, rows, :]
            probs.append(dict(rows=rows, gam=gam, be=bec, q=_bf(qc), k=_bf(kc), qe=qc * eg,
                              kd=_bf(kc * jnp.exp(glast - gac)), bv=_bf(bec * v_scr[rslot, rows, :]),
                              bek=_bf(bec * eg * kc)))
        tick()
        items = [(pr, s) for pr in probs for s in psl]
        qk = [_mm(jnp.concatenate([pr["q"][:, s], pr["k"][:, s]], axis=0), _bd(pr["k"][:, s], left), NT)
              for pr, s in items]
        tick()
        pm = [_bf(x[:CORE] * pr["gam"][:, s]) for x, (pr, s) in zip(qk, items)]
        amat = [jnp.where(strict, pr["be"][:, s] * x[CORE:] * pr["gam"][:, s], 0.0)
                for x, (pr, s) in zip(qk, items)]
        tinv = _tri_inv_sbs(amat, left, tick)
        sol = [_bf(_mm(_bf(tm), jnp.concatenate([_bd(pr["bv"][:, s], left), _bd(pr["bek"][:, s], left)],
                                                axis=1)))
               for tm, (pr, s) in zip(tinv, items)]
        tick()
        kts = [_mm(pr["kd"][:, s], x, TN) for x, (pr, s) in zip(sol, items)]
        tick()
        ps = [_mm(p_, jnp.concatenate([_bd(x[:, :LANES], left), _bd(x[:, LANES:], left)], axis=1))
              for p_, x in zip(pm, sol)]
        tick()
        for n, (pr, s) in enumerate(items):
            n_scr[pr["rows"], s] = _diag(kts[n][:, :LANES], left)
            m_scr[pr["rows"], s] = _diag(kts[n][:, LANES:], left)
            o_scr[pr["rows"], s] = ps[n][:, :LANES]
            qp_scr[pr["rows"], s] = pr["qe"][:, s] - ps[n][:, LANES:]

    phase1(range(nc), tick)

    for c in range(nc):
        rows = pl.ds(c * CORE, CORE)
        mc = _bf(m_scr[rows, :])
        nn = n_scr[rows, :]
        qpc = _bf(qp_scr[rows, :])
        eglc = egl_scr[pl.ds(c, 1), :]
        s_old = [s_scr[p] for p in range(PAIRS)]
        res = [_mm(jnp.concatenate([mc[:, psl[p]], qpc[:, psl[p]]], axis=0), _bd(_bf(s_old[p]), left))
               for p in range(PAIRS)]
        for p in range(PAIRS):
            s_scr[p] = eglc[:, psl[p]] * s_old[p] - res[p][:CORE] + nn[:, psl[p]]
        o_scr[rows, :] = o_scr[rows, :] + jnp.concatenate([x[CORE:] for x in res], axis=1)
        tick()
    for _ in gen:
        pass

    o = o_scr[pl.ds(0, tb), :]
    ms = _dot1(o * o, gones) * (1.0 / GDN_DV)
    gate = gate_scr[rslot]
    o_ref[...] = (o * lax.rsqrt(ms + EPS) * nrm_ref[...] * _silu(gate)).astype(BF16)

    @pl.when((jp == nblk - 1) & (t > 0))
    def _():
        _store_state_sbs(s_scr, sn_ref)

    @pl.when(ja == nblk - 1)
    def _():
        cn_ref[...] = xp_scr[pl.ds(HALO - (CONV_W - 1), CONV_W - 1), :]


def _gdn_pipe(h, w, cw, alog, dtb, nrm, s0, c0, tb):
    b, l, _ = h.shape
    nblk = l // tb
    total = b * nblk
    in_blk, out_blk = _pipe_maps(nblk, total)
    st = (GDN_HEADS, GDN_DK, GDN_DV)
    cst = (CONV_W - 1, GDN_CONV_CH)
    tbp = max(tb, CORE)
    dbl = pltpu.VMEM((2, tbp, GDN_V), F32)
    big = pltpu.VMEM((tbp, GDN_V), F32)
    ncp = max(tbp // CORE, SUBLANES)
    return pl.pallas_call(
        functools.partial(_gdn_pipe_body, tb=tb, nblk=nblk),
        grid=(total + 1,),
        in_specs=[
            pl.BlockSpec((None, tb, D_MODEL), lambda t: in_blk(t) + (0,)),
            _const_spec(w.shape),
            _const_spec(cw.shape),
            _const_spec((1, LANES)),
            _const_spec((1, LANES)),
            _const_spec((1, GDN_V)),
            pl.BlockSpec((None,) + st, lambda t: (out_blk(t)[0], 0, 0, 0)),
            pl.BlockSpec((None,) + cst, lambda t: (in_blk(t)[0], 0, 0)),
        ],
        out_specs=[
            pl.BlockSpec((None, tb, GDN_V), lambda t: out_blk(t) + (0,)),
            pl.BlockSpec((None,) + st, lambda t: (out_blk(t)[0], 0, 0, 0)),
            pl.BlockSpec((None,) + cst, lambda t: (in_blk(t)[0], 0, 0)),
        ],
        out_shape=[jax.ShapeDtypeStruct((b, l, GDN_V), BF16),
                   jax.ShapeDtypeStruct((b,) + st, F32),
                   jax.ShapeDtypeStruct((b,) + cst, F32)],
        scratch_shapes=[
            pltpu.VMEM((PAIRS, HEAD, LANES), F32),
            pltpu.VMEM((HALO, GDN_CONV_CH), F32),
            dbl, dbl, dbl, dbl, dbl,
            pltpu.VMEM((2, tb, GDN_V), F32),
            pltpu.VMEM((2, ncp, GDN_V), F32),
            big, big, big, big,
            pltpu.VMEM((ncp, GDN_V), F32),
        ],
        compiler_params=pltpu.CompilerParams(
            dimension_semantics=("arbitrary",), vmem_limit_bytes=VMEM_LIMIT),
        name="gdn",
    )(h, w, cw, alog, dtb, nrm, s0, c0)


def _rwkv_pipe_body(h_ref, w_ref, mu_ref, w0_ref, w2_ref, a0_ref, a2_ref, g2_ref,
                    kkw_ref, ka_ref, rk_ref, gn_ref, s0_ref, sh0_ref,
                    o_ref, sn_ref, shn_ref,
                    s_scr, zc_scr, r_scr, k2_scr, v_scr, kk_scr, kka_scr, lw_scr, cum_scr,
                    bonus_scr, g_scr, o_scr, qp_scr, m_scr, n_scr, wt_scr, *, tb, nblk):
    t = pl.program_id(0)
    total = pl.num_programs(0) - 1
    ja = jnp.minimum(t, total - 1) % nblk
    jp = jnp.maximum(t - 1, 0) % nblk
    wslot = t % 2
    rslot = 1 - wslot
    tbp = max(tb, CORE)
    nc = tbp // CORE
    slab = min(SLAB, tb)
    slabp = max(slab, CORE)
    nslab = tb // slab

    @pl.when(t == 0)
    def _():
        for ref in (r_scr, k2_scr, v_scr, kk_scr, kka_scr, lw_scr, cum_scr, bonus_scr, g_scr):
            ref[...] = jnp.zeros_like(ref)

    @pl.when(ja == 0)
    def _():
        zc_scr[...] = jnp.zeros_like(zc_scr)
        zc_scr[pl.ds(HALO - 1, 1), :] = sh0_ref[...]

    @pl.when(jp == 0)
    def _():
        _load_state_sbs(s0_ref, s_scr)

    gones = _group_ones(RWKV_W, RWKV_N)
    tril = _chunk_tril(slabp, CORE)

    def input_stage():
        prev = zc_scr[...]
        for s in range(nslab):
            z = jnp.dot(h_ref[pl.ds(s * slab, slab), :], w_ref[...], preferred_element_type=F32)
            yield
            cs = z + (_shift_rows(z, prev, 1) - z) * mu_ref[...]
            prev = z[slab - HALO:, :]
            yield
            r = cs[:, :RWKV_W]
            k = cs[:, RWKV_W:2 * RWKV_W]
            v = cs[:, 2 * RWKV_W:3 * RWKV_W]
            zwa = cs[:, 3 * RWKV_W:3 * RWKV_W + LANES]
            zg = cs[:, 3 * RWKV_W + LANES:]
            w_raw = -_softplus(-(w0_ref[...] + _dot1(jnp.tanh(zwa), w2_ref[...]))) - 0.5
            lw = _pad_rows(-jnp.exp(w_raw), slabp)
            a = _sigmoid(a0_ref[...] + _dot1(zwa, a2_ref[...]))
            g = _dot1(_sigmoid(zg), g2_ref[...])
            yield
            kkn = k * kkw_ref[...]
            kkn = kkn * lax.rsqrt(_dot1(kkn * kkn, gones) + 1e-6)
            k2 = k * (1.0 + (a - 1.0) * ka_ref[...])
            bonus = _dot1(r * k2 * rk_ref[...], gones) * v
            yield
            rows = pl.ds(s * slabp, slabp)
            r_scr[wslot, rows, :] = _pad_rows(r, slabp)
            k2_scr[wslot, rows, :] = _pad_rows(k2, slabp)
            v_scr[wslot, rows, :] = _pad_rows(v, slabp)
            kk_scr[wslot, rows, :] = _pad_rows(kkn, slabp)
            kka_scr[wslot, rows, :] = _pad_rows(kkn * a, slabp)
            lw_scr[wslot, rows, :] = lw
            cum_scr[wslot, rows, :] = _dot_exact_lhs(tril, lw)
            bonus_scr[wslot, pl.ds(s * slab, slab), :] = bonus
            g_scr[wslot, pl.ds(s * slab, slab), :] = g
            yield
        zc_scr[...] = prev
        yield

    gen = input_stage()

    def tick():
        next(gen, None)

    left, incl, strict = _core_masks()
    psl = [slice(p * LANES, (p + 1) * LANES) for p in range(PAIRS)]

    probs = []
    for c in range(nc):
        rows = pl.ds(c * CORE, CORE)
        cum = cum_scr[rslot, rows, :]
        tot = cum[CORE - 1:CORE, :]
        wt_scr[pl.ds(c, 1), :] = jnp.exp(tot)
        w_inv = jnp.exp(-cum)
        w_rem = jnp.exp(tot - cum)
        kkac = kka_scr[rslot, rows, :]
        k2c = k2_scr[rslot, rows, :]
        rt = r_scr[rslot, rows, :] * jnp.exp(cum)
        probs.append(dict(rows=rows, rt=rt, rtb=_bf(rt),
                          at=_bf(-kk_scr[rslot, rows, :] * jnp.exp(cum - lw_scr[rslot, rows, :])),
                          bt=_bf(kkac * w_inv), kt=_bf(k2c * w_inv), bh=_bf(kkac * w_rem),
                          kh=_bf(k2c * w_rem), v=_bf(v_scr[rslot, rows, :])))
    tick()
    items = [(pr, s) for pr in probs for s in psl]
    x = [_mm(jnp.concatenate([pr["at"][:, s], pr["rtb"][:, s]], axis=0),
             jnp.concatenate([_bd(pr["bt"][:, s], left), _bd(pr["kt"][:, s], left)], axis=0), NT)
         for pr, s in items]
    tick()
    neg_ab = [jnp.where(strict, -tt[:CORE, :LANES], 0.0) for tt in x]
    a_rb = [_bf(jnp.where(incl, tt[CORE:, :LANES], 0.0)) for tt in x]
    a_ak = [_bf(jnp.where(strict, tt[:CORE, LANES:], 0.0)) for tt in x]
    a_rk = [_bf(jnp.where(incl, tt[CORE:, LANES:], 0.0)) for tt in x]
    tinv = _tri_inv_sbs(neg_ab, left, tick)
    both_v = [_mm(jnp.concatenate([p_, q_], axis=0), _bd(pr["v"][:, s], left))
              for p_, q_, (pr, s) in zip(a_ak, a_rk, items)]
    akv = [_bf(t_[:CORE]) for t_ in both_v]
    rkv = [t_[CORE:] for t_ in both_v]
    tick()
    pu = [_bf(_mm(_bf(tm), jnp.concatenate([_bd(pr["at"][:, s], left), _bd(u, left)], axis=1)))
          for tm, u, (pr, s) in zip(tinv, akv, items)]
    tick()
    rbp = [_mm(p_, jnp.concatenate([_bd(u[:, :LANES], left), _bd(u[:, LANES:], left)], axis=1))
           for p_, u in zip(a_rb, pu)]
    tick()
    tick()
    bp =[_mm(u, pr["bh"][:, s], TN) for u, (pr, s) in zip(pu, items)]
    tick()
    vk = [_mm(pr["v"][:, s], pr["kh"][:, s], TN) for pr, s in items]
    tick()
    for n, (pr, s) in enumerate(items):
        qp_scr[pr["rows"], s] = pr["rt"][:, s] + rbp[n][:, :LANES]
        o_scr[pr["rows"], s] = rbp[n][:, LANES:] + rkv[n]
        m_scr[pr["rows"], s] = _diag(bp[n][:LANES], left)
        n_scr[pr["rows"], s] = _diag(bp[n][LANES:], left) + _diag(vk[n], left)

    for c in range(nc):
        rows = pl.ds(c * CORE, CORE)
        mc = _bf(m_scr[rows, :])
        nn = n_scr[rows, :]
        qpc = _bf(qp_scr[rows, :])
        wtc = wt_scr[pl.ds(c, 1), :]
        s_old = [s_scr[p] for p in range(PAIRS)]
        sb = [_bf(s) for s in s_old]
        od = [_mm(qpc[:, psl[p]], _bd(sb[p], left), NT) for p in range(PAIRS)]
        sm = [_mm(sb[p], _bd(mc[:, psl[p]], left)) for p in range(PAIRS)]
        for p in range(PAIRS):
            s_scr[p] = s_old[p] * wtc[:, psl[p]] + sm[p] + nn[:, psl[p]]
        o_scr[rows, :] = o_scr[rows, :] + jnp.concatenate(od, axis=1)
        tick()
    for _ in gen:
        pass

    o = o_scr[pl.ds(0, tb), :]
    mean = _dot1(o, gones) * (1.0 / RWKV_N)
    dlt = o - mean
    var = _dot1(dlt * dlt, gones) * (1.0 / RWKV_N)
    y = dlt * lax.rsqrt(var + RWKV_GN_EPS) * gn_ref[...]
    o_ref[...] = ((y + bonus_scr[rslot]) * g_scr[rslot]).astype(BF16)

    @pl.when((jp == nblk - 1) & (t > 0))
    def _():
        _store_state_sbs(s_scr, sn_ref)

    @pl.when(ja == nblk - 1)
    def _():
        shn_ref[...] = zc_scr[pl.ds(HALO - 1, 1), :]


def _rwkv_pipe(h, w, mu, w0, w2, a0, a2, g2, kkw, ka, rk, gn, s0, sh0, tb):
    b, l, _ = h.shape
    nblk = l // tb
    total = b * nblk
    in_blk, out_blk = _pipe_maps(nblk, total)
    st = (RWKV_HEADS, RWKV_N, RWKV_N)
    sst = (1, RWKV_SHIFT_CH)
    tbp = max(tb, CORE)
    row = _const_spec((1, RWKV_W))
    dbl = pltpu.VMEM((2, tbp, RWKV_W), F32)
    dbl_tb = pltpu.VMEM((2, tb, RWKV_W), F32)
    big = pltpu.VMEM((tbp, RWKV_W), F32)
    return pl.pallas_call(
        functools.partial(_rwkv_pipe_body, tb=tb, nblk=nblk),
        grid=(total + 1,),
        in_specs=[
            pl.BlockSpec((None, tb, D_MODEL), lambda t: in_blk(t) + (0,)),
            _const_spec(w.shape),
            _const_spec((1, RWKV_SHIFT_CH)),
            row, _const_spec((LANES, RWKV_W)), row, _const_spec((LANES, RWKV_W)),
            _const_spec((LANES, RWKV_W)),
            row, row, row, row,
            pl.BlockSpec((None,) + st, lambda t: (out_blk(t)[0], 0, 0, 0)),
            pl.BlockSpec((None,) + sst, lambda t: (in_blk(t)[0], 0, 0)),
        ],
        out_specs=[
            pl.BlockSpec((None, tb, RWKV_W), lambda t: out_blk(t) + (0,)),
            pl.BlockSpec((None,) + st, lambda t: (out_blk(t)[0], 0, 0, 0)),
            pl.BlockSpec((None,) + sst, lambda t: (in_blk(t)[0], 0, 0)),
        ],
        out_shape=[jax.ShapeDtypeStruct((b, l, RWKV_W), BF16),
                   jax.ShapeDtypeStruct((b,) + st, F32),
                   jax.ShapeDtypeStruct((b,) + sst, F32)],
        scratch_shapes=[
            pltpu.VMEM((PAIRS, HEAD, LANES), F32),
            pltpu.VMEM((HALO, RWKV_SHIFT_CH), F32),
            dbl, dbl, dbl, dbl, dbl, dbl, dbl,
            dbl_tb, dbl_tb,
            big, big, big, big,
            pltpu.VMEM((max(tbp // CORE, SUBLANES), RWKV_W), F32),
        ],
        compiler_params=pltpu.CompilerParams(
            dimension_semantics=("arbitrary",), vmem_limit_bytes=VMEM_LIMIT),
        name="rwkv7",
    )(h, w, mu, w0, w2, a0, a2, g2, kkw, ka, rk, gn, s0, sh0)


def _merge_body(x_ref, h_ref, wg_ref, oa_ref, ob_ref, oc_ref, wb_ref, wo_ref, o_ref):
    x = x_ref[...]
    h = h_ref[...]
    m = None
    for i, br in enumerate((oa_ref, ob_ref, oc_ref)):
        gate = jnp.dot(h, wg_ref[:, i * D_MODEL:(i + 1) * D_MODEL], preferred_element_type=F32)
        t = _sigmoid(gate) * jnp.dot(br[...], wb_ref[i], preferred_element_type=F32)
        m = t if m is None else m + t
    o_ref[...] = x + jnp.dot(m.astype(BF16), wo_ref[...], preferred_element_type=F32)


def _merge(x2d, h2d, wg, oa, ob, oc, wb, wo):
    m = x2d.shape[0]
    tm = min(TOKEN_TILE, m)
    bw = oa.shape[1]
    tok = lambda w: pl.BlockSpec((tm, w), lambda i: (i, 0))
    return pl.pallas_call(
        _merge_body,
        grid=(m // tm,),
        in_specs=[tok(D_MODEL), tok(D_MODEL), _const_spec(wg.shape),
                  tok(bw), tok(bw), tok(bw), _const_spec(wb.shape), _const_spec(wo.shape)],
        out_specs=tok(D_MODEL),
        out_shape=jax.ShapeDtypeStruct((m, D_MODEL), F32),
        compiler_params=pltpu.CompilerParams(
            dimension_semantics=("arbitrary",), vmem_limit_bytes=VMEM_LIMIT),
        name="merge",
    )(x2d, h2d, wg, oa, ob, oc, wb, wo)


def _rope_tables(pos0, l):
    half = RET_DK // 2
    inv = ROPE_BASE ** (-jnp.arange(half, dtype=F32) / half)
    ang = (pos0 + jnp.arange(l)).astype(F32)[:, None] * inv[None, :]
    cos = jnp.cos(ang)
    sin = jnp.sin(ang)
    cos = jnp.tile(jnp.concatenate([cos, cos], axis=1), (1, RET_HEADS))
    sin = jnp.tile(jnp.concatenate([-sin, sin], axis=1), (1, RET_HEADS))
    return cos, sin


def _prep_weights(p):
    w_in = p["w_in"].astype(BF16)
    depth = w_in.shape[0]
    o = 0
    pieces = {}
    for name, size in (("a_qkvg", 2 * RET_QK + 2 * RET_W), ("b_qkv", GDN_CONV_CH), ("b_g", GDN_V),
                       ("b_beta", GDN_HEADS), ("b_alpha", GDN_HEADS), ("c_z", RWKV_SHIFT_CH),
                       ("gate", 3 * D_MODEL)):
        pieces[name] = w_in[:, :, o:o + size]
        o += size
    pad_l = lambda t, n: jnp.pad(t, ((0, 0), (0, 0), (0, n - t.shape[2])))
    row = lambda t: t[:, None, :]
    zeros64 = jnp.zeros((depth, 64, RWKV_W), F32)
    return dict(
        norm_ffn1=row(p["norm_ffn1"]), norm_mix=row(p["norm_mix"]), norm_ffn2=row(p["norm_ffn2"]),
        ffn1_w13=p["ffn1_w13"].astype(BF16), ffn1_w2=p["ffn1_w2"].astype(BF16),
        ffn2_w13=p["ffn2_w13"].astype(BF16), ffn2_w2=p["ffn2_w2"].astype(BF16),
        w_ret=pieces["a_qkvg"],
        w_gdn=jnp.concatenate([pieces["b_qkv"], pieces["b_g"], pad_l(pieces["b_beta"], LANES),
                               pad_l(pieces["b_alpha"], LANES)], axis=2),
        w_rwkv=pieces["c_z"],
        w_gate=pieces["gate"],
        ret_gn=row(p["ret_gn"]),
        gdn_conv=p["gdn_conv"],
        gdn_a_log=row(jnp.pad(p["gdn_a_log"], ((0, 0), (0, LANES - GDN_HEADS)))),
        gdn_dt_bias=row(jnp.pad(p["gdn_dt_bias"], ((0, 0), (0, LANES - GDN_HEADS)))),
        gdn_norm=row(jnp.tile(p["gdn_norm"], (1, GDN_HEADS))),
        rwkv_mu=row(p["rwkv_mu"]),
        rwkv_w0=row(p["rwkv_w0"]),
        rwkv_w2=jnp.concatenate([p["rwkv_w2"], zeros64], axis=1),
        rwkv_a0=row(p["rwkv_a0"]),
        rwkv_a2=jnp.concatenate([zeros64, p["rwkv_a2"]], axis=1),
        rwkv_g2=p["rwkv_g2"],
        rwkv_kk=row(p["rwkv_kk"]), rwkv_ka=row(p["rwkv_ka"]), rwkv_rk=row(p["rwkv_rk"]),
        rwkv_gn=row(p["rwkv_gn"]),
        w_branch=p["w_branch"].astype(BF16), w_out=p["w_out"].astype(BF16),
    )


def _layer(x, st, lp, tables, gf, final_norm, tb):
    b, l, d = x.shape
    s_ret, s_delta, s_conv, s_wkv, s_shift = st
    cos, sin = tables
    x2, h2 = _ffn(x.reshape(b * l, d), lp["norm_ffn1"], lp["ffn1_w13"], lp["ffn1_w2"],
                  lp["norm_mix"], "with_h")
    h3 = h2.reshape(b, l, d)
    oa, n_ret = _retention(h3, lp["w_ret"], cos, sin, lp["ret_gn"], s_ret, tb)
    ob, n_delta, n_conv = _gdn_pipe(h3, lp["w_gdn"], lp["gdn_conv"], lp["gdn_a_log"],
                               lp["gdn_dt_bias"], lp["gdn_norm"], s_delta, s_conv, tb)
    oc, n_wkv, n_shift = _rwkv_pipe(h3, lp["w_rwkv"], lp["rwkv_mu"], lp["rwkv_w0"],
                                    lp["rwkv_w2"], lp["rwkv_a0"], lp["rwkv_a2"], lp["rwkv_g2"],
                                    lp["rwkv_kk"], lp["rwkv_ka"], lp["rwkv_rk"], lp["rwkv_gn"],
                                    s_wkv, s_shift, tb)
    bw = oa.shape[-1]
    x4 = _merge(x2, h2, lp["w_gate"], oa.reshape(b * l, bw), ob.reshape(b * l, bw),
                oc.reshape(b * l, bw), lp["w_branch"], lp["w_out"])
    (x5,) = _ffn(x4, lp["norm_ffn2"], lp["ffn2_w13"], lp["ffn2_w2"], gf,
                 "normed" if final_norm else "plain")
    return x5.reshape(b, l, d), (n_ret, n_delta, n_conv, n_wkv, n_shift)


def _run_stream(x, pos0, states, wts, gf, tb):
    depth = wts["w_ret"].shape[0]
    l = x.shape[1]
    tb = min(tb, l)
    tables = _rope_tables(pos0, l)
    new = [[] for _ in range(5)]
    for li in range(depth):
        lp = {k: v[li] for k, v in wts.items()}
        st = tuple(s[li] for s in states)
        x, nst = _layer(x, st, lp, tables, gf, li == depth - 1, tb)
        for jdx in range(5):
            new[jdx].append(nst[jdx])
    return x, tuple(jnp.stack(t, axis=0) for t in new)


def kernel(x_prompt, x_sample, state_ret, state_delta, state_conv, state_wkv, state_shift,
           norm_ffn1, ffn1_w13, ffn1_w2, norm_mix, w_in, ret_gn, gdn_conv, gdn_a_log, gdn_dt_bias,
           gdn_norm, rwkv_mu, rwkv_w0, rwkv_w2, rwkv_a0, rwkv_a2, rwkv_g2, rwkv_kk, rwkv_ka, rwkv_rk,
           rwkv_gn, w_branch, w_out, norm_ffn2, ffn2_w13, ffn2_w2, norm_final):
    params = dict(
        norm_ffn1=norm_ffn1, ffn1_w13=ffn1_w13, ffn1_w2=ffn1_w2, norm_mix=norm_mix, w_in=w_in,
        ret_gn=ret_gn, gdn_conv=gdn_conv, gdn_a_log=gdn_a_log, gdn_dt_bias=gdn_dt_bias,
        gdn_norm=gdn_norm, rwkv_mu=rwkv_mu, rwkv_w0=rwkv_w0, rwkv_w2=rwkv_w2, rwkv_a0=rwkv_a0,
        rwkv_a2=rwkv_a2, rwkv_g2=rwkv_g2, rwkv_kk=rwkv_kk, rwkv_ka=rwkv_ka, rwkv_rk=rwkv_rk,
        rwkv_gn=rwkv_gn, w_branch=w_branch, w_out=w_out, norm_ffn2=norm_ffn2, ffn2_w13=ffn2_w13,
        ffn2_w2=ffn2_w2)
    wts = _prep_weights(params)
    gf = norm_final[None, :]
    depth = w_in.shape[0]
    bp = x_prompt.shape[0]
    zero_states = (
        jnp.zeros((depth, bp, RET_HEADS, RET_DK, RET_DV), F32),
        jnp.zeros((depth, bp, GDN_HEADS, GDN_DK, GDN_DV), F32),
        jnp.zeros((depth, bp, CONV_W - 1, GDN_CONV_CH), F32),
        jnp.zeros((depth, bp, RWKV_HEADS, RWKV_N, RWKV_N), F32),
        jnp.zeros((depth, bp, 1, RWKV_SHIFT_CH), F32),
    )
    y_p, st_p = _run_stream(x_prompt, 0, zero_states, wts, gf, SEQ_BLOCK)
    y_s, st_s = _run_stream(x_sample, PAST_LEN, (state_ret, state_delta, state_conv, state_wkv,
                                                 state_shift), wts, gf, SEQ_BLOCK)
    return (y_p, y_s) + st_p + st_s
```

```python
import functools
import math

import jax
import jax.numpy as jnp
from jax import lax
from jax.experimental import pallas as pl
from jax.experimental.pallas import tpu as pltpu

F32 = jnp.float32
BF16 = jnp.bfloat16

D_MODEL = 1024
D_FF = 2816
EPS = 1e-6
PAST_LEN = 2048
RET_HEADS, RET_DK, RET_DV = 4, 64, 128
ROPE_BASE = 10000.0
GDN_HEADS, GDN_DK, GDN_DV, CONV_W = 8, 64, 64, 4
RWKV_HEADS, RWKV_N = 8, 64
RWKV_GN_EPS = 64e-5
RET_QK = RET_HEADS * RET_DK
RET_W = RET_HEADS * RET_DV
GDN_QK = GDN_HEADS * GDN_DK
GDN_V = GDN_HEADS * GDN_DV
GDN_CONV_CH = 2 * GDN_QK + GDN_V
RWKV_W = RWKV_HEADS * RWKV_N
RWKV_SHIFT_CH = 3 * RWKV_W + 256
LANES = 128
SUBLANES = 8
VMEM_LIMIT = 56 * 1024 * 1024
HEAD = 64
PAIRS = 4
CORE = 64
SEQ_BLOCK = 512
TOKEN_TILE = 512
FFN_TILE = 1024
FF_CHUNK = 256
assert D_FF % FF_CHUNK == 0
assert GDN_DK == GDN_DV == RWKV_N == HEAD and GDN_HEADS == RWKV_HEADS == 2 * PAIRS


NT = (((1,), (1,)), ((), ()))
TN = (((0,), (0,)), ((), ()))
NN = (((1,), (0,)), ((), ()))


def _bf(x):
    return x.astype(BF16)


def _mm(a, b, dims=NN):
    return lax.dot_general(a, b, dims, preferred_element_type=F32)


def _dot1(a, b, dims=NN):
    return _mm(_bf(a), _bf(b), dims)


def _split3(x):
    x0 = _bf(x)
    r1 = x - x0.astype(F32)
    x1 = _bf(r1)
    x2 = _bf(r1 - x1.astype(F32))
    return x0, x1, x2


def _dot_exact_rhs(a, b_bf16):
    a0, a1, a2 = _split3(a)
    return _mm(a0, b_bf16) + (_mm(a1, b_bf16) + _mm(a2, b_bf16))


def _dot_exact_lhs(a_bf16, b):
    b0, b1, b2 = _split3(b)
    return _mm(a_bf16, b0) + (_mm(a_bf16, b1) + _mm(a_bf16, b2))


def _rms(x, g):
    return x * lax.rsqrt(jnp.mean(x * x, axis=-1, keepdims=True) + EPS) * g


def _sigmoid(x):
    return 0.5 + 0.5 * jnp.tanh(0.5 * x)


def _silu(x):
    return x * _sigmoid(x)


def _softplus(x):
    return jnp.maximum(x, 0.0) + jnp.log(1.0 + jnp.exp(-jnp.abs(x)))


def _iota2(shape, dim):
    return lax.broadcasted_iota(jnp.int32, shape, dim)


def _group_ones(width, group):
    r = _iota2((width, width), 0) // group
    c = _iota2((width, width), 1) // group
    return jnp.where(r == c, 1.0, 0.0).astype(BF16)


def _head_expand(width):
    r = _iota2((LANES, width), 0)
    c = _iota2((LANES, width), 1) // HEAD
    return jnp.where(r == c, 1.0, 0.0).astype(BF16)


def _chunk_tril(tb, chunk):
    r = _iota2((tb, tb), 0)
    c = _iota2((tb, tb), 1)
    return jnp.where(((r // chunk) == (c // chunk)) & (c <= r), 1.0, 0.0).astype(BF16)


def _pad_rows(x, rows):
    if x.shape[0] == rows:
        return x
    return jnp.concatenate([x, jnp.zeros((rows - x.shape[0], x.shape[1]), x.dtype)], axis=0)


def _shift_rows(x, prev, i):
    rolled = pltpu.roll(x, i, 0)
    first = jnp.where(_iota2((SUBLANES, x.shape[1]), 0) < i, pltpu.roll(prev, i, 0),
                      rolled[:SUBLANES])
    return jnp.concatenate([first, rolled[SUBLANES:]], axis=0)


def _bd(y, left):
    z = jnp.zeros_like(y)
    return jnp.concatenate([jnp.where(left, y, z), jnp.where(left, z, y)], axis=0)


def _diag(r, left):
    return jnp.where(left, r[:HEAD], r[HEAD:])


BASE = 16


def _bd8(y):
    g = _iota2((BASE, LANES), 1) // BASE
    z = jnp.zeros_like(y)
    return jnp.concatenate([jnp.where(g == i, y, z) for i in range(LANES // BASE)], axis=0)


def _tri_inv_sbs(a_list, left, tick=lambda: None):
    r = _iota2((CORE, LANES), 0)
    c = _iota2((CORE, LANES), 1) % HEAD
    nb = CORE // BASE
    lane16 = _iota2((BASE, LANES), 1)
    blk_of_lane = (lane16 % HEAD) // BASE
    eye16 = jnp.where(_iota2((BASE, LANES), 0) == lane16 % BASE, 1.0, 0.0).astype(F32)

    def diag_blocks(a):
        d = a[(nb - 1) * BASE:]
        for b in range(nb - 2, -1, -1):
            d = jnp.where(blk_of_lane == b, a[b * BASE:(b + 1) * BASE], d)
        return d

    d0f = [diag_blocks(a) for a in a_list]
    d0 = [_bf(x) for x in d0f]
    pw = [_bf(_mm(x, _bd8(x))) for x in d0]
    tick()
    m = [eye16 - x for x in d0f]
    for _ in range(2):
        bdp = [_bd8(x) for x in pw]
        both = [_mm(jnp.concatenate([x, _bf(mi)], axis=0), y) for x, mi, y in zip(pw, m, bdp)]
        tick()
        pw = [_bf(t[:BASE]) for t in both]
        m = [mi + t[BASE:] for mi, t in zip(m, both)]
    m = [mi + _mm(_bf(mi), _bd8(x)) for mi, x in zip(m, pw)]
    tick()
    m = [jnp.concatenate([jnp.where(blk_of_lane == b, mi, 0.0) for b in range(nb)], axis=0) for mi in m]
    size = 2 * BASE
    while size <= CORE:
        half = size // 2
        nh = CORE // half
        sel = ((r // size) == (c // size)) & ((r // half) != (c // half))
        bdb = [_bd(_bf(jnp.where(sel, a, 0.0)), left) for a in a_list]
        mb = [_bf(mi) for mi in m]
        low = [jnp.concatenate([x[b * half:(b + 1) * half] for b in range(1, nh, 2)], axis=0) for x in mb]
        t = [_bf(_mm(x, y)) for x, y in zip(low, bdb)]
        tick()
        upd = [_mm(x, _bd(y, left)) for x, y in zip(t, mb)]
        tick()
        m = [jnp.concatenate([mi[b * half:(b + 1) * half] - u[(b // 2) * half:(b // 2 + 1) * half]
                              if b % 2 else mi[b * half:(b + 1) * half] for b in range(nh)], axis=0)
             for mi, u in zip(m, upd)]
        size *= 2
    return m


def _const_spec(shape):
    nd = len(shape)
    return pl.BlockSpec(shape, lambda *_: (0,) * nd, pipeline_mode=pl.Buffered(1))


def _ffn_body(x_ref, g_ref, w13_ref, w2_ref, gn_ref, *o_refs, tail):
    x = x_ref[...]
    h = _rms(x, g_ref[...]).astype(BF16)
    acc = None
    for c in range(D_FF // FF_CHUNK):
        lo = c * FF_CHUNK
        a = jnp.dot(h, w13_ref[:, lo:lo + FF_CHUNK], preferred_element_type=F32)
        b = jnp.dot(h, w13_ref[:, D_FF + lo:D_FF + lo + FF_CHUNK], preferred_element_type=F32)
        part = jnp.dot((_silu(a) * b).astype(BF16), w2_ref[lo:lo + FF_CHUNK, :],
                       preferred_element_type=F32)
        acc = part if acc is None else acc + part
    y = x + 0.5 * acc
    if tail == "normed":
        o_refs[0][...] = _rms(y, gn_ref[...])
    else:
        o_refs[0][...] = y
    if tail == "with_h":
        o_refs[1][...] = _rms(y, gn_ref[...]).astype(BF16)


def _ffn(x2d, g, w13, w2, gn, tail):
    m = x2d.shape[0]
    tm = min(FFN_TILE, m)
    tok = pl.BlockSpec((tm, D_MODEL), lambda i: (i, 0))
    out_specs = [tok]
    out_shape = [jax.ShapeDtypeStruct((m, D_MODEL), F32)]
    if tail == "with_h":
        out_specs.append(tok)
        out_shape.append(jax.ShapeDtypeStruct((m, D_MODEL), BF16))
    return pl.pallas_call(
        functools.partial(_ffn_body, tail=tail),
        grid=(m // tm,),
        in_specs=[
            tok,
            _const_spec((1, D_MODEL)),
            _const_spec((D_MODEL, 2 * D_FF)),
            _const_spec((D_FF, D_MODEL)),
            _const_spec((1, D_MODEL)),
        ],
        out_specs=out_specs,
        out_shape=out_shape,
        compiler_params=pltpu.CompilerParams(
            dimension_semantics=("arbitrary",), vmem_limit_bytes=VMEM_LIMIT),
        name="ffn",
    )(x2d, g, w13, w2, gn)


RET_PAIRS = RET_HEADS // 2
RET_LOG_G = [math.log1p(-(2.0 ** (-5.0 - hh))) for hh in range(RET_HEADS)]


def _ret_body(h_ref, w_ref, cos_ref, sin_ref, gn_ref, s0_ref,
              o_ref, sn_ref, s_scr, q_scr, k_scr, v_scr, o_scr, *, tb):
    j = pl.program_id(1)
    tbp = max(tb, CORE)
    nc = tbp // CORE
    grp = _chunk_groups(nc)
    valid = min(tb, CORE)
    zero_blk = jnp.zeros((RET_DK, RET_DV), F32)

    @pl.when(j == 0)
    def _():
        for p in range(RET_PAIRS):
            s_scr[p] = jnp.concatenate(
                [jnp.concatenate([s0_ref[2 * p], zero_blk], axis=1),
                 jnp.concatenate([zero_blk, s0_ref[2 * p + 1]], axis=1)], axis=0)

    z = jnp.dot(h_ref[...], w_ref[...], preferred_element_type=F32)
    q = z[:, :RET_QK]
    k = z[:, RET_QK:2 * RET_QK]
    cos = cos_ref[...]
    sin = sin_ref[...]
    first = (_iota2((tb, RET_QK), 1) % RET_DK) < (RET_DK // 2)

    def rot(t):
        swapped = jnp.where(first, pltpu.roll(t, RET_QK - RET_DK // 2, 1),
                            pltpu.roll(t, RET_DK // 2, 1))
        return t * cos + swapped * sin

    q_scr[...] = _pad_rows(rot(q), tbp)
    k_scr[...] = _pad_rows(rot(k) * (RET_DK ** -0.5), tbp)
    v_scr[...] = _pad_rows(z[:, 2 * RET_QK:2 * RET_QK + RET_W], tbp)

    left, incl, _ = _core_masks()
    ri = _iota2((CORE, LANES), 0)
    ci = _iota2((CORE, LANES), 1) % HEAD
    rf = ri.astype(F32)
    dfi = jnp.where(incl, (ri - ci).astype(F32), 0.0)
    row2 = _iota2((2 * RET_DK, 2 * RET_DV), 0) < RET_DK
    col2 = _iota2((2 * RET_DK, 2 * RET_DV), 1) < RET_DV
    blockmask = row2 == col2
    vtop = _iota2((CORE, 2 * RET_DV), 1) < RET_DV
    dmask, qdec, kdec, cdec = [], [], [], []
    for p in range(RET_PAIRS):
        lg = jnp.where(left, RET_LOG_G[2 * p], RET_LOG_G[2 * p + 1])
        dmask.append(jnp.where(incl, jnp.exp(dfi * lg), 0.0))
        qdec.append(jnp.exp((rf + 1.0) * lg))
        kdec.append(jnp.exp((valid - 1.0 - rf) * lg))
        cdec.append(jnp.where(row2, math.exp(valid * RET_LOG_G[2 * p]),
                              math.exp(valid * RET_LOG_G[2 * p + 1])))
    psl = [slice(p * LANES, (p + 1) * LANES) for p in range(RET_PAIRS)]
    vsl = [slice(p * 2 * RET_DV, (p + 1) * 2 * RET_DV) for p in range(RET_PAIRS)]

    def body(i, carry):
        chunks = []
        for gi in range(grp):
            rows = pl.ds(pl.multiple_of((i * grp + gi) * CORE, CORE), CORE)
            chunks.append((rows, q_scr[rows, :], k_scr[rows, :], v_scr[rows, :]))
        items = [(ch, p) for ch in chunks for p in range(RET_PAIRS)]
        scores = [_mm(_bf(qc[:, psl[p]]), _bd(_bf(kc[:, psl[p]]), left), NT) * dmask[p]
                  for (_, qc, kc, _), p in items]
        kv = [_mm(_bf(kc[:, psl[p]] * kdec[p]), _bf(vc[:, vsl[p]]), TN)
              for (_, _, kc, vc), p in items]
        state = [s_scr[p] for p in range(RET_PAIRS)]
        for n, ((rows, qc, _, vc), p) in enumerate(items):
            vp = _bf(vc[:, vsl[p]])
            zv = jnp.zeros_like(vp)
            rhs = jnp.concatenate([jnp.where(vtop, vp, zv), jnp.where(vtop, zv, vp), _bf(state[p])],
                                  axis=0)
            lhs = _bf(jnp.concatenate([scores[n], qc[:, psl[p]] * qdec[p]], axis=1))
            o_scr[rows, vsl[p]] = _mm(lhs, rhs)
            state[p] = cdec[p] * state[p] + jnp.where(blockmask, kv[n], 0.0)
        for p in range(RET_PAIRS):
            s_scr[p] = state[p]
        return carry

    lax.fori_loop(0, nc // grp, body, 0)

    gate = z[:, 2 * RET_QK + RET_W:]
    gn = gn_ref[...]
    for hh in range(RET_HEADS):
        sl = slice(hh * RET_DV, (hh + 1) * RET_DV)
        oh = o_scr[pl.ds(0, tb), sl]
        mu = jnp.mean(oh, axis=-1, keepdims=True)
        dlt = oh - mu
        var = jnp.mean(dlt * dlt, axis=-1, keepdims=True)
        y = dlt * lax.rsqrt(var + 1e-5) * gn[:, sl]
        o_ref[:, sl] = (y * _silu(gate[:, sl])).astype(BF16)

    @pl.when(j == pl.num_programs(1) - 1)
    def _():
        for p in range(RET_PAIRS):
            s = s_scr[p]
            sn_ref[2 * p] = s[:RET_DK, :RET_DV]
            sn_ref[2 * p + 1] = s[RET_DK:, RET_DV:]


def _retention(h, w, cos, sin, gn, s0, tb):
    b, l, _ = h.shape
    st = (RET_HEADS, RET_DK, RET_DV)
    tbp = max(tb, CORE)
    return pl.pallas_call(
        functools.partial(_ret_body, tb=tb),
        grid=(b, l // tb),
        in_specs=[
            pl.BlockSpec((None, tb, D_MODEL), lambda i, j: (i, j, 0)),
            _const_spec(w.shape),
            pl.BlockSpec((tb, RET_QK), lambda i, j: (j, 0)),
            pl.BlockSpec((tb, RET_QK), lambda i, j: (j, 0)),
            _const_spec((1, RET_W)),
            pl.BlockSpec((None,) + st, lambda i, j: (i, 0, 0, 0)),
        ],
        out_specs=[
            pl.BlockSpec((None, tb, RET_W), lambda i, j: (i, j, 0)),
            pl.BlockSpec((None,) + st, lambda i, j: (i, 0, 0, 0)),
        ],
        out_shape=[jax.ShapeDtypeStruct((b, l, RET_W), BF16),
                   jax.ShapeDtypeStruct((b,) + st, F32)],
        scratch_shapes=[
            pltpu.VMEM((RET_PAIRS, 2 * RET_DK, 2 * RET_DV), F32),
            pltpu.VMEM((tbp, RET_QK), F32),
            pltpu.VMEM((tbp, RET_QK), F32),
            pltpu.VMEM((tbp, RET_W), F32),
            pltpu.VMEM((tbp, RET_W), F32),
        ],
        compiler_params=pltpu.CompilerParams(
            dimension_semantics=("arbitrary", "arbitrary"), vmem_limit_bytes=VMEM_LIMIT),
        name="retention",
    )(h, w, cos, sin, gn, s0)


HALO = SUBLANES


def _core_masks():
    r = _iota2((CORE, LANES), 0)
    c = _iota2((CORE, LANES), 1) % HEAD
    left = _iota2((CORE, LANES), 1) < HEAD
    return left, r >= c, r > c


def _load_state_sbs(s0_ref, s_scr):
    for p in range(PAIRS):
        s_scr[p] = jnp.concatenate([s0_ref[2 * p], s0_ref[2 * p + 1]], axis=1)


def _store_state_sbs(s_scr, sn_ref):
    for p in range(PAIRS):
        s = s_scr[p]
        sn_ref[2 * p] = s[:, :HEAD]
        sn_ref[2 * p + 1] = s[:, HEAD:]


def _chunk_groups(nc):
    return 8 if nc % 8 == 0 else (4 if nc % 4 == 0 else (2 if nc % 2 == 0 else 1))


SLAB = 128


def _pipe_maps(nblk, total):
    def in_blk(t):
        ta = jnp.minimum(t, total - 1)
        return ta // nblk, ta % nblk

    def out_blk(t):
        tp = jnp.maximum(t - 1, 0)
        return tp // nblk, tp % nblk

    return in_blk, out_blk


def _gdn_pipe_body(h_ref, w_ref, cw_ref, alog_ref, dtb_ref, nrm_ref, s0_ref, c0_ref,
                   o_ref, sn_ref, cn_ref,
                   s_scr, xp_scr, q_scr, k_scr, v_scr, be_scr, ga_scr, gate_scr, gar_scr,
                   o_scr, m_scr, n_scr, qp_scr, egl_scr, *, tb, nblk):
    t = pl.program_id(0)
    total = pl.num_programs(0) - 1
    ja = jnp.minimum(t, total - 1) % nblk
    jp = jnp.maximum(t - 1, 0) % nblk
    wslot = 0
    rslot = 0
    tbp = max(tb, CORE)
    nc = tbp // CORE
    slab = min(SLAB, tb)
    slabp = max(slab, CORE)
    nslab = tb // slab

    @pl.when(t == 0)
    def _():
        for ref in (q_scr, k_scr, v_scr, be_scr, ga_scr, gate_scr, gar_scr):
            ref[...] = jnp.zeros_like(ref)

    @pl.when(ja == 0)
    def _():
        xp_scr[...] = jnp.zeros_like(xp_scr)
        xp_scr[pl.ds(HALO - (CONV_W - 1), CONV_W - 1), :] = c0_ref[...]

    @pl.when(jp == 0)
    def _():
        _load_state_sbs(s0_ref, s_scr)

    gones = _group_ones(GDN_QK, GDN_DK)
    expand = _head_expand(GDN_V)
    tril = _chunk_tril(slabp, CORE)
    ncp = gar_scr.shape[1]
    rr = _iota2((slabp, GDN_V), 0)
    cc = _iota2((slabp, GDN_V), 1)
    on_diag = (rr % CORE) == (cc % HEAD)

    def input_stage():
        prev = xp_scr[...]
        garv = jnp.zeros((ncp, GDN_V), F32)
        for s in range(nslab):
            z = jnp.dot(h_ref[pl.ds(s * slab, slab), :], w_ref[...], preferred_element_type=F32)
            yield
            zq = z[:, :GDN_CONV_CH]
            cw = cw_ref[...]
            y = zq * cw[CONV_W - 1:CONV_W, :]
            for i in range(1, CONV_W):
                y = y + _shift_rows(zq, prev, i) * cw[CONV_W - 1 - i:CONV_W - i, :]
            prev = zq[slab - HALO:, :]
            yield
            qkv = _silu(y)
            q = qkv[:, :GDN_QK]
            k = qkv[:, GDN_QK:2 * GDN_QK]
            q = _pad_rows(q * lax.rsqrt(_dot1(q * q, gones) + 1e-6) * (GDN_DK ** -0.5), slabp)
            k = _pad_rows(k * lax.rsqrt(_dot1(k * k, gones) + 1e-6), slabp)
            v = _pad_rows(qkv[:, 2 * GDN_QK:], slabp)
            yield
            zb = z[:, GDN_CONV_CH + GDN_V:GDN_CONV_CH + GDN_V + LANES]
            za = z[:, GDN_CONV_CH + GDN_V + LANES:]
            beta = _pad_rows(_sigmoid(zb), slabp)
            log_a = _pad_rows(-jnp.exp(alog_ref[...]) * _softplus(za + dtb_ref[...]), slabp)
            ga = _dot_exact_lhs(tril, log_a)
            gae = _dot_exact_rhs(ga, expand)
            bee = _dot_exact_rhs(beta, expand)
            pick = jnp.where(on_diag, gae, 0.0)
            chunk_of = jnp.where(_iota2((ncp, slabp), 0)
                                 == _iota2((ncp, slabp), 1) // CORE + s * (slabp // CORE), 1.0, 0.0)
            garv = garv + _dot_exact_lhs(_bf(chunk_of), pick)
            yield
            rows = pl.ds(s * slabp, slabp)
            q_scr[wslot, rows, :] = q
            k_scr[wslot, rows, :] = k
            v_scr[wslot, rows, :] = v
            be_scr[wslot, rows, :] = bee
            ga_scr[wslot, rows, :] = gae
            gate_scr[wslot, pl.ds(s * slab, slab), :] = z[:, GDN_CONV_CH:GDN_CONV_CH + GDN_V]
            yield
        xp_scr[...] = prev
        gar_scr[wslot] = garv
        yield

    gate_prev = gate_scr[rslot]
    gen = input_stage()

    def tick():
        next(gen, None)

    left, incl, strict = _core_masks()
    incl4 = jnp.concatenate([incl] * PAIRS, axis=1)
    psl = [slice(p * LANES, (p + 1) * LANES) for p in range(PAIRS)]

    def phase1(chunks, tick):
        probs = []
        for c in chunks:
            rows = pl.ds(c * CORE, CORE)
            gac = ga_scr[rslot, rows, :]
            glast = gac[CORE - 1:CORE, :]
            egl_scr[pl.ds(c, 1), :] = jnp.exp(glast)
            eg = jnp.exp(gac)
            gam = jnp.where(incl4, jnp.exp(jnp.where(incl4, gac - gar_scr[rslot, pl.ds(c, 1), :], 0.0)),
                            0.0)
            qc = q_scr[rslot, rows, :]
            kc = k_scr[rslot, rows, :]
            bec = be_scr[rslot, rows, :]
            probs.append(dict(rows=rows, gam=gam, be=bec, q=_bf(qc), k=_bf(kc), qe=qc * eg,
                              kd=_bf(kc * jnp.exp(glast - gac)), bv=_bf(bec * v_scr[rslot, rows, :]),
                              bek=_bf(bec * eg * kc)))
        tick()
        items = [(pr, s) for pr in probs for s in psl]
        qk = [_mm(jnp.concatenate([pr["q"][:, s], pr["k"][:, s]], axis=0), _bd(pr["k"][:, s], left), NT)
              for pr, s in items]
        tick()
        pm = [_bf(x[:CORE] * pr["gam"][:, s]) for x, (pr, s) in zip(qk, items)]
        amat = [jnp.where(strict, pr["be"][:, s] * x[CORE:] * pr["gam"][:, s], 0.0)
                for x, (pr, s) in zip(qk, items)]
        tinv = _tri_inv_sbs(amat, left, tick)
        sol = [_bf(_mm(_bf(tm), jnp.concatenate([_bd(pr["bv"][:, s], left), _bd(pr["bek"][:, s], left)],
                                                axis=1)))
               for tm, (pr, s) in zip(tinv, items)]
        tick()
        kts = [_mm(pr["kd"][:, s], x, TN) for x, (pr, s) in zip(sol, items)]
        tick()
        ps = [_mm(p_, jnp.concatenate([_bd(x[:, :LANES], left), _bd(x[:, LANES:], left)], axis=1))
              for p_, x in zip(pm, sol)]
        tick()
        for n, (pr, s) in enumerate(items):
            n_scr[pr["rows"], s] = _diag(kts[n][:, :LANES], left)
            m_scr[pr["rows"], s] = _diag(kts[n][:, LANES:], left)
            o_scr[pr["rows"], s] = ps[n][:, :LANES]
            qp_scr[pr["rows"], s] = pr["qe"][:, s] - ps[n][:, LANES:]

    phase1(range(nc), tick)

    for c in range(nc):
        rows = pl.ds(c * CORE, CORE)
        mc = _bf(m_scr[rows, :])
        nn = n_scr[rows, :]
        qpc = _bf(qp_scr[rows, :])
        eglc = egl_scr[pl.ds(c, 1), :]
        s_old = [s_scr[p] for p in range(PAIRS)]
        res = [_mm(jnp.concatenate([mc[:, psl[p]], qpc[:, psl[p]]], axis=0), _bd(_bf(s_old[p]), left))
               for p in range(PAIRS)]
        for p in range(PAIRS):
            s_scr[p] = eglc[:, psl[p]] * s_old[p] - res[p][:CORE] + nn[:, psl[p]]
        o_scr[rows, :] = o_scr[rows, :] + jnp.concatenate([x[CORE:] for x in res], axis=1)
        tick()
    for _ in gen:
        pass

    o = o_scr[pl.ds(0, tb), :]
    ms = _dot1(o * o, gones) * (1.0 / GDN_DV)
    o_ref[...] = (o * lax.rsqrt(ms + EPS) * nrm_ref[...] * _silu(gate_prev)).astype(BF16)

    @pl.when((jp == nblk - 1) & (t > 0))
    def _():
        _store_state_sbs(s_scr, sn_ref)

    @pl.when(ja == nblk - 1)
    def _():
        cn_ref[...] = xp_scr[pl.ds(HALO - (CONV_W - 1), CONV_W - 1), :]


def _gdn_pipe(h, w, cw, alog, dtb, nrm, s0, c0, tb):
    b, l, _ = h.shape
    nblk = l // tb
    total = b * nblk
    in_blk, out_blk = _pipe_maps(nblk, total)
    st = (GDN_HEADS, GDN_DK, GDN_DV)
    cst = (CONV_W - 1, GDN_CONV_CH)
    tbp = max(tb, CORE)
    dbl = pltpu.VMEM((1, tbp, GDN_V), F32)
    big = pltpu.VMEM((tbp, GDN_V), F32)
    ncp = max(tbp // CORE, SUBLANES)
    return pl.pallas_call(
        functools.partial(_gdn_pipe_body, tb=tb, nblk=nblk),
        grid=(total + 1,),
        in_specs=[
            pl.BlockSpec((None, tb, D_MODEL), lambda t: in_blk(t) + (0,)),
            _const_spec(w.shape),
            _const_spec(cw.shape),
            _const_spec((1, LANES)),
            _const_spec((1, LANES)),
            _const_spec((1, GDN_V)),
            pl.BlockSpec((None,) + st, lambda t: (out_blk(t)[0], 0, 0, 0)),
            pl.BlockSpec((None,) + cst, lambda t: (in_blk(t)[0], 0, 0)),
        ],
        out_specs=[
            pl.BlockSpec((None, tb, GDN_V), lambda t: out_blk(t) + (0,)),
            pl.BlockSpec((None,) + st, lambda t: (out_blk(t)[0], 0, 0, 0)),
            pl.BlockSpec((None,) + cst, lambda t: (in_blk(t)[0], 0, 0)),
        ],
        out_shape=[jax.ShapeDtypeStruct((b, l, GDN_V), BF16),
                   jax.ShapeDtypeStruct((b,) + st, F32),
                   jax.ShapeDtypeStruct((b,) + cst, F32)],
        scratch_shapes=[
            pltpu.VMEM((PAIRS, HEAD, LANES), F32),
            pltpu.VMEM((HALO, GDN_CONV_CH), F32),
            dbl, dbl, dbl, dbl, dbl,
            pltpu.VMEM((1, tb, GDN_V), F32),
            pltpu.VMEM((1, ncp, GDN_V), F32),
            big, big, big, big,
            pltpu.VMEM((ncp, GDN_V), F32),
        ],
        compiler_params=pltpu.CompilerParams(
            dimension_semantics=("arbitrary",), vmem_limit_bytes=VMEM_LIMIT),
        name="gdn",
    )(h, w, cw, alog, dtb, nrm, s0, c0)


def _rwkv_pipe_body(h_ref, w_ref, mu_ref, w0_ref, w2_ref, a0_ref, a2_ref, g2_ref,
                    kkw_ref, ka_ref, rk_ref, gn_ref, s0_ref, sh0_ref,
                    o_ref, sn_ref, shn_ref,
                    s_scr, zc_scr, r_scr, k2_scr, v_scr, kk_scr, kka_scr, lw_scr, cum_scr,
                    bonus_scr, g_scr, o_scr, qp_scr, m_scr, n_scr, wt_scr, *, tb, nblk):
    t = pl.program_id(0)
    total = pl.num_programs(0) - 1
    ja = jnp.minimum(t, total - 1) % nblk
    jp = jnp.maximum(t - 1, 0) % nblk
    wslot = 0
    rslot = 0
    tbp = max(tb, CORE)
    nc = tbp // CORE
    slab = min(SLAB, tb)
    slabp = max(slab, CORE)
    nslab = tb // slab

    @pl.when(t == 0)
    def _():
        for ref in (r_scr, k2_scr, v_scr, kk_scr, kka_scr, lw_scr, cum_scr, bonus_scr, g_scr):
            ref[...] = jnp.zeros_like(ref)

    @pl.when(ja == 0)
    def _():
        zc_scr[...] = jnp.zeros_like(zc_scr)
        zc_scr[pl.ds(HALO - 1, 1), :] = sh0_ref[...]

    @pl.when(jp == 0)
    def _():
        _load_state_sbs(s0_ref, s_scr)

    gones = _group_ones(RWKV_W, RWKV_N)
    tril = _chunk_tril(slabp, CORE)

    def input_stage():
        prev = zc_scr[...]
        for s in range(nslab):
            z = jnp.dot(h_ref[pl.ds(s * slab, slab), :], w_ref[...], preferred_element_type=F32)
            yield
            cs = z + (_shift_rows(z, prev, 1) - z) * mu_ref[...]
            prev = z[slab - HALO:, :]
            yield
            r = cs[:, :RWKV_W]
            k = cs[:, RWKV_W:2 * RWKV_W]
            v = cs[:, 2 * RWKV_W:3 * RWKV_W]
            zwa = cs[:, 3 * RWKV_W:3 * RWKV_W + LANES]
            zg = cs[:, 3 * RWKV_W + LANES:]
            w_raw = -_softplus(-(w0_ref[...] + _dot1(jnp.tanh(zwa), w2_ref[...]))) - 0.5
            lw = _pad_rows(-jnp.exp(w_raw), slabp)
            a = _sigmoid(a0_ref[...] + _dot1(zwa, a2_ref[...]))
            g = _dot1(_sigmoid(zg), g2_ref[...])
            yield
            kkn = k * kkw_ref[...]
            kkn = kkn * lax.rsqrt(_dot1(kkn * kkn, gones) + 1e-6)
            k2 = k * (1.0 + (a - 1.0) * ka_ref[...])
            bonus = _dot1(r * k2 * rk_ref[...], gones) * v
            yield
            rows = pl.ds(s * slabp, slabp)
            r_scr[wslot, rows, :] = _pad_rows(r, slabp)
            k2_scr[wslot, rows, :] = _pad_rows(k2, slabp)
            v_scr[wslot, rows, :] = _pad_rows(v, slabp)
            kk_scr[wslot, rows, :] = _pad_rows(kkn, slabp)
            kka_scr[wslot, rows, :] = _pad_rows(kkn * a, slabp)
            lw_scr[wslot, rows, :] = lw
            cum_scr[wslot, rows, :] = _dot_exact_lhs(tril, lw)
            bonus_scr[wslot, pl.ds(s * slab, slab), :] = bonus
            g_scr[wslot, pl.ds(s * slab, slab), :] = g
            yield
        zc_scr[...] = prev
        yield

    bonus_prev = bonus_scr[rslot]
    g_prev = g_scr[rslot]
    gen = input_stage()

    def tick():
        next(gen, None)

    left, incl, strict = _core_masks()
    psl = [slice(p * LANES, (p + 1) * LANES) for p in range(PAIRS)]

    probs = []
    for c in range(nc):
        rows = pl.ds(c * CORE, CORE)
        cum = cum_scr[rslot, rows, :]
        tot = cum[CORE - 1:CORE, :]
        wt_scr[pl.ds(c, 1), :] = jnp.exp(tot)
        w_inv = jnp.exp(-cum)
        w_rem = jnp.exp(tot - cum)
        kkac = kka_scr[rslot, rows, :]
        k2c = k2_scr[rslot, rows, :]
        rt = r_scr[rslot, rows, :] * jnp.exp(cum)
        probs.append(dict(rows=rows, rt=rt, rtb=_bf(rt),
                          at=_bf(-kk_scr[rslot, rows, :] * jnp.exp(cum - lw_scr[rslot, rows, :])),
                          bt=_bf(kkac * w_inv), kt=_bf(k2c * w_inv), bh=_bf(kkac * w_rem),
                          kh=_bf(k2c * w_rem), v=_bf(v_scr[rslot, rows, :])))
    tick()
    items = [(pr, s) for pr in probs for s in psl]
    x = [_mm(jnp.concatenate([pr["at"][:, s], pr["rtb"][:, s]], axis=0),
             jnp.concatenate([_bd(pr["bt"][:, s], left), _bd(pr["kt"][:, s], left)], axis=0), NT)
         for pr, s in items]
    tick()
    neg_ab = [jnp.where(strict, -tt[:CORE, :LANES], 0.0) for tt in x]
    a_rb = [_bf(jnp.where(incl, tt[CORE:, :LANES], 0.0)) for tt in x]
    a_ak = [_bf(jnp.where(strict, tt[:CORE, LANES:], 0.0)) for tt in x]
    a_rk = [_bf(jnp.where(incl, tt[CORE:, LANES:], 0.0)) for tt in x]
    tinv = _tri_inv_sbs(neg_ab, left, tick)
    both_v = [_mm(jnp.concatenate([p_, q_], axis=0), _bd(pr["v"][:, s], left))
              for p_, q_, (pr, s) in zip(a_ak, a_rk, items)]
    akv = [_bf(t_[:CORE]) for t_ in both_v]
    rkv = [t_[CORE:] for t_ in both_v]
    tick()
    pu = [_bf(_mm(_bf(tm), jnp.concatenate([_bd(pr["at"][:, s], left), _bd(u, left)], axis=1)))
          for tm, u, (pr, s) in zip(tinv, akv, items)]
    tick()
    rbp = [_mm(p_, jnp.concatenate([_bd(u[:, :LANES], left), _bd(u[:, LANES:], left)], axis=1))
           for p_, u in zip(a_rb, pu)]
    tick()
    tick()
    bp =[_mm(u, pr["bh"][:, s], TN) for u, (pr, s) in zip(pu, items)]
    tick()
    vk = [_mm(pr["v"][:, s], pr["kh"][:, s], TN) for pr, s in items]
    tick()
    for n, (pr, s) in enumerate(items):
        qp_scr[pr["rows"], s] = pr["rt"][:, s] + rbp[n][:, :LANES]
        o_scr[pr["rows"], s] = rbp[n][:, LANES:] + rkv[n]
        m_scr[pr["rows"], s] = _diag(bp[n][:LANES], left)
        n_scr[pr["rows"], s] = _diag(bp[n][LANES:], left) + _diag(vk[n], left)

    for c in range(nc):
        rows = pl.ds(c * CORE, CORE)
        mc = _bf(m_scr[rows, :])
        nn = n_scr[rows, :]
        qpc = _bf(qp_scr[rows, :])
        wtc = wt_scr[pl.ds(c, 1), :]
        s_old = [s_scr[p] for p in range(PAIRS)]
        sb = [_bf(s) for s in s_old]
        od = [_mm(qpc[:, psl[p]], _bd(sb[p], left), NT) for p in range(PAIRS)]
        sm = [_mm(sb[p], _bd(mc[:, psl[p]], left)) for p in range(PAIRS)]
        for p in range(PAIRS):
            s_scr[p] = s_old[p] * wtc[:, psl[p]] + sm[p] + nn[:, psl[p]]
        o_scr[rows, :] = o_scr[rows, :] + jnp.concatenate(od, axis=1)
        tick()
    for _ in gen:
        pass

    o = o_scr[pl.ds(0, tb), :]
    mean = _dot1(o, gones) * (1.0 / RWKV_N)
    dlt = o - mean
    var = _dot1(dlt * dlt, gones) * (1.0 / RWKV_N)
    y = dlt * lax.rsqrt(var + RWKV_GN_EPS) * gn_ref[...]
    o_ref[...] = ((y + bonus_prev) * g_prev).astype(BF16)

    @pl.when((jp == nblk - 1) & (t > 0))
    def _():
        _store_state_sbs(s_scr, sn_ref)

    @pl.when(ja == nblk - 1)
    def _():
        shn_ref[...] = zc_scr[pl.ds(HALO - 1, 1), :]


def _rwkv_pipe(h, w, mu, w0, w2, a0, a2, g2, kkw, ka, rk, gn, s0, sh0, tb):
    b, l, _ = h.shape
    nblk = l // tb
    total = b * nblk
    in_blk, out_blk = _pipe_maps(nblk, total)
    st = (RWKV_HEADS, RWKV_N, RWKV_N)
    sst = (1, RWKV_SHIFT_CH)
    tbp = max(tb, CORE)
    row = _const_spec((1, RWKV_W))
    dbl = pltpu.VMEM((1, tbp, RWKV_W), F32)
    dbl_tb = pltpu.VMEM((1, tb, RWKV_W), F32)
    big = pltpu.VMEM((tbp, RWKV_W), F32)
    return pl.pallas_call(
        functools.partial(_rwkv_pipe_body, tb=tb, nblk=nblk),
        grid=(total + 1,),
        in_specs=[
            pl.BlockSpec((None, tb, D_MODEL), lambda t: in_blk(t) + (0,)),
            _const_spec(w.shape),
            _const_spec((1, RWKV_SHIFT_CH)),
            row, _const_spec((LANES, RWKV_W)), row, _const_spec((LANES, RWKV_W)),
            _const_spec((LANES, RWKV_W)),
            row, row, row, row,
            pl.BlockSpec((None,) + st, lambda t: (out_blk(t)[0], 0, 0, 0)),
            pl.BlockSpec((None,) + sst, lambda t: (in_blk(t)[0], 0, 0)),
        ],
        out_specs=[
            pl.BlockSpec((None, tb, RWKV_W), lambda t: out_blk(t) + (0,)),
            pl.BlockSpec((None,) + st, lambda t: (out_blk(t)[0], 0, 0, 0)),
            pl.BlockSpec((None,) + sst, lambda t: (in_blk(t)[0], 0, 0)),
        ],
        out_shape=[jax.ShapeDtypeStruct((b, l, RWKV_W), BF16),
                   jax.ShapeDtypeStruct((b,) + st, F32),
                   jax.ShapeDtypeStruct((b,) + sst, F32)],
        scratch_shapes=[
            pltpu.VMEM((PAIRS, HEAD, LANES), F32),
            pltpu.VMEM((HALO, RWKV_SHIFT_CH), F32),
            dbl, dbl, dbl, dbl, dbl, dbl, dbl,
            dbl_tb, dbl_tb,
            big, big, big, big,
            pltpu.VMEM((max(tbp // CORE, SUBLANES), RWKV_W), F32),
        ],
        compiler_params=pltpu.CompilerParams(
            dimension_semantics=("arbitrary",), vmem_limit_bytes=VMEM_LIMIT),
        name="rwkv7",
    )(h, w, mu, w0, w2, a0, a2, g2, kkw, ka, rk, gn, s0, sh0)


def _merge_body(x_ref, h_ref, wg_ref, oa_ref, ob_ref, oc_ref, wb_ref, wo_ref, o_ref):
    x = x_ref[...]
    h = h_ref[...]
    m = None
    for i, br in enumerate((oa_ref, ob_ref, oc_ref)):
        gate = jnp.dot(h, wg_ref[:, i * D_MODEL:(i + 1) * D_MODEL], preferred_element_type=F32)
        t = _sigmoid(gate) * jnp.dot(br[...], wb_ref[i], preferred_element_type=F32)
        m = t if m is None else m + t
    o_ref[...] = x + jnp.dot(m.astype(BF16), wo_ref[...], preferred_element_type=F32)


def _merge(x2d, h2d, wg, oa, ob, oc, wb, wo):
    m = x2d.shape[0]
    tm = min(TOKEN_TILE, m)
    bw = oa.shape[1]
    tok = lambda w: pl.BlockSpec((tm, w), lambda i: (i, 0))
    return pl.pallas_call(
        _merge_body,
        grid=(m // tm,),
        in_specs=[tok(D_MODEL), tok(D_MODEL), _const_spec(wg.shape),
                  tok(bw), tok(bw), tok(bw), _const_spec(wb.shape), _const_spec(wo.shape)],
        out_specs=tok(D_MODEL),
        out_shape=jax.ShapeDtypeStruct((m, D_MODEL), F32),
        compiler_params=pltpu.CompilerParams(
            dimension_semantics=("arbitrary",), vmem_limit_bytes=VMEM_LIMIT),
        name="merge",
    )(x2d, h2d, wg, oa, ob, oc, wb, wo)


def _rope_tables(pos0, l):
    half = RET_DK // 2
    inv = ROPE_BASE ** (-jnp.arange(half, dtype=F32) / half)
    ang = (pos0 + jnp.arange(l)).astype(F32)[:, None] * inv[None, :]
    cos = jnp.cos(ang)
    sin = jnp.sin(ang)
    cos = jnp.tile(jnp.concatenate([cos, cos], axis=1), (1, RET_HEADS))
    sin = jnp.tile(jnp.concatenate([-sin, sin], axis=1), (1, RET_HEADS))
    return cos, sin


def _prep_weights(p):
    w_in = p["w_in"].astype(BF16)
    depth = w_in.shape[0]
    o = 0
    pieces = {}
    for name, size in (("a_qkvg", 2 * RET_QK + 2 * RET_W), ("b_qkv", GDN_CONV_CH), ("b_g", GDN_V),
                       ("b_beta", GDN_HEADS), ("b_alpha", GDN_HEADS), ("c_z", RWKV_SHIFT_CH),
                       ("gate", 3 * D_MODEL)):
        pieces[name] = w_in[:, :, o:o + size]
        o += size
    pad_l = lambda t, n: jnp.pad(t, ((0, 0), (0, 0), (0, n - t.shape[2])))
    row = lambda t: t[:, None, :]
    zeros64 = jnp.zeros((depth, 64, RWKV_W), F32)
    return dict(
        norm_ffn1=row(p["norm_ffn1"]), norm_mix=row(p["norm_mix"]), norm_ffn2=row(p["norm_ffn2"]),
        ffn1_w13=p["ffn1_w13"].astype(BF16), ffn1_w2=p["ffn1_w2"].astype(BF16),
        ffn2_w13=p["ffn2_w13"].astype(BF16), ffn2_w2=p["ffn2_w2"].astype(BF16),
        w_ret=pieces["a_qkvg"],
        w_gdn=jnp.concatenate([pieces["b_qkv"], pieces["b_g"], pad_l(pieces["b_beta"], LANES),
                               pad_l(pieces["b_alpha"], LANES)], axis=2),
        w_rwkv=pieces["c_z"],
        w_gate=pieces["gate"],
        ret_gn=row(p["ret_gn"]),
        gdn_conv=p["gdn_conv"],
        gdn_a_log=row(jnp.pad(p["gdn_a_log"], ((0, 0), (0, LANES - GDN_HEADS)))),
        gdn_dt_bias=row(jnp.pad(p["gdn_dt_bias"], ((0, 0), (0, LANES - GDN_HEADS)))),
        gdn_norm=row(jnp.tile(p["gdn_norm"], (1, GDN_HEADS))),
        rwkv_mu=row(p["rwkv_mu"]),
        rwkv_w0=row(p["rwkv_w0"]),
        rwkv_w2=jnp.concatenate([p["rwkv_w2"], zeros64], axis=1),
        rwkv_a0=row(p["rwkv_a0"]),
        rwkv_a2=jnp.concatenate([zeros64, p["rwkv_a2"]], axis=1),
        rwkv_g2=p["rwkv_g2"],
        rwkv_kk=row(p["rwkv_kk"]), rwkv_ka=row(p["rwkv_ka"]), rwkv_rk=row(p["rwkv_rk"]),
        rwkv_gn=row(p["rwkv_gn"]),
        w_branch=p["w_branch"].astype(BF16), w_out=p["w_out"].astype(BF16),
    )


def _layer(x, st, lp, tables, gf, final_norm, tb):
    b, l, d = x.shape
    s_ret, s_delta, s_conv, s_wkv, s_shift = st
    cos, sin = tables
    x2, h2 = _ffn(x.reshape(b * l, d), lp["norm_ffn1"], lp["ffn1_w13"], lp["ffn1_w2"],
                  lp["norm_mix"], "with_h")
    h3 = h2.reshape(b, l, d)
    oa, n_ret = _retention(h3, lp["w_ret"], cos, sin, lp["ret_gn"], s_ret, tb)
    ob, n_delta, n_conv = _gdn_pipe(h3, lp["w_gdn"], lp["gdn_conv"], lp["gdn_a_log"],
                               lp["gdn_dt_bias"], lp["gdn_norm"], s_delta, s_conv, tb)
    oc, n_wkv, n_shift = _rwkv_pipe(h3, lp["w_rwkv"], lp["rwkv_mu"], lp["rwkv_w0"],
                                    lp["rwkv_w2"], lp["rwkv_a0"], lp["rwkv_a2"], lp["rwkv_g2"],
                                    lp["rwkv_kk"], lp["rwkv_ka"], lp["rwkv_rk"], lp["rwkv_gn"],
                                    s_wkv, s_shift, tb)
    bw = oa.shape[-1]
    x4 = _merge(x2, h2, lp["w_gate"], oa.reshape(b * l, bw), ob.reshape(b * l, bw),
                oc.reshape(b * l, bw), lp["w_branch"], lp["w_out"])
    (x5,) = _ffn(x4, lp["norm_ffn2"], lp["ffn2_w13"], lp["ffn2_w2"], gf,
                 "normed" if final_norm else "plain")
    return x5.reshape(b, l, d), (n_ret, n_delta, n_conv, n_wkv, n_shift)


def _run_stream(x, pos0, states, wts, gf, tb):
    depth = wts["w_ret"].shape[0]
    l = x.shape[1]
    tb = min(tb, l)
    tables = _rope_tables(pos0, l)
    new = [[] for _ in range(5)]
    for li in range(depth):
        lp = {k: v[li] for k, v in wts.items()}
        st = tuple(s[li] for s in states)
        x, nst = _layer(x, st, lp, tables, gf, li == depth - 1, tb)
        for jdx in range(5):
            new[jdx].append(nst[jdx])
    return x, tuple(jnp.stack(t, axis=0) for t in new)


def kernel(x_prompt, x_sample, state_ret, state_delta, state_conv, state_wkv, state_shift,
           norm_ffn1, ffn1_w13, ffn1_w2, norm_mix, w_in, ret_gn, gdn_conv, gdn_a_log, gdn_dt_bias,
           gdn_norm, rwkv_mu, rwkv_w0, rwkv_w2, rwkv_a0, rwkv_a2, rwkv_g2, rwkv_kk, rwkv_ka, rwkv_rk,
           rwkv_gn, w_branch, w_out, norm_ffn2, ffn2_w13, ffn2_w2, norm_final):
    params = dict(
        norm_ffn1=norm_ffn1, ffn1_w13=ffn1_w13, ffn1_w2=ffn1_w2, norm_mix=norm_mix, w_in=w_in,
        ret_gn=ret_gn, gdn_conv=gdn_conv, gdn_a_log=gdn_a_log, gdn_dt_bias=gdn_dt_bias,
        gdn_norm=gdn_norm, rwkv_mu=rwkv_mu, rwkv_w0=rwkv_w0, rwkv_w2=rwkv_w2, rwkv_a0=rwkv_a0,
        rwkv_a2=rwkv_a2, rwkv_g2=rwkv_g2, rwkv_kk=rwkv_kk, rwkv_ka=rwkv_ka, rwkv_rk=rwkv_rk,
        rwkv_gn=rwkv_gn, w_branch=w_branch, w_out=w_out, norm_ffn2=norm_ffn2, ffn2_w13=ffn2_w13,
        ffn2_w2=ffn2_w2)
    wts = _prep_weights(params)
    gf = norm_final[None, :]
    depth = w_in.shape[0]
    bp = x_prompt.shape[0]
    zero_states = (
        jnp.zeros((depth, bp, RET_HEADS, RET_DK, RET_DV), F32),
        jnp.zeros((depth, bp, GDN_HEADS, GDN_DK, GDN_DV), F32),
        jnp.zeros((depth, bp, CONV_W - 1, GDN_CONV_CH), F32),
        jnp.zeros((depth, bp, RWKV_HEADS, RWKV_N, RWKV_N), F32),
        jnp.zeros((depth, bp, 1, RWKV_SHIFT_CH), F32),
    )
    y_p, st_p = _run_stream(x_prompt, 0, zero_states, wts, gf, SEQ_BLOCK)
    y_s, st_s = _run_stream(x_sample, PAST_LEN, (state_ret, state_delta, state_conv, state_wkv,
                                                 state_shift), wts, gf, SEQ_BLOCK)
    return (y_p, y_s) + st_p + st_s
```
